```python
import jax
import jax.numpy as jnp
from jax import lax

D_MODEL = 2048
BATCH = 2
SEQ = 4096
DEPTH = 4

CHUNK = 64
Q_BLOCK = 128
EPS = 1e-6

DN_QK_HEADS = 16
DN_V_HEADS = 32
DN_HEAD_DIM = 128
DN_QK_DIM = DN_QK_HEADS * DN_HEAD_DIM
DN_V_DIM = DN_V_HEADS * DN_HEAD_DIM
DN_CONV_DIM = 2 * DN_QK_DIM + DN_V_DIM
CONV_K = 4

FOX_HEADS = 16
FOX_HEAD_DIM = 128
FOX_DIM = FOX_HEADS * FOX_HEAD_DIM

N_BRANCH = 2
FFN_HIDDEN = -(-8 * D_MODEL // (3 * 256)) * 256

IN_SIZES = (DN_CONV_DIM, DN_V_DIM, DN_V_HEADS, DN_V_HEADS, 3 * FOX_DIM, FOX_HEADS, N_BRANCH * D_MODEL)
N_IN = sum(IN_SIZES)
IN_OFFSETS = tuple(sum(IN_SIZES[:i + 1]) for i in range(len(IN_SIZES) - 1))

kernel_name = 'hybrid_deltanet_fox_adaln_block'


def rmsnorm(x, gain):
    xf = x.astype(jnp.float32)
    y = xf * lax.rsqrt(jnp.mean(xf * xf, axis=-1, keepdims=True) + EPS)
    return (y * gain.astype(jnp.float32)).astype(x.dtype)


def l2norm(x):
    return x * lax.rsqrt(jnp.sum(x * x, axis=-1, keepdims=True) + EPS)


def causal_short_conv(u, w):
    t = u.shape[1]
    up = jnp.pad(u, ((0, 0), (CONV_K - 1, 0), (0, 0)))
    out = up[:, 0:t, :] * w[:, 0]
    for j in range(1, CONV_K):
        out = out + up[:, j:j + t, :] * w[:, j]
    return out


def chunk_gated_delta_rule(q, k, v, g, beta):
    b, t, h, dk = q.shape
    dv = v.shape[-1]
    n = t // CHUNK
    f32 = jnp.float32
    q = l2norm(q.astype(f32)) * dk ** -0.5
    k = l2norm(k.astype(f32))
    v = v.astype(f32)

    def blocks(a):
        return a.reshape(b, n, CHUNK, h, -1).transpose(0, 3, 1, 2, 4)

    q, k, v = blocks(q), blocks(k), blocks(v)
    beta = beta.astype(f32).reshape(b, n, CHUNK, h).transpose(0, 3, 1, 2)
    gc = jnp.cumsum(g.astype(f32).reshape(b, n, CHUNK, h).transpose(0, 3, 1, 2), axis=-1)
    causal = jnp.tril(jnp.ones((CHUNK, CHUNK), dtype=bool))
    strict = jnp.tril(jnp.ones((CHUNK, CHUNK), dtype=bool), -1)
    decay = jnp.exp(jnp.where(causal, gc[..., :, None] - gc[..., None, :], -jnp.inf))
    k_beta = k * beta[..., None]
    v_beta = v * beta[..., None]
    lower = jnp.where(strict, jnp.einsum('bhncd,bhnsd->bhncs', k_beta, k) * decay, 0.0)
    a_mat = lower + jnp.eye(CHUNK, dtype=f32)
    u = lax.linalg.triangular_solve(a_mat, v_beta, left_side=True, lower=True, unit_diagonal=True)
    w = lax.linalg.triangular_solve(a_mat, k_beta * jnp.exp(gc)[..., None], left_side=True, lower=True,
                                    unit_diagonal=True)
    attn = jnp.einsum('bhncd,bhnsd->bhncs', q, k) * decay

    def step(state, xs):
        q_i, k_i, u_i, w_i, attn_i, gc_i = xs
        v_new = u_i - jnp.einsum('bhcd,bhde->bhce', w_i, state)
        o = (jnp.einsum('bhcd,bhde->bhce', q_i * jnp.exp(gc_i)[..., None], state)
             + jnp.einsum('bhcs,bhse->bhce', attn_i, v_new))
        g_last = gc_i[..., -1]
        state = (state * jnp.exp(g_last)[..., None, None]
                 + jnp.einsum('bhcd,bhce->bhde', k_i * jnp.exp(g_last[..., None] - gc_i)[..., None], v_new))
        return state, o

    xs = tuple(jnp.moveaxis(a, 2, 0) for a in (q, k, u, w, attn, gc))
    _, o = lax.scan(step, jnp.zeros((b, h, dk, dv), f32), xs)
    return o.transpose(1, 0, 3, 2, 4).reshape(b, t, h, dv)


def forgetting_attention(q, k, v, log_f):
    b, t, h, d = q.shape
    f32 = jnp.float32
    q, k, v = q.astype(f32), k.astype(f32), v.astype(f32)
    cum_f = jnp.cumsum(log_f.astype(f32), axis=1)
    nb = t // Q_BLOCK
    qb = q.reshape(b, nb, Q_BLOCK, h, d).transpose(1, 0, 3, 2, 4)
    fq = cum_f.reshape(b, nb, Q_BLOCK, h).transpose(1, 0, 3, 2)
    fk = cum_f.transpose(0, 2, 1)
    pos_k = jnp.arange(t)
    scale = d ** -0.5

    def block(args):
        i, q_i, f_i = args
        s = jnp.einsum('bhqd,bkhd->bhqk', q_i, k) * scale + f_i[..., None] - fk[:, :, None, :]
        pos_q = i * Q_BLOCK + jnp.arange(Q_BLOCK)
        s = jnp.where(pos_k[None, :] <= pos_q[:, None], s, -jnp.inf)
        p = jax.nn.softmax(s, axis=-1)
        return jnp.einsum('bhqk,bkhd->bqhd', p, v)

    o = lax.map(block, (jnp.arange(nb), qb, fq))
    return o.transpose(1, 0, 2, 3, 4).reshape(b, t, h, d)


def hybrid_mixer(h, w_in, conv_w, a_log, dt_bias, norm_w, f_bias, w_br_dn, w_br_fox, w_out):
    b, t, _ = h.shape
    f32 = jnp.float32
    proj = h @ w_in
    dn_qkv, dn_z, dn_b, dn_a, fox_qkv, fox_f, merge = jnp.split(proj, IN_OFFSETS, axis=-1)

    dn_qkv = jax.nn.silu(causal_short_conv(dn_qkv, conv_w))
    q, k, v = jnp.split(dn_qkv, (DN_QK_DIM, 2 * DN_QK_DIM), axis=-1)
    rep = DN_V_HEADS // DN_QK_HEADS
    q = jnp.repeat(q.reshape(b, t, DN_QK_HEADS, DN_HEAD_DIM), rep, axis=2)
    k = jnp.repeat(k.reshape(b, t, DN_QK_HEADS, DN_HEAD_DIM), rep, axis=2)
    v = v.reshape(b, t, DN_V_HEADS, DN_HEAD_DIM)
    beta = jax.nn.sigmoid(dn_b.astype(f32))
    g = -jnp.exp(a_log.astype(f32)) * jax.nn.softplus(dn_a.astype(f32) + dt_bias.astype(f32))
    o_dn = chunk_gated_delta_rule(q, k, v, g, beta)
    z = dn_z.reshape(b, t, DN_V_HEADS, DN_HEAD_DIM).astype(f32)
    o_dn = rmsnorm(o_dn, norm_w) * jax.nn.silu(z)
    y_dn = o_dn.reshape(b, t, DN_V_DIM).astype(h.dtype) @ w_br_dn

    fq, fk, fv = jnp.split(fox_qkv, 3, axis=-1)
    shp = (b, t, FOX_HEADS, FOX_HEAD_DIM)
    log_f = jax.nn.log_sigmoid(fox_f.astype(f32) + f_bias.astype(f32))
    o_fox = forgetting_attention(fq.reshape(shp), fk.reshape(shp), fv.reshape(shp), log_f)
    y_fox = o_fox.reshape(b, t, FOX_DIM).astype(h.dtype) @ w_br_fox

    gate_dn, gate_fox = jnp.split(jax.nn.sigmoid(merge), N_BRANCH, axis=-1)
    return (gate_dn * y_dn + gate_fox * y_fox) @ w_out


def swiglu(h, w_gate, w_up, w_down):
    return (jax.nn.silu(h @ w_gate) * (h @ w_up)) @ w_down


def setup_inputs(seed: int = 0) -> dict:
    key = jax.random.key(seed)
    ks = jax.random.split(key, 20)
    L, D, F = DEPTH, D_MODEL, FFN_HIDDEN
    f32 = jnp.float32

    def nrm(k, shape, fan_in):
        return jax.random.normal(k, shape, f32) * fan_in ** -0.5

    x = jax.random.normal(ks[0], (BATCH, SEQ, D), f32)
    c = jax.random.normal(ks[1], (BATCH, D), f32)
    w_ada = nrm(ks[2], (L, D, 6 * D), D)
    b_ada = 0.02 * jax.random.normal(ks[3], (L, 6 * D), f32)
    norm_gains = 1.0 + 0.05 * jax.random.normal(ks[4], (L, 4, D), f32)
    w_in = nrm(ks[5], (L, D, N_IN), D)
    dn_conv = nrm(ks[6], (L, DN_CONV_DIM, CONV_K), CONV_K)
    dn_a_log = jnp.log(jax.random.uniform(ks[7], (L, DN_V_HEADS), f32, 1.0, 16.0))
    dt = jnp.exp(jax.random.uniform(ks[8], (L, DN_V_HEADS), f32, jnp.log(1e-3), jnp.log(1e-1)))
    dn_dt_bias = dt + jnp.log(-jnp.expm1(-dt))
    dn_norm_w = 1.0 + 0.05 * jax.random.normal(ks[9], (L, DN_HEAD_DIM), f32)
    fox_f_bias = jax.random.uniform(ks[10], (L, FOX_HEADS), f32, 1.0, 4.0)
    w_branch_dn = nrm(ks[11], (L, DN_V_DIM, D), DN_V_DIM)
    w_branch_fox = nrm(ks[12], (L, FOX_DIM, D), FOX_DIM)
    w_out = nrm(ks[13], (L, D, D), D)
    w_gate = nrm(ks[14], (L, D, F), D)
    w_up = nrm(ks[15], (L, D, F), D)
    w_down = nrm(ks[16], (L, F, D), F)
    return {'x': x, 'c': c, 'w_ada': w_ada, 'b_ada': b_ada, 'norm_gains': norm_gains, 'w_in': w_in,
            'dn_conv': dn_conv, 'dn_a_log': dn_a_log, 'dn_dt_bias': dn_dt_bias, 'dn_norm_w': dn_norm_w,
            'fox_f_bias': fox_f_bias, 'w_branch_dn': w_branch_dn, 'w_branch_fox': w_branch_fox,
            'w_out': w_out, 'w_gate': w_gate, 'w_up': w_up, 'w_down': w_down}


def reference(x, c, w_ada, b_ada, norm_gains, w_in, dn_conv, dn_a_log, dn_dt_bias, dn_norm_w,
              fox_f_bias, w_branch_dn, w_branch_fox, w_out, w_gate, w_up, w_down):
    cond = jax.nn.silu(c)
    for l in range(DEPTH):
        mod = (cond @ w_ada[l] + b_ada[l])[:, None, :]
        shift_m, scale_m, gate_m, shift_f, scale_f, gate_f = jnp.split(mod, 6, axis=-1)
        h = rmsnorm(x, norm_gains[l, 0]) * (1.0 + scale_m) + shift_m
        y = hybrid_mixer(h, w_in[l], dn_conv[l], dn_a_log[l], dn_dt_bias[l], dn_norm_w[l],
                         fox_f_bias[l], w_branch_dn[l], w_branch_fox[l], w_out[l])
        x = x + gate_m * rmsnorm(y, norm_gains[l, 1])
        h = rmsnorm(x, norm_gains[l, 2]) * (1.0 + scale_f) + shift_f
        y = swiglu(h, w_gate[l], w_up[l], w_down[l])
        x = x + gate_f * rmsnorm(y, norm_gains[l, 3])
    return x
```

```python
import functools

import jax
import jax.numpy as jnp
from jax import lax
from jax.experimental import pallas as pl
from jax.experimental.pallas import tpu as pltpu

EPS = 1e-6
HEAD_DIM = 128
CONV_K = 4
DN_CHUNK = 128
GATE_LANES = 128
VMEM_LIMIT_BYTES = 48 * 1024 * 1024

F32 = jnp.float32
BF16 = jnp.bfloat16


def _params(*sem):
    return pltpu.CompilerParams(dimension_semantics=sem, vmem_limit_bytes=VMEM_LIMIT_BYTES)


def _tile(dim, pref):
    if dim <= pref:
        return dim
    t = pref - pref % HEAD_DIM
    while dim % t:
        t -= HEAD_DIM
    assert t > 0, (dim, pref)
    return t


def _dot(a, b):
    return jnp.dot(a.astype(BF16), b.astype(BF16), preferred_element_type=F32)


def _dot_nt(a, b):
    return lax.dot_general(a.astype(BF16), b.astype(BF16), (((1,), (1,)), ((), ())),
                           preferred_element_type=F32)


def _dot_tn(a, b):
    return lax.dot_general(a.astype(BF16), b.astype(BF16), (((0,), (0,)), ((), ())),
                           preferred_element_type=F32)


def _sigmoid(x):
    return 1.0 / (1.0 + jnp.exp(-x))


def _rms(x):
    return x * lax.rsqrt(jnp.mean(x * x, axis=-1, keepdims=True) + EPS)


def _mod_kernel(c_ref, w_ref, b_ref, o_ref):
    c = c_ref[...]
    cond = c * _sigmoid(c)
    o_ref[0] = _dot(cond, w_ref[0]) + b_ref[0]


def _modulation(c, w_ada, b_ada):
    depth, d, n = w_ada.shape
    b = c.shape[0]
    rows = 16
    c_pad = jnp.zeros((rows, d), F32).at[:b].set(c)
    tn = _tile(n, 1024)
    out = pl.pallas_call(
        _mod_kernel,
        grid=(depth, n // tn),
        in_specs=[pl.BlockSpec((rows, d), lambda l, j: (0, 0)),
                  pl.BlockSpec((1, d, tn), lambda l, j: (l, 0, j)),
                  pl.BlockSpec((1, 1, tn), lambda l, j: (l, 0, j))],
        out_specs=pl.BlockSpec((1, rows, tn), lambda l, j: (l, 0, j)),
        out_shape=jax.ShapeDtypeStruct((depth, rows, n), F32),
        compiler_params=_params("parallel", "parallel"),
        name="adaln_mod",
    )(c_pad, w_ada, b_ada.reshape(depth, 1, n))
    return out[:, :b].reshape(depth, b, 6, d)


def _norm_mod_kernel(x_ref, g_ref, sc_ref, sh_ref, o_ref):
    y = _rms(x_ref[...]) * g_ref[...]
    o_ref[...] = (y * (1.0 + sc_ref[0]) + sh_ref[0]).astype(o_ref.dtype)


def _norm_mod(x2, gain, mod3, sc_idx, sh_idx, seq):
    m, d = x2.shape
    tm = _tile(seq, 512)
    per_b = seq // tm
    return pl.pallas_call(
        _norm_mod_kernel,
        grid=(m // tm,),
        in_specs=[pl.BlockSpec((tm, d), lambda i: (i, 0)),
                  pl.BlockSpec((1, d), lambda i: (0, 0)),
                  pl.BlockSpec((1, 1, d), lambda i: ((i // per_b) * 6 + sc_idx, 0, 0)),
                  pl.BlockSpec((1, 1, d), lambda i: ((i // per_b) * 6 + sh_idx, 0, 0))],
        out_specs=pl.BlockSpec((tm, d), lambda i: (i, 0)),
        out_shape=jax.ShapeDtypeStruct((m, d), BF16),
        compiler_params=_params("parallel"),
        name="norm_mod",
    )(x2, gain.reshape(1, d), mod3, mod3)


def _mm_kernel(a_ref, b_ref, o_ref):
    o_ref[...] = _dot(a_ref[...], b_ref[...]).astype(o_ref.dtype)


def _matmul(a, b):
    m, k = a.shape
    n = b.shape[1]
    tm, tn = _tile(m, 1024), _tile(n, 1024)
    return pl.pallas_call(
        _mm_kernel,
        grid=(m // tm, n // tn),
        in_specs=[pl.BlockSpec((tm, k), lambda i, j: (i, 0)),
                  pl.BlockSpec((k, tn), lambda i, j: (0, j))],
        out_specs=pl.BlockSpec((tm, tn), lambda i, j: (i, j)),
        out_shape=jax.ShapeDtypeStruct((m, n), BF16),
        compiler_params=_params("parallel", "parallel"),
        name="in_proj",
    )(a, b)


def _split3(x):
    hi = x.astype(BF16)
    r1 = x - hi.astype(F32)
    mid = r1.astype(BF16)
    lo = (r1 - mid.astype(F32)).astype(BF16)
    return hi, mid, lo


def _gate_act(x, bias, mult, idx, hv, hf):
    xb = x + bias
    e = jnp.exp(-jnp.abs(xb))
    l1p = jnp.log(1.0 + e)
    sig = jnp.where(xb >= 0, 1.0, e) / (1.0 + e)
    softplus = jnp.maximum(xb, 0.0) + l1p
    logsig = jnp.minimum(xb, 0.0) - l1p
    return jnp.where(idx < hv, sig,
                     jnp.where(idx < 2 * hv, mult * softplus,
                               jnp.where(idx < 2 * hv + hf, logsig, 0.0)))


def _gates_kernel(h_ref, wg_ref, wgt_ref, pcol_ref, prow_ref, cols_ref, rows_ref, carry_ref,
                  *, hv, hf, chunk):
    t = pl.program_id(1)
    tm = h_ref.shape[0]

    @pl.when(t == 0)
    def _():
        carry_ref[...] = jnp.zeros_like(carry_ref)

    h = h_ref[...]
    g_cols = _dot(h, wg_ref[...])
    g_rows = _dot_nt(wgt_ref[...], h)

    lane = lax.broadcasted_iota(jnp.int32, (tm, GATE_LANES), 1)
    val_c = _gate_act(g_cols, pcol_ref[0:1, :], pcol_ref[1:2, :], lane, hv, hf)
    sub = lax.broadcasted_iota(jnp.int32, (GATE_LANES, tm), 0)
    val_r = _gate_act(g_rows, prow_ref[:, 0:1], prow_ref[:, 1:2], sub, hv, hf)

    r = lax.broadcasted_iota(jnp.int32, (tm, tm), 0)
    c = lax.broadcasted_iota(jnp.int32, (tm, tm), 1)
    sh = chunk.bit_length() - 1
    same = (r >> sh) == (c >> sh)
    tril_blk = jnp.where((r >= c) & same, 1.0, 0.0).astype(BF16)
    triu_blk = jnp.where((r <= c) & same, 1.0, 0.0).astype(BF16)
    triu_all = jnp.where(r <= c, 1.0, 0.0).astype(BF16)

    cum_c = sum(jnp.dot(tril_blk, p, preferred_element_type=F32) for p in _split3(val_c))
    pieces_r = _split3(val_r)
    cum_r_blk = sum(jnp.dot(p, triu_blk, preferred_element_type=F32) for p in pieces_r)
    cum_r_all = sum(jnp.dot(p, triu_all, preferred_element_type=F32) for p in pieces_r)
    cum_r_all = cum_r_all + carry_ref[:, 0:1]

    is_decay_c = (lane >= hv) & (lane < 2 * hv)
    cols_ref[0] = jnp.where(is_decay_c, cum_c, val_c)
    is_decay_r = (sub >= hv) & (sub < 2 * hv)
    is_forget_r = (sub >= 2 * hv) & (sub < 2 * hv + hf)
    rows = jnp.where(is_decay_r, cum_r_blk, jnp.where(is_forget_r, cum_r_all, val_r))
    rows_ref[0] = rows
    carry_ref[...] = jnp.broadcast_to(cum_r_all[:, tm - 1:tm], carry_ref.shape)


def _gates(h2, wg, wgt, pcol, prow, batch, seq, hv, hf):
    m, d = h2.shape
    tm = _tile(seq, 512)
    per_b = seq // tm
    kern = functools.partial(_gates_kernel, hv=hv, hf=hf, chunk=DN_CHUNK)
    return pl.pallas_call(
        kern,
        grid=(batch, per_b),
        in_specs=[pl.BlockSpec((tm, d), lambda b, t: (b * per_b + t, 0)),
                  pl.BlockSpec((d, GATE_LANES), lambda b, t: (0, 0)),
                  pl.BlockSpec((GATE_LANES, d), lambda b, t: (0, 0)),
                  pl.BlockSpec((2, GATE_LANES), lambda b, t: (0, 0)),
                  pl.BlockSpec((GATE_LANES, 2), lambda b, t: (0, 0))],
        out_specs=[pl.BlockSpec((1, tm, GATE_LANES), lambda b, t: (b, t, 0)),
                   pl.BlockSpec((1, GATE_LANES, tm), lambda b, t: (b, 0, t))],
        out_shape=[jax.ShapeDtypeStruct((batch, seq, GATE_LANES), F32),
                   jax.ShapeDtypeStruct((batch, GATE_LANES, seq), F32)],
        scratch_shapes=[pltpu.VMEM((GATE_LANES, GATE_LANES), F32)],
        compiler_params=_params("parallel", "arbitrary"),
        name="gates",
    )(h2, wg, wgt, pcol, prow)


def _causal_conv(u, tail, w):
    out = u * w[CONV_K - 1:CONV_K, :]
    row8 = lax.broadcasted_iota(jnp.int32, tail.shape, 0)
    for s in range(1, CONV_K):
        rolled = pltpu.roll(u, s, 0)
        head = jnp.where(row8 < s, pltpu.roll(tail, s, 0), rolled[0:8])
        shifted = jnp.concatenate([head, rolled[8:]], axis=0)
        out = out + shifted * w[CONV_K - 1 - s:CONV_K - s, :]
    return out


def _unit_lower_inverse(lm):
    n = lm.shape[0]
    r = lax.broadcasted_iota(jnp.int32, (n, n), 0)
    c = lax.broadcasted_iota(jnp.int32, (n, n), 1)
    n0 = jnp.where((r >> 3) == (c >> 3), -lm, 0.0)
    n2 = _dot(n0, n0)
    n4 = _dot(n2, n2)
    p = jnp.where(r == c, 1.0, 0.0) + n0
    p = p + _dot(p, n2)
    p = p + _dot(p, n4)
    s = 8
    while s < n:
        sh = s.bit_length() - 1
        mask = ((r >> (sh + 1)) == (c >> (sh + 1))) & ((r >> sh) == (c >> sh) + 1)
        a = jnp.where(mask, lm, 0.0)
        p = p - _dot(p, _dot(a, p))
        s *= 2
    return p


def _deltanet_kernel(q_ref, k_ref, v_ref, z_ref, wq_ref, wk_ref, wv_ref, cols_ref, rows_ref, nw_ref,
                     o_ref, s_ref, qt_ref, kt_ref, vt_ref, *, hv, rep):
    i = pl.program_id(1)
    tb = pl.program_id(2)
    n_chunks = q_ref.shape[0] // DN_CHUNK
    c_ = DN_CHUNK

    @pl.when(tb == 0)
    def _():
        s_ref[...] = jnp.zeros_like(s_ref)
        qt_ref[...] = jnp.zeros_like(qt_ref)
        kt_ref[...] = jnp.zeros_like(kt_ref)
        vt_ref[...] = jnp.zeros_like(vt_ref)

    r = lax.broadcasted_iota(jnp.int32, (c_, c_), 0)
    c = lax.broadcasted_iota(jnp.int32, (c_, c_), 1)
    lane = lax.broadcasted_iota(jnp.int32, (c_, GATE_LANES), 1)
    nw = nw_ref[...]

    def chunk_body(ci, carry):
        off = pl.multiple_of(ci * c_, c_)
        sl = pl.ds(off, c_)
        q_raw = q_ref[sl, :].astype(F32)
        k_raw = k_ref[sl, :].astype(F32)
        v_raw = v_ref[sl, :].astype(F32)
        qc = _causal_conv(q_raw, qt_ref[...], wq_ref[...])
        kc = _causal_conv(k_raw, kt_ref[...], wk_ref[...])
        vc = _causal_conv(v_raw, vt_ref[...], wv_ref[...])
        qt_ref[...] = q_raw[c_ - 8:, :]
        kt_ref[...] = k_raw[c_ - 8:, :]
        vt_ref[...] = v_raw[c_ - 8:, :]
        qc = qc * _sigmoid(qc)
        kc = kc * _sigmoid(kc)
        vc = vc * _sigmoid(vc)
        qn = qc * lax.rsqrt(jnp.sum(qc * qc, axis=-1, keepdims=True) + EPS) * (HEAD_DIM ** -0.5)
        kn = kc * lax.rsqrt(jnp.sum(kc * kc, axis=-1, keepdims=True) + EPS)
        kk = _dot_nt(kn, kn)
        qk = _dot_nt(qn, kn)
        cols = cols_ref[0, sl, :]
        z_all = z_ref[sl, :].astype(F32)
        row0 = pl.multiple_of(((hv + i * rep) // 8) * 8, 8)
        rows8 = rows_ref[0, pl.ds(row0, 8), sl]
        sub8 = lax.broadcasted_iota(jnp.int32, (8, c_), 0)

        for hh in range(rep):
            head = i * rep + hh
            beta_c = jnp.sum(jnp.where(lane == head, cols, 0.0), axis=1, keepdims=True)
            gc_c = jnp.sum(jnp.where(lane == hv + head, cols, 0.0), axis=1, keepdims=True)
            gc_r = jnp.sum(jnp.where(sub8 == (hv + head) % 8, rows8, 0.0), axis=0, keepdims=True)
            g_last = gc_r[:, c_ - 1:c_]
            decay = jnp.exp(jnp.where(r >= c, gc_c - gc_r, -1e30))
            lm = jnp.where(r > c, beta_c * kk * decay, 0.0)
            attn = qk * decay
            tinv = _unit_lower_inverse(lm)
            v_h = vc[:, hh * HEAD_DIM:(hh + 1) * HEAD_DIM]
            eg = jnp.exp(gc_c)
            rhs = jnp.concatenate([v_h * beta_c, kn * (beta_c * eg)], axis=1)
            uw = _dot(tinv, rhs)
            u, w = uw[:, :HEAD_DIM], uw[:, HEAD_DIM:]
            state = s_ref[hh]
            ws_qs = _dot(jnp.concatenate([w, qn * eg], axis=0), state)
            v_new = u - ws_qs[:c_]
            o = ws_qs[c_:] + _dot(attn, v_new)
            kd = kn * jnp.exp(g_last - gc_c)
            s_ref[hh] = state * jnp.exp(g_last) + _dot_tn(kd, v_new)
            z = z_all[:, hh * HEAD_DIM:(hh + 1) * HEAD_DIM]
            out = _rms(o) * nw * (z * _sigmoid(z))
            o_ref[sl, hh * HEAD_DIM:(hh + 1) * HEAD_DIM] = out.astype(o_ref.dtype)
        return carry

    lax.fori_loop(0, n_chunks, chunk_body, 0)


def _deltanet(proj, conv_wt, cols, rows, norm_w, batch, seq, hqk, hv):
    rep = hv // hqk
    assert rep == 2 and DN_CHUNK == HEAD_DIM
    tb = _tile(seq, 1024)
    per_b = seq // tb
    w2 = rep * HEAD_DIM
    row = lambda b, i, t: b * per_b + t
    kern = functools.partial(_deltanet_kernel, hv=hv, rep=rep)
    return pl.pallas_call(
        kern,
        grid=(batch, hqk, per_b),
        in_specs=[pl.BlockSpec((tb, HEAD_DIM), lambda b, i, t: (row(b, i, t), i)),
                  pl.BlockSpec((tb, HEAD_DIM), lambda b, i, t: (row(b, i, t), hqk + i)),
                  pl.BlockSpec((tb, w2), lambda b, i, t: (row(b, i, t), hqk + i)),
                  pl.BlockSpec((tb, w2), lambda b, i, t: (row(b, i, t), hqk + hv // rep + i)),
                  pl.BlockSpec((CONV_K, HEAD_DIM), lambda b, i, t: (0, i)),
                  pl.BlockSpec((CONV_K, HEAD_DIM), lambda b, i, t: (0, hqk + i)),
                  pl.BlockSpec((CONV_K, w2), lambda b, i, t: (0, hqk + i)),
                  pl.BlockSpec((1, tb, GATE_LANES), lambda b, i, t: (b, t, 0)),
                  pl.BlockSpec((1, GATE_LANES, tb), lambda b, i, t: (b, 0, t)),
                  pl.BlockSpec((1, HEAD_DIM), lambda b, i, t: (0, 0))],
        out_specs=pl.BlockSpec((tb, w2), lambda b, i, t: (row(b, i, t), i)),
        out_shape=jax.ShapeDtypeStruct((batch * seq, hv * HEAD_DIM), BF16),
        scratch_shapes=[pltpu.VMEM((rep, HEAD_DIM, HEAD_DIM), F32),
                        pltpu.VMEM((8, HEAD_DIM), F32),
                        pltpu.VMEM((8, HEAD_DIM), F32),
                        pltpu.VMEM((8, w2), F32)],
        compiler_params=_params("parallel", "parallel", "arbitrary"),
        name="deltanet",
    )(proj, proj, proj, proj, conv_wt, conv_wt, conv_wt, cols, rows, norm_w.reshape(1, HEAD_DIM))


def _fox_kernel(q_ref, k_ref, v_ref, f_ref, o_ref, m_ref, l_ref, acc_ref, *, scale):
    qi = pl.program_id(2)
    kj = pl.program_id(3)
    nk = pl.num_programs(3)
    tq, tk = q_ref.shape[0], k_ref.shape[0]

    @pl.when(kj == 0)
    def _():
        m_ref[...] = jnp.full_like(m_ref, -1e30)
        l_ref[...] = jnp.zeros_like(l_ref)
        acc_ref[...] = jnp.zeros_like(acc_ref)

    def step(masked):
        s = _dot_nt(q_ref[...], k_ref[...]) * scale - f_ref[0, 0]
        if masked:
            r = lax.broadcasted_iota(jnp.int32, (tq, tk), 0)
            c = lax.broadcasted_iota(jnp.int32, (tq, tk), 1)
            s = jnp.where(r >= c, s, -1e30)
        m_prev = m_ref[...]
        m_new = jnp.maximum(m_prev, jnp.max(s, axis=-1, keepdims=True))
        p = jnp.exp(s - m_new[:, 0:1])
        alpha = jnp.exp(m_prev - m_new)
        l_ref[...] = alpha * l_ref[...] + jnp.sum(p, axis=-1, keepdims=True)
        acc_ref[...] = alpha[:, 0:1] * acc_ref[...] + _dot(p, v_ref[...])
        m_ref[...] = m_new

    @pl.when(kj < qi)
    def _():
        step(False)

    @pl.when(kj == qi)
    def _():
        step(True)

    @pl.when(kj == nk - 1)
    def _():
        o_ref[...] = (acc_ref[...] / l_ref[:, 0:1]).astype(o_ref.dtype)


def _fox(proj, f_rows, batch, seq, base, hf):
    tq = _tile(seq, 512)
    nq = seq // tq
    kern = functools.partial(_fox_kernel, scale=HEAD_DIM ** -0.5)
    kv_row = lambda b, h, qi, kj: b * nq + jnp.minimum(kj, qi)
    return pl.pallas_call(
        kern,
        grid=(batch, hf, nq, nq),
        in_specs=[pl.BlockSpec((tq, HEAD_DIM), lambda b, h, qi, kj: (b * nq + qi, base + h)),
                  pl.BlockSpec((tq, HEAD_DIM), lambda b, h, qi, kj: (kv_row(b, h, qi, kj), base + hf + h)),
                  pl.BlockSpec((tq, HEAD_DIM), lambda b, h, qi, kj: (kv_row(b, h, qi, kj), base + 2 * hf + h)),
                  pl.BlockSpec((1, 1, 1, tq), lambda b, h, qi, kj: (b, h, 0, jnp.minimum(kj, qi)))],
        out_specs=pl.BlockSpec((tq, HEAD_DIM), lambda b, h, qi, kj: (b * nq + qi, h)),
        out_shape=jax.ShapeDtypeStruct((batch * seq, hf * HEAD_DIM), BF16),
        scratch_shapes=[pltpu.VMEM((tq, HEAD_DIM), F32),
                        pltpu.VMEM((tq, HEAD_DIM), F32),
                        pltpu.VMEM((tq, HEAD_DIM), F32)],
        compiler_params=_params("parallel", "parallel", "parallel", "arbitrary"),
        name="fox_attention",
    )(proj, proj, proj, f_rows)


def _merge_kernel(a1_ref, w1_ref, a2_ref, w2_ref, m1_ref, m2_ref, o_ref):
    y1 = _dot(a1_ref[...], w1_ref[...])
    y2 = _dot(a2_ref[...], w2_ref[...])
    g1 = _sigmoid(m1_ref[...].astype(F32))
    g2 = _sigmoid(m2_ref[...].astype(F32))
    o_ref[...] = (g1 * y1 + g2 * y2).astype(o_ref.dtype)


def _merge(o_dn, w_dn, o_fox, w_fox, proj, merge_base_cols):
    m, k1 = o_dn.shape
    k2 = o_fox.shape[1]
    d = w_dn.shape[1]
    tm, tn = _tile(m, 512), _tile(d, 512)
    assert merge_base_cols % tn == 0
    mb = merge_base_cols // tn
    return pl.pallas_call(
        _merge_kernel,
        grid=(m // tm, d // tn),
        in_specs=[pl.BlockSpec((tm, k1), lambda i, j: (i, 0)),
                  pl.BlockSpec((k1, tn), lambda i, j: (0, j)),
                  pl.BlockSpec((tm, k2), lambda i, j: (i, 0)),
                  pl.BlockSpec((k2, tn), lambda i, j: (0, j)),
                  pl.BlockSpec((tm, tn), lambda i, j: (i, mb + j)),
                  pl.BlockSpec((tm, tn), lambda i, j: (i, mb + d // tn + j))],
        out_specs=pl.BlockSpec((tm, tn), lambda i, j: (i, j)),
        out_shape=jax.ShapeDtypeStruct((m, d), BF16),
        compiler_params=_params("parallel", "parallel"),
        name="branch_merge",
    )(o_dn, w_dn, o_fox, w_fox, proj, proj)


def _out_res_kernel(a_ref, w_ref, x_ref, g_ref, gate_ref, o_ref, acc_ref):
    k = pl.program_id(1)

    @pl.when(k == 0)
    def _():
        acc_ref[...] = jnp.zeros_like(acc_ref)

    acc_ref[...] += _dot(a_ref[...], w_ref[...])

    @pl.when(k == pl.num_programs(1) - 1)
    def _():
        y = _rms(acc_ref[...]) * g_ref[...]
        o_ref[...] = x_ref[...] + gate_ref[0] * y


def _out_residual(a, w, x2, gain, mod3, gate_idx, seq):
    m, k = a.shape
    d = w.shape[1]
    tm = _tile(seq, 512)
    tk = _tile(k, 512)
    per_b = seq // tm
    return pl.pallas_call(
        _out_res_kernel,
        grid=(m // tm, k // tk),
        in_specs=[pl.BlockSpec((tm, tk), lambda i, kk: (i, kk)),
                  pl.BlockSpec((tk, d), lambda i, kk: (kk, 0)),
                  pl.BlockSpec((tm, d), lambda i, kk: (i, 0)),
                  pl.BlockSpec((1, d), lambda i, kk: (0, 0)),
                  pl.BlockSpec((1, 1, d), lambda i, kk: ((i // per_b) * 6 + gate_idx, 0, 0))],
        out_specs=pl.BlockSpec((tm, d), lambda i, kk: (i, 0)),
        out_shape=jax.ShapeDtypeStruct((m, d), F32),
        scratch_shapes=[pltpu.VMEM((tm, d), F32)],
        compiler_params=_params("parallel", "arbitrary"),
        name="proj_residual",
    )(a, w, x2, gain.reshape(1, d), mod3)


def _glu_kernel(a_ref, wg_ref, wu_ref, o_ref):
    a = a_ref[...]
    g = _dot(a, wg_ref[...])
    u = _dot(a, wu_ref[...])
    o_ref[...] = (g * _sigmoid(g) * u).astype(o_ref.dtype)


def _glu(a, wg, wu):
    m, k = a.shape
    n = wg.shape[1]
    tm, tn = _tile(m, 1024), _tile(n, 512)
    return pl.pallas_call(
        _glu_kernel,
        grid=(m // tm, n // tn),
        in_specs=[pl.BlockSpec((tm, k), lambda i, j: (i, 0)),
                  pl.BlockSpec((k, tn), lambda i, j: (0, j)),
                  pl.BlockSpec((k, tn), lambda i, j: (0, j))],
        out_specs=pl.BlockSpec((tm, tn), lambda i, j: (i, j)),
        out_shape=jax.ShapeDtypeStruct((m, n), BF16),
        compiler_params=_params("parallel", "parallel"),
        name="swiglu_up",
    )(a, wg, wu)


def kernel(x, c, w_ada, b_ada, norm_gains, w_in, dn_conv, dn_a_log, dn_dt_bias, dn_norm_w, fox_f_bias,
           w_branch_dn, w_branch_fox, w_out, w_gate, w_up, w_down):
    batch, seq, d = x.shape
    depth = w_ada.shape[0]
    hv = dn_a_log.shape[1]
    hf = fox_f_bias.shape[1]
    v_dim = hv * HEAD_DIM
    conv_dim = dn_conv.shape[1]
    qk_dim = (conv_dim - v_dim) // 2
    hqk = qk_dim // HEAD_DIM
    fox_dim = hf * HEAD_DIM
    assert 2 * hv + hf <= GATE_LANES

    o_z = conv_dim
    o_b = o_z + v_dim
    o_a = o_b + hv
    o_fq = o_a + hv
    o_ff = o_fq + 3 * fox_dim
    o_mg = o_ff + hf
    fox_base = (conv_dim + v_dim) // HEAD_DIM
    merge_base = conv_dim + v_dim + 3 * fox_dim

    mod = _modulation(c, w_ada, b_ada)
    x2 = x.reshape(batch * seq, d)
    pad = GATE_LANES - (2 * hv + hf)
    zpad = jnp.zeros((pad,), F32)

    for l in range(depth):
        mod3 = mod[l].reshape(batch * 6, 1, d)
        wl = w_in[l]
        w_main = jnp.concatenate([wl[:, :o_b], wl[:, o_fq:o_ff], wl[:, o_mg:]], axis=1).astype(BF16)
        w_g = jnp.concatenate([wl[:, o_b:o_fq], wl[:, o_ff:o_mg], jnp.zeros((d, pad), F32)], axis=1).astype(BF16)
        bias = jnp.concatenate([jnp.zeros((hv,), F32), dn_dt_bias[l], fox_f_bias[l], zpad])
        mult = jnp.concatenate([jnp.ones((hv,), F32), -jnp.exp(dn_a_log[l]), jnp.ones((hf,), F32), zpad])
        pcol = jnp.stack([bias, mult], axis=0)

        h = _norm_mod(x2, norm_gains[l, 0], mod3, 1, 0, seq)
        proj = _matmul(h, w_main)
        cols, rows = _gates(h, w_g, w_g.T, pcol, pcol.T, batch, seq, hv, hf)
        o_dn = _deltanet(proj, dn_conv[l].T, cols, rows, dn_norm_w[l], batch, seq, hqk, hv)
        f_rows = rows[:, 2 * hv:2 * hv + hf, :].reshape(batch, hf, 1, seq)
        o_fox = _fox(proj, f_rows, batch, seq, fox_base, hf)
        ymix = _merge(o_dn, w_branch_dn[l].astype(BF16), o_fox, w_branch_fox[l].astype(BF16), proj, merge_base)
        x2 = _out_residual(ymix, w_out[l].astype(BF16), x2, norm_gains[l, 1], mod3, 2, seq)

        h = _norm_mod(x2, norm_gains[l, 2], mod3, 4, 3, seq)
        gu = _glu(h, w_gate[l].astype(BF16), w_up[l].astype(BF16))
        x2 = _out_residual(gu, w_down[l].astype(BF16), x2, norm_gains[l, 3], mod3, 5, seq)

    return x2.reshape(batch, seq, d)
```

```python
import functools

import jax
import jax.numpy as jnp
from jax import lax
from jax.experimental import pallas as pl
from jax.experimental.pallas import tpu as pltpu

EPS = 1e-6
HEAD_DIM = 128
CONV_K = 4
DN_CHUNK = 128
GATE_LANES = 128
VMEM_LIMIT_BYTES = 48 * 1024 * 1024

F32 = jnp.float32
BF16 = jnp.bfloat16


def _params(*sem):
    return pltpu.CompilerParams(dimension_semantics=sem, vmem_limit_bytes=VMEM_LIMIT_BYTES)


def _tile(dim, pref):
    if dim <= pref:
        return dim
    t = pref - pref % HEAD_DIM
    while dim % t:
        t -= HEAD_DIM
    assert t > 0, (dim, pref)
    return t


def _dot(a, b):
    return jnp.dot(a.astype(BF16), b.astype(BF16), preferred_element_type=F32)


def _dot_nt(a, b):
    return lax.dot_general(a.astype(BF16), b.astype(BF16), (((1,), (1,)), ((), ())),
                           preferred_element_type=F32)


def _dot_tn(a, b):
    return lax.dot_general(a.astype(BF16), b.astype(BF16), (((0,), (0,)), ((), ())),
                           preferred_element_type=F32)


def _sigmoid(x):
    return 1.0 / (1.0 + jnp.exp(-x))


def _rms(x):
    return x * lax.rsqrt(jnp.mean(x * x, axis=-1, keepdims=True) + EPS)


def _mod_kernel(c_ref, w_ref, b_ref, o_ref):
    c = c_ref[...]
    cond = c * _sigmoid(c)
    o_ref[0] = _dot(cond, w_ref[0]) + b_ref[0]


def _modulation(c, w_ada, b_ada):
    depth, d, n = w_ada.shape
    b = c.shape[0]
    rows = 16
    c_pad = jnp.zeros((rows, d), F32).at[:b].set(c)
    tn = _tile(n, 1024)
    out = pl.pallas_call(
        _mod_kernel,
        grid=(depth, n // tn),
        in_specs=[pl.BlockSpec((rows, d), lambda l, j: (0, 0)),
                  pl.BlockSpec((1, d, tn), lambda l, j: (l, 0, j)),
                  pl.BlockSpec((1, 1, tn), lambda l, j: (l, 0, j))],
        out_specs=pl.BlockSpec((1, rows, tn), lambda l, j: (l, 0, j)),
        out_shape=jax.ShapeDtypeStruct((depth, rows, n), F32),
        compiler_params=_params("parallel", "parallel"),
        name="adaln_mod",
    )(c_pad, w_ada, b_ada.reshape(depth, 1, n))
    return out[:, :b].reshape(depth, b, 6, d)


def _norm_mod_kernel(x_ref, g_ref, sc_ref, sh_ref, o_ref):
    y = _rms(x_ref[...]) * g_ref[...]
    o_ref[...] = (y * (1.0 + sc_ref[0]) + sh_ref[0]).astype(o_ref.dtype)


def _norm_mod(x2, gain, mod3, sc_idx, sh_idx, seq):
    m, d = x2.shape
    tm = _tile(seq, 512)
    per_b = seq // tm
    return pl.pallas_call(
        _norm_mod_kernel,
        grid=(m // tm,),
        in_specs=[pl.BlockSpec((tm, d), lambda i: (i, 0)),
                  pl.BlockSpec((1, d), lambda i: (0, 0)),
                  pl.BlockSpec((1, 1, d), lambda i: ((i // per_b) * 6 + sc_idx, 0, 0)),
                  pl.BlockSpec((1, 1, d), lambda i: ((i // per_b) * 6 + sh_idx, 0, 0))],
        out_specs=pl.BlockSpec((tm, d), lambda i: (i, 0)),
        out_shape=jax.ShapeDtypeStruct((m, d), BF16),
        compiler_params=_params("parallel"),
        name="norm_mod",
    )(x2, gain.reshape(1, d), mod3, mod3)


def _mm_kernel(a_ref, b_ref, o_ref):
    o_ref[...] = _dot(a_ref[...], b_ref[...]).astype(o_ref.dtype)


def _matmul(a, b):
    m, k = a.shape
    n = b.shape[1]
    tm, tn = _tile(m, 1024), _tile(n, 1024)
    return pl.pallas_call(
        _mm_kernel,
        grid=(m // tm, n // tn),
        in_specs=[pl.BlockSpec((tm, k), lambda i, j: (i, 0)),
                  pl.BlockSpec((k, tn), lambda i, j: (0, j))],
        out_specs=pl.BlockSpec((tm, tn), lambda i, j: (i, j)),
        out_shape=jax.ShapeDtypeStruct((m, n), BF16),
        compiler_params=_params("parallel", "parallel"),
        name="in_proj",
    )(a, b)


def _split3(x):
    hi = x.astype(BF16)
    r1 = x - hi.astype(F32)
    mid = r1.astype(BF16)
    lo = (r1 - mid.astype(F32)).astype(BF16)
    return hi, mid, lo


def _gate_act(x, bias, mult, idx, hv, hf):
    xb = x + bias
    e = jnp.exp(-jnp.abs(xb))
    l1p = jnp.log(1.0 + e)
    sig = jnp.where(xb >= 0, 1.0, e) / (1.0 + e)
    softplus = jnp.maximum(xb, 0.0) + l1p
    logsig = jnp.minimum(xb, 0.0) - l1p
    return jnp.where(idx < hv, sig,
                     jnp.where(idx < 2 * hv, mult * softplus,
                               jnp.where(idx < 2 * hv + hf, logsig, 0.0)))


def _gates_kernel(h_ref, wg_ref, wgt_ref, pcol_ref, prow_ref, cols_ref, rows_ref, carry_ref,
                  *, hv, hf, chunk):
    t = pl.program_id(1)
    tm = h_ref.shape[0]

    @pl.when(t == 0)
    def _():
        carry_ref[...] = jnp.zeros_like(carry_ref)

    h = h_ref[...]
    g_cols = _dot(h, wg_ref[...])
    g_rows = _dot_nt(wgt_ref[...], h)

    lane = lax.broadcasted_iota(jnp.int32, (tm, GATE_LANES), 1)
    val_c = _gate_act(g_cols, pcol_ref[0:1, :], pcol_ref[1:2, :], lane, hv, hf)
    sub = lax.broadcasted_iota(jnp.int32, (GATE_LANES, tm), 0)
    val_r = _gate_act(g_rows, prow_ref[:, 0:1], prow_ref[:, 1:2], sub, hv, hf)

    r = lax.broadcasted_iota(jnp.int32, (tm, tm), 0)
    c = lax.broadcasted_iota(jnp.int32, (tm, tm), 1)
    sh = chunk.bit_length() - 1
    same = (r >> sh) == (c >> sh)
    tril_blk = jnp.where((r >= c) & same, 1.0, 0.0).astype(BF16)
    triu_blk = jnp.where((r <= c) & same, 1.0, 0.0).astype(BF16)
    triu_all = jnp.where(r <= c, 1.0, 0.0).astype(BF16)

    cum_c = sum(jnp.dot(tril_blk, p, preferred_element_type=F32) for p in _split3(val_c))
    pieces_r = _split3(val_r)
    cum_r_blk = sum(jnp.dot(p, triu_blk, preferred_element_type=F32) for p in pieces_r)
    cum_r_all = sum(jnp.dot(p, triu_all, preferred_element_type=F32) for p in pieces_r)
    cum_r_all = cum_r_all + carry_ref[:, 0:1]

    is_decay_c = (lane >= hv) & (lane < 2 * hv)
    cols_ref[0] = jnp.where(is_decay_c, cum_c, val_c)
    is_decay_r = (sub >= hv) & (sub < 2 * hv)
    is_forget_r = (sub >= 2 * hv) & (sub < 2 * hv + hf)
    rows = jnp.where(is_decay_r, cum_r_blk, jnp.where(is_forget_r, cum_r_all, val_r))
    rows_ref[0] = rows
    carry_ref[...] = jnp.broadcast_to(cum_r_all[:, tm - 1:tm], carry_ref.shape)


def _gates(h2, wg, wgt, pcol, prow, batch, seq, hv, hf):
    m, d = h2.shape
    tm = _tile(seq, 512)
    per_b = seq // tm
    kern = functools.partial(_gates_kernel, hv=hv, hf=hf, chunk=DN_CHUNK)
    return pl.pallas_call(
        kern,
        grid=(batch, per_b),
        in_specs=[pl.BlockSpec((tm, d), lambda b, t: (b * per_b + t, 0)),
                  pl.BlockSpec((d, GATE_LANES), lambda b, t: (0, 0)),
                  pl.BlockSpec((GATE_LANES, d), lambda b, t: (0, 0)),
                  pl.BlockSpec((2, GATE_LANES), lambda b, t: (0, 0)),
                  pl.BlockSpec((GATE_LANES, 2), lambda b, t: (0, 0))],
        out_specs=[pl.BlockSpec((1, tm, GATE_LANES), lambda b, t: (b, t, 0)),
                   pl.BlockSpec((1, GATE_LANES, tm), lambda b, t: (b, 0, t))],
        out_shape=[jax.ShapeDtypeStruct((batch, seq, GATE_LANES), F32),
                   jax.ShapeDtypeStruct((batch, GATE_LANES, seq), F32)],
        scratch_shapes=[pltpu.VMEM((GATE_LANES, GATE_LANES), F32)],
        compiler_params=_params("parallel", "arbitrary"),
        name="gates",
    )(h2, wg, wgt, pcol, prow)


def _causal_conv(u, tail, w):
    out = u * w[CONV_K - 1:CONV_K, :]
    row8 = lax.broadcasted_iota(jnp.int32, tail.shape, 0)
    for s in range(1, CONV_K):
        rolled = pltpu.roll(u, s, 0)
        head = jnp.where(row8 < s, pltpu.roll(tail, s, 0), rolled[0:8])
        shifted = jnp.concatenate([head, rolled[8:]], axis=0)
        out = out + shifted * w[CONV_K - 1 - s:CONV_K - s, :]
    return out


INV_LEVELS = (DN_CHUNK // 8).bit_length() - 1
MASK_DIAG8, MASK_EYE, MASK_STRICT = 0, INV_LEVELS + 1, INV_LEVELS + 2
N_MASKS = INV_LEVELS + 3


def _inverse_masks(n):
    r = lax.broadcasted_iota(jnp.int32, (n, n), 0)
    c = lax.broadcasted_iota(jnp.int32, (n, n), 1)
    masks = [(r >> 3) == (c >> 3)]
    for sh in range(3, 3 + INV_LEVELS):
        masks.append(((r >> (sh + 1)) == (c >> (sh + 1))) & ((r >> sh) == (c >> sh) + 1))
    masks += [r == c, r > c]
    return [jnp.where(m, 1.0, 0.0) for m in masks]


def _unit_lower_inverses(lms, mask_ref):
    n0s = [-(lm * mask_ref[MASK_DIAG8]) for lm in lms]
    n2s = [_dot(n0, n0) for n0 in n0s]
    n4s = [_dot(n2, n2) for n2 in n2s]
    ps = [mask_ref[MASK_EYE] + n0 for n0 in n0s]
    ps = [p + _dot(p, n2) for p, n2 in zip(ps, n2s)]
    ps = [p + _dot(p, n4) for p, n4 in zip(ps, n4s)]
    for level in range(1, INV_LEVELS + 1):
        xs = [_dot(lm * mask_ref[level], p) for lm, p in zip(lms, ps)]
        ps = [p - _dot(p, x) for p, x in zip(ps, xs)]
    return ps


def _deltanet_kernel(q_ref, k_ref, v_ref, z_ref, wq_ref, wk_ref, wv_ref, cols_ref, rows_ref, nw_ref,
                     o_ref, s_ref, qt_ref, kt_ref, vt_ref, mask_ref, u_ref, lhs1_ref, lhs2_ref, egl_ref,
                     *, hv, rep, unroll):
    i = pl.program_id(1)
    tb = pl.program_id(2)
    rows_blk = q_ref.shape[0]
    n_chunks = rows_blk // DN_CHUNK
    c_ = DN_CHUNK

    @pl.when(tb == 0)
    def _():
        s_ref[...] = jnp.zeros_like(s_ref)
        qt_ref[...] = jnp.zeros_like(qt_ref)
        kt_ref[...] = jnp.zeros_like(kt_ref)
        vt_ref[...] = jnp.zeros_like(vt_ref)
        for j, m in enumerate(_inverse_masks(c_)):
            mask_ref[j] = m

    r = lax.broadcasted_iota(jnp.int32, (c_, c_), 0)
    c = lax.broadcasted_iota(jnp.int32, (c_, c_), 1)
    lane = lax.broadcasted_iota(jnp.int32, (c_, GATE_LANES), 1)
    nw = nw_ref[...]

    def tail_of(ref, tail_ref, ci, off):
        prev = pl.multiple_of(jnp.maximum(off - 16, 0), 16)
        inside = ref[pl.ds(prev, 16), :].astype(F32)[8:16]
        return jnp.where(ci == 0, tail_ref[...], inside)

    def prepare_body(j, carry):
        chains = []
        for uu in range(unroll):
            ci = j * unroll + uu
            off = pl.multiple_of(ci * c_, c_)
            sl = pl.ds(off, c_)
            q_raw = q_ref[sl, :].astype(F32)
            k_raw = k_ref[sl, :].astype(F32)
            v_raw = v_ref[sl, :].astype(F32)
            qc = _causal_conv(q_raw, tail_of(q_ref, qt_ref, ci, off), wq_ref[...])
            kc = _causal_conv(k_raw, tail_of(k_ref, kt_ref, ci, off), wk_ref[...])
            vc = _causal_conv(v_raw, tail_of(v_ref, vt_ref, ci, off), wv_ref[...])
            qc = qc * _sigmoid(qc)
            kc = kc * _sigmoid(kc)
            vc = vc * _sigmoid(vc)
            qn = qc * lax.rsqrt(jnp.sum(qc * qc, axis=-1, keepdims=True) + EPS) * (HEAD_DIM ** -0.5)
            kn = kc * lax.rsqrt(jnp.sum(kc * kc, axis=-1, keepdims=True) + EPS)
            kk = _dot_nt(kn, kn)
            qk = _dot_nt(qn, kn)
            cols = cols_ref[0, sl, :]
            row0 = pl.multiple_of(((hv + i * rep) // 8) * 8, 8)
            rows8 = rows_ref[0, pl.ds(row0, 8), sl]
            sub8 = lax.broadcasted_iota(jnp.int32, (8, c_), 0)
            for hh in range(rep):
                head = i * rep + hh
                beta_c = jnp.sum(jnp.where(lane == head, cols, 0.0), axis=1, keepdims=True)
                gc_c = jnp.sum(jnp.where(lane == hv + head, cols, 0.0), axis=1, keepdims=True)
                gc_r = jnp.sum(jnp.where(sub8 == (hv + head) % 8, rows8, 0.0), axis=0, keepdims=True)
                g_last = gc_r[:, c_ - 1:c_]
                decay = jnp.exp(jnp.where(r >= c, gc_c - gc_r, -1e30))
                lm = (beta_c * kk) * (decay * mask_ref[MASK_STRICT])
                eg = jnp.exp(gc_c)
                v_h = vc[:, hh * HEAD_DIM:(hh + 1) * HEAD_DIM]
                rhs = jnp.concatenate([v_h * beta_c, kn * (beta_c * eg)], axis=1).astype(BF16)
                kd = kn * jnp.exp(g_last - gc_c)
                lhs2_ref[hh, ci] = jnp.concatenate([qk * decay, kd.T], axis=0).astype(BF16)
                egl_ref[hh, ci] = jnp.broadcast_to(jnp.exp(g_last), (8, HEAD_DIM))
                chains.append((hh, ci, sl, lm, rhs, (qn * eg).astype(BF16)))
        tinvs = _unit_lower_inverses([ch[3] for ch in chains], mask_ref)
        uws = [_dot(tinv, ch[4]) for tinv, ch in zip(tinvs, chains)]
        for uw, (hh, ci, sl, _, _, qg) in zip(uws, chains):
            u_ref[hh, sl, :] = uw[:, :HEAD_DIM]
            lhs1_ref[hh, ci] = jnp.concatenate([uw[:, HEAD_DIM:].astype(BF16), qg], axis=0)
        return carry

    lax.fori_loop(0, n_chunks // unroll, prepare_body, 0)

    def scan_body(ci, carry):
        off = pl.multiple_of(ci * c_, c_)
        sl = pl.ds(off, c_)
        heads = range(rep)
        states = [s_ref[hh] for hh in heads]
        ws_qs = [_dot(lhs1_ref[hh, ci], states[hh]) for hh in heads]
        v_new = [u_ref[hh, sl, :] - ws_qs[hh][:c_] for hh in heads]
        av_kv = [_dot(lhs2_ref[hh, ci], v_new[hh]) for hh in heads]
        for hh in heads:
            s_ref[hh] = states[hh] * egl_ref[hh, ci, 0:1, :] + av_kv[hh][c_:]
            o = ws_qs[hh][c_:] + av_kv[hh][:c_]
            z = z_ref[sl, hh * HEAD_DIM:(hh + 1) * HEAD_DIM].astype(F32)
            out = _rms(o) * nw * (z * _sigmoid(z))
            o_ref[sl, hh * HEAD_DIM:(hh + 1) * HEAD_DIM] = out.astype(o_ref.dtype)
        return carry

    lax.fori_loop(0, n_chunks, scan_body, 0)

    qt_ref[...] = q_ref[rows_blk - 16:, :].astype(F32)[8:16]
    kt_ref[...] = k_ref[rows_blk - 16:, :].astype(F32)[8:16]
    vt_ref[...] = v_ref[rows_blk - 16:, :].astype(F32)[8:16]


def _deltanet(proj, conv_wt, cols, rows, norm_w, batch, seq, hqk, hv):
    rep = hv // hqk
    assert rep == 2 and DN_CHUNK == HEAD_DIM
    tb = _tile(seq, 1024)
    per_b = seq // tb
    w2 = rep * HEAD_DIM
    row = lambda b, i, t: b * per_b + t
    n_chunks = tb // DN_CHUNK
    unroll = next(u for u in (4, 2, 1) if n_chunks % u == 0)
    kern = functools.partial(_deltanet_kernel, hv=hv, rep=rep, unroll=unroll)
    return pl.pallas_call(
        kern,
        grid=(batch, hqk, per_b),
        in_specs=[pl.BlockSpec((tb, HEAD_DIM), lambda b, i, t: (row(b, i, t), i)),
                  pl.BlockSpec((tb, HEAD_DIM), lambda b, i, t: (row(b, i, t), hqk + i)),
                  pl.BlockSpec((tb, w2), lambda b, i, t: (row(b, i, t), hqk + i)),
                  pl.BlockSpec((tb, w2), lambda b, i, t: (row(b, i, t), hqk + hv // rep + i)),
                  pl.BlockSpec((CONV_K, HEAD_DIM), lambda b, i, t: (0, i)),
                  pl.BlockSpec((CONV_K, HEAD_DIM), lambda b, i, t: (0, hqk + i)),
                  pl.BlockSpec((CONV_K, w2), lambda b, i, t: (0, hqk + i)),
                  pl.BlockSpec((1, tb, GATE_LANES), lambda b, i, t: (b, t, 0)),
                  pl.BlockSpec((1, GATE_LANES, tb), lambda b, i, t: (b, 0, t)),
                  pl.BlockSpec((1, HEAD_DIM), lambda b, i, t: (0, 0))],
        out_specs=pl.BlockSpec((tb, w2), lambda b, i, t: (row(b, i, t), i)),
        out_shape=jax.ShapeDtypeStruct((batch * seq, hv * HEAD_DIM), BF16),
        scratch_shapes=[pltpu.VMEM((rep, HEAD_DIM, HEAD_DIM), F32),
                        pltpu.VMEM((8, HEAD_DIM), F32),
                        pltpu.VMEM((8, HEAD_DIM), F32),
                        pltpu.VMEM((8, w2), F32),
                        pltpu.VMEM((N_MASKS, DN_CHUNK, DN_CHUNK), F32),
                        pltpu.VMEM((rep, tb, HEAD_DIM), F32),
                        pltpu.VMEM((rep, n_chunks, 2 * DN_CHUNK, HEAD_DIM), BF16),
                        pltpu.VMEM((rep, n_chunks, 2 * DN_CHUNK, DN_CHUNK), BF16),
                        pltpu.VMEM((rep, n_chunks, 8, HEAD_DIM), F32)],
        compiler_params=_params("parallel", "parallel", "arbitrary"),
        name="deltanet",
    )(proj, proj, proj, proj, conv_wt, conv_wt, conv_wt, cols, rows, norm_w.reshape(1, HEAD_DIM))


LOG2E = 1.4426950408889634


def _fox_kernel(q_ref, k_ref, v_ref, f_ref, o_ref, m_ref, l_ref, acc_ref, *, scale, sub):
    qi = pl.program_id(2)
    tq = q_ref.shape[0]
    tk = tq

    m_ref[...] = jnp.full_like(m_ref, -1e30)
    l_ref[...] = jnp.zeros_like(l_ref)
    acc_ref[...] = jnp.zeros_like(acc_ref)
    n_sub = tq // sub if tq % sub == 0 else 1
    ts = tq // n_sub

    def step(kj, masked):
        off = pl.multiple_of(kj * tk, tk)
        k = k_ref[pl.ds(off, tk), :]
        v = v_ref[pl.ds(off, tk), :]
        f2 = f_ref[0, 0, :, pl.ds(off, tk)] * LOG2E

        def scores(j):
            return _dot_nt(q_ref[j * ts:(j + 1) * ts, :], k)

        qk = scores(0)
        for j in range(n_sub):
            qk_next = scores(j + 1) if j + 1 < n_sub else None
            rows = slice(j * ts, (j + 1) * ts)
            s = qk * (scale * LOG2E) - f2
            if masked:
                r = lax.broadcasted_iota(jnp.int32, (ts, tk), 0) + j * ts
                c = lax.broadcasted_iota(jnp.int32, (ts, tk), 1)
                s = jnp.where(r >= c, s, -1e30)
            m_prev = m_ref[rows, :]
            m_new = jnp.maximum(m_prev, jnp.max(s, axis=-1, keepdims=True))
            p = jnp.exp2(s - m_new[:, 0:1])
            alpha = jnp.exp2(m_prev - m_new)
            l_ref[rows, :] = alpha * l_ref[rows, :] + jnp.sum(p, axis=-1, keepdims=True)
            acc_ref[rows, :] = alpha[:, 0:1] * acc_ref[rows, :] + _dot(p, v)
            m_ref[rows, :] = m_new
            qk = qk_next

    def body(kj, carry):
        step(kj, False)
        return carry

    lax.fori_loop(0, qi, body, 0)
    step(qi, True)
    o_ref[...] = (acc_ref[...] / l_ref[:, 0:1]).astype(o_ref.dtype)


def _fox(proj, f_rows, batch, seq, base, hf):
    tq = _tile(seq, 512)
    nq = seq // tq
    kern = functools.partial(_fox_kernel, scale=HEAD_DIM ** -0.5, sub=128)
    return pl.pallas_call(
        kern,
        grid=(batch, hf, nq),
        in_specs=[pl.BlockSpec((tq, HEAD_DIM), lambda b, h, qi: (b * nq + qi, base + h)),
                  pl.BlockSpec((seq, HEAD_DIM), lambda b, h, qi: (b, base + hf + h)),
                  pl.BlockSpec((seq, HEAD_DIM), lambda b, h, qi: (b, base + 2 * hf + h)),
                  pl.BlockSpec((1, 1, 1, seq), lambda b, h, qi: (b, h, 0, 0))],
        out_specs=pl.BlockSpec((tq, HEAD_DIM), lambda b, h, qi: (b * nq + qi, h)),
        out_shape=jax.ShapeDtypeStruct((batch * seq, hf * HEAD_DIM), BF16),
        scratch_shapes=[pltpu.VMEM((tq, HEAD_DIM), F32),
                        pltpu.VMEM((tq, HEAD_DIM), F32),
                        pltpu.VMEM((tq, HEAD_DIM), F32)],
        compiler_params=_params("parallel", "parallel", "arbitrary"),
        name="fox_attention",
    )(proj, proj, proj, f_rows)


def _merge_kernel(a1_ref, w1_ref, a2_ref, w2_ref, m1_ref, m2_ref, o_ref):
    y1 = _dot(a1_ref[...], w1_ref[...])
    y2 = _dot(a2_ref[...], w2_ref[...])
    g1 = _sigmoid(m1_ref[...].astype(F32))
    g2 = _sigmoid(m2_ref[...].astype(F32))
    o_ref[...] = (g1 * y1 + g2 * y2).astype(o_ref.dtype)


def _merge(o_dn, w_dn, o_fox, w_fox, proj, merge_base_cols):
    m, k1 = o_dn.shape
    k2 = o_fox.shape[1]
    d = w_dn.shape[1]
    tm, tn = _tile(m, 512), _tile(d, 512)
    assert merge_base_cols % tn == 0
    mb = merge_base_cols // tn
    return pl.pallas_call(
        _merge_kernel,
        grid=(m // tm, d // tn),
        in_specs=[pl.BlockSpec((tm, k1), lambda i, j: (i, 0)),
                  pl.BlockSpec((k1, tn), lambda i, j: (0, j)),
                  pl.BlockSpec((tm, k2), lambda i, j: (i, 0)),
                  pl.BlockSpec((k2, tn), lambda i, j: (0, j)),
                  pl.BlockSpec((tm, tn), lambda i, j: (i, mb + j)),
                  pl.BlockSpec((tm, tn), lambda i, j: (i, mb + d // tn + j))],
        out_specs=pl.BlockSpec((tm, tn), lambda i, j: (i, j)),
        out_shape=jax.ShapeDtypeStruct((m, d), BF16),
        compiler_params=_params("parallel", "parallel"),
        name="branch_merge",
    )(o_dn, w_dn, o_fox, w_fox, proj, proj)


def _out_res_kernel(a_ref, w_ref, x_ref, g_ref, gate_ref, o_ref, acc_ref):
    k = pl.program_id(1)

    @pl.when(k == 0)
    def _():
        acc_ref[...] = jnp.zeros_like(acc_ref)

    acc_ref[...] += _dot(a_ref[...], w_ref[...])

    @pl.when(k == pl.num_programs(1) - 1)
    def _():
        y = _rms(acc_ref[...]) * g_ref[...]
        o_ref[...] = x_ref[...] + gate_ref[0] * y


def _out_residual(a, w, x2, gain, mod3, gate_idx, seq):
    m, k = a.shape
    d = w.shape[1]
    tm = _tile(seq, 512)
    tk = _tile(k, 512)
    per_b = seq // tm
    return pl.pallas_call(
        _out_res_kernel,
        grid=(m // tm, k // tk),
        in_specs=[pl.BlockSpec((tm, tk), lambda i, kk: (i, kk)),
                  pl.BlockSpec((tk, d), lambda i, kk: (kk, 0)),
                  pl.BlockSpec((tm, d), lambda i, kk: (i, 0)),
                  pl.BlockSpec((1, d), lambda i, kk: (0, 0)),
                  pl.BlockSpec((1, 1, d), lambda i, kk: ((i // per_b) * 6 + gate_idx, 0, 0))],
        out_specs=pl.BlockSpec((tm, d), lambda i, kk: (i, 0)),
        out_shape=jax.ShapeDtypeStruct((m, d), F32),
        scratch_shapes=[pltpu.VMEM((tm, d), F32)],
        compiler_params=_params("parallel", "arbitrary"),
        name="proj_residual",
    )(a, w, x2, gain.reshape(1, d), mod3)


def _glu_kernel(a_ref, wg_ref, wu_ref, o_ref):
    a = a_ref[...]
    g = _dot(a, wg_ref[...])
    u = _dot(a, wu_ref[...])
    o_ref[...] = (g * _sigmoid(g) * u).astype(o_ref.dtype)


def _glu(a, wg, wu):
    m, k = a.shape
    n = wg.shape[1]
    tm, tn = _tile(m, 1024), _tile(n, 512)
    return pl.pallas_call(
        _glu_kernel,
        grid=(m // tm, n // tn),
        in_specs=[pl.BlockSpec((tm, k), lambda i, j: (i, 0)),
                  pl.BlockSpec((k, tn), lambda i, j: (0, j)),
                  pl.BlockSpec((k, tn), lambda i, j: (0, j))],
        out_specs=pl.BlockSpec((tm, tn), lambda i, j: (i, j)),
        out_shape=jax.ShapeDtypeStruct((m, n), BF16),
        compiler_params=_params("parallel", "parallel"),
        name="swiglu_up",
    )(a, wg, wu)


def kernel(x, c, w_ada, b_ada, norm_gains, w_in, dn_conv, dn_a_log, dn_dt_bias, dn_norm_w, fox_f_bias,
           w_branch_dn, w_branch_fox, w_out, w_gate, w_up, w_down):
    batch, seq, d = x.shape
    depth = w_ada.shape[0]
    hv = dn_a_log.shape[1]
    hf = fox_f_bias.shape[1]
    v_dim = hv * HEAD_DIM
    conv_dim = dn_conv.shape[1]
    qk_dim = (conv_dim - v_dim) // 2
    hqk = qk_dim // HEAD_DIM
    fox_dim = hf * HEAD_DIM
    assert 2 * hv + hf <= GATE_LANES

    o_z = conv_dim
    o_b = o_z + v_dim
    o_a = o_b + hv
    o_fq = o_a + hv
    o_ff = o_fq + 3 * fox_dim
    o_mg = o_ff + hf
    fox_base = (conv_dim + v_dim) // HEAD_DIM
    merge_base = conv_dim + v_dim + 3 * fox_dim

    mod = _modulation(c, w_ada, b_ada)
    x2 = x.reshape(batch * seq, d)
    pad = GATE_LANES - (2 * hv + hf)
    zpad = jnp.zeros((pad,), F32)

    for l in range(depth):
        mod3 = mod[l].reshape(batch * 6, 1, d)
        wl = w_in[l]
        w_main = jnp.concatenate([wl[:, :o_b], wl[:, o_fq:o_ff], wl[:, o_mg:]], axis=1).astype(BF16)
        w_g = jnp.concatenate([wl[:, o_b:o_fq], wl[:, o_ff:o_mg], jnp.zeros((d, pad), F32)], axis=1).astype(BF16)
        bias = jnp.concatenate([jnp.zeros((hv,), F32), dn_dt_bias[l], fox_f_bias[l], zpad])
        mult = jnp.concatenate([jnp.ones((hv,), F32), -jnp.exp(dn_a_log[l]), jnp.ones((hf,), F32), zpad])
        pcol = jnp.stack([bias, mult], axis=0)

        h = _norm_mod(x2, norm_gains[l, 0], mod3, 1, 0, seq)
        proj = _matmul(h, w_main)
        cols, rows = _gates(h, w_g, w_g.T, pcol, pcol.T, batch, seq, hv, hf)
        o_dn = _deltanet(proj, dn_conv[l].T, cols, rows, dn_norm_w[l], batch, seq, hqk, hv)
        f_rows = rows[:, 2 * hv:2 * hv + hf, :].reshape(batch, hf, 1, seq)
        o_fox = _fox(proj, f_rows, batch, seq, fox_base, hf)
        ymix = _merge(o_dn, w_branch_dn[l].astype(BF16), o_fox, w_branch_fox[l].astype(BF16), proj, merge_base)
        x2 = _out_residual(ymix, w_out[l].astype(BF16), x2, norm_gains[l, 1], mod3, 2, seq)

        h = _norm_mod(x2, norm_gains[l, 2], mod3, 4, 3, seq)
        gu = _glu(h, w_gate[l].astype(BF16), w_up[l].astype(BF16))
        x2 = _out_residual(gu, w_down[l].astype(BF16), x2, norm_gains[l, 3], mod3, 5, seq)

    return x2.reshape(batch, seq, d)
```

```python
import functools

import jax
import jax.numpy as jnp
from jax import lax
from jax.experimental import pallas as pl
from jax.experimental.pallas import tpu as pltpu

EPS = 1e-6
HEAD_DIM = 128
CONV_K = 4
DN_CHUNK = 128
GATE_LANES = 128
VMEM_LIMIT_BYTES = 48 * 1024 * 1024

F32 = jnp.float32
BF16 = jnp.bfloat16


def _params(*sem):
    return pltpu.CompilerParams(dimension_semantics=sem, vmem_limit_bytes=VMEM_LIMIT_BYTES)


def _tile(dim, pref):
    if dim <= pref:
        return dim
    t = pref - pref % HEAD_DIM
    while dim % t:
        t -= HEAD_DIM
    assert t > 0, (dim, pref)
    return t


def _dot(a, b):
    return jnp.dot(a.astype(BF16), b.astype(BF16), preferred_element_type=F32)


def _dot_nt(a, b):
    return lax.dot_general(a.astype(BF16), b.astype(BF16), (((1,), (1,)), ((), ())),
                           preferred_element_type=F32)


def _dot_tn(a, b):
    return lax.dot_general(a.astype(BF16), b.astype(BF16), (((0,), (0,)), ((), ())),
                           preferred_element_type=F32)


def _sigmoid(x):
    return 1.0 / (1.0 + jnp.exp(-x))


def _rms(x):
    return x * lax.rsqrt(jnp.mean(x * x, axis=-1, keepdims=True) + EPS)


def _mod_kernel(c_ref, w_ref, b_ref, o_ref):
    c = c_ref[...]
    cond = c * _sigmoid(c)
    o_ref[0] = _dot(cond, w_ref[0]) + b_ref[0]


def _modulation(c, w_ada, b_ada):
    depth, d, n = w_ada.shape
    b = c.shape[0]
    rows = 16
    c_pad = jnp.zeros((rows, d), F32).at[:b].set(c)
    tn = _tile(n, 1024)
    out = pl.pallas_call(
        _mod_kernel,
        grid=(depth, n // tn),
        in_specs=[pl.BlockSpec((rows, d), lambda l, j: (0, 0)),
                  pl.BlockSpec((1, d, tn), lambda l, j: (l, 0, j)),
                  pl.BlockSpec((1, 1, tn), lambda l, j: (l, 0, j))],
        out_specs=pl.BlockSpec((1, rows, tn), lambda l, j: (l, 0, j)),
        out_shape=jax.ShapeDtypeStruct((depth, rows, n), F32),
        compiler_params=_params("parallel", "parallel"),
        name="adaln_mod",
    )(c_pad, w_ada, b_ada.reshape(depth, 1, n))
    return out[:, :b].reshape(depth, b, 6, d)


def _norm_mod_kernel(x_ref, g_ref, sc_ref, sh_ref, o_ref):
    y = _rms(x_ref[...]) * g_ref[...]
    o_ref[...] = (y * (1.0 + sc_ref[0]) + sh_ref[0]).astype(o_ref.dtype)


def _norm_mod(x2, gain, mod3, sc_idx, sh_idx, seq):
    m, d = x2.shape
    tm = _tile(seq, 512)
    per_b = seq // tm
    return pl.pallas_call(
        _norm_mod_kernel,
        grid=(m // tm,),
        in_specs=[pl.BlockSpec((tm, d), lambda i: (i, 0)),
                  pl.BlockSpec((1, d), lambda i: (0, 0)),
                  pl.BlockSpec((1, 1, d), lambda i: ((i // per_b) * 6 + sc_idx, 0, 0)),
                  pl.BlockSpec((1, 1, d), lambda i: ((i // per_b) * 6 + sh_idx, 0, 0))],
        out_specs=pl.BlockSpec((tm, d), lambda i: (i, 0)),
        out_shape=jax.ShapeDtypeStruct((m, d), BF16),
        compiler_params=_params("parallel"),
        name="norm_mod",
    )(x2, gain.reshape(1, d), mod3, mod3)


def _mm_kernel(a_ref, b_ref, o_ref):
    o_ref[...] = _dot(a_ref[...], b_ref[...]).astype(o_ref.dtype)


def _matmul(a, b_all, layer):
    m, k = a.shape
    n = b_all.shape[2]
    tm, tn = _tile(m, 1024), _tile(n, 1024)
    return pl.pallas_call(
        _mm_kernel,
        grid=(m // tm, n // tn),
        in_specs=[pl.BlockSpec((tm, k), lambda i, j: (i, 0)),
                  pl.BlockSpec((None, k, tn), lambda i, j: (layer, 0, j))],
        out_specs=pl.BlockSpec((tm, tn), lambda i, j: (i, j)),
        out_shape=jax.ShapeDtypeStruct((m, n), BF16),
        compiler_params=_params("parallel", "parallel"),
        name="in_proj",
    )(a, b_all)


def _split3(x):
    hi = x.astype(BF16)
    r1 = x - hi.astype(F32)
    mid = r1.astype(BF16)
    lo = (r1 - mid.astype(F32)).astype(BF16)
    return hi, mid, lo


def _gate_act(x, bias, mult, idx, hv, hf):
    xb = x + bias
    e = jnp.exp(-jnp.abs(xb))
    l1p = jnp.log(1.0 + e)
    sig = jnp.where(xb >= 0, 1.0, e) / (1.0 + e)
    softplus = jnp.maximum(xb, 0.0) + l1p
    logsig = jnp.minimum(xb, 0.0) - l1p
    return jnp.where(idx < hv, sig,
                     jnp.where(idx < 2 * hv, mult * softplus,
                               jnp.where(idx < 2 * hv + hf, logsig, 0.0)))


def _gates_kernel(h_ref, wg_ref, wgt_ref, pcol_ref, prow_ref, cols_ref, rows_ref, carry_ref,
                  *, hv, hf, chunk):
    t = pl.program_id(1)
    tm = h_ref.shape[0]

    @pl.when(t == 0)
    def _():
        carry_ref[...] = jnp.zeros_like(carry_ref)

    h = h_ref[...]
    g_cols = _dot(h, wg_ref[...])
    g_rows = _dot_nt(wgt_ref[...], h)

    lane = lax.broadcasted_iota(jnp.int32, (tm, GATE_LANES), 1)
    val_c = _gate_act(g_cols, pcol_ref[0:1, :], pcol_ref[1:2, :], lane, hv, hf)
    sub = lax.broadcasted_iota(jnp.int32, (GATE_LANES, tm), 0)
    val_r = _gate_act(g_rows, prow_ref[:, 0:1], prow_ref[:, 1:2], sub, hv, hf)

    r = lax.broadcasted_iota(jnp.int32, (tm, tm), 0)
    c = lax.broadcasted_iota(jnp.int32, (tm, tm), 1)
    sh = chunk.bit_length() - 1
    same = (r >> sh) == (c >> sh)
    tril_blk = jnp.where((r >= c) & same, 1.0, 0.0).astype(BF16)
    triu_blk = jnp.where((r <= c) & same, 1.0, 0.0).astype(BF16)
    triu_all = jnp.where(r <= c, 1.0, 0.0).astype(BF16)

    cum_c = sum(jnp.dot(tril_blk, p, preferred_element_type=F32) for p in _split3(val_c))
    pieces_r = _split3(val_r)
    cum_r_blk = sum(jnp.dot(p, triu_blk, preferred_element_type=F32) for p in pieces_r)
    cum_r_all = sum(jnp.dot(p, triu_all, preferred_element_type=F32) for p in pieces_r)
    cum_r_all = cum_r_all + carry_ref[:, 0:1]

    is_decay_c = (lane >= hv) & (lane < 2 * hv)
    cols_ref[0] = jnp.where(is_decay_c, cum_c, val_c)
    is_decay_r = (sub >= hv) & (sub < 2 * hv)
    is_forget_r = (sub >= 2 * hv) & (sub < 2 * hv + hf)
    rows = jnp.where(is_decay_r, cum_r_blk, jnp.where(is_forget_r, cum_r_all, val_r))
    rows_ref[0] = rows
    carry_ref[...] = jnp.broadcast_to(cum_r_all[:, tm - 1:tm], carry_ref.shape)


def _gates(h2, wg, wgt, pcol, prow, batch, seq, hv, hf):
    m, d = h2.shape
    tm = _tile(seq, 512)
    per_b = seq // tm
    kern = functools.partial(_gates_kernel, hv=hv, hf=hf, chunk=DN_CHUNK)
    return pl.pallas_call(
        kern,
        grid=(batch, per_b),
        in_specs=[pl.BlockSpec((tm, d), lambda b, t: (b * per_b + t, 0)),
                  pl.BlockSpec((d, GATE_LANES), lambda b, t: (0, 0)),
                  pl.BlockSpec((GATE_LANES, d), lambda b, t: (0, 0)),
                  pl.BlockSpec((2, GATE_LANES), lambda b, t: (0, 0)),
                  pl.BlockSpec((GATE_LANES, 2), lambda b, t: (0, 0))],
        out_specs=[pl.BlockSpec((1, tm, GATE_LANES), lambda b, t: (b, t, 0)),
                   pl.BlockSpec((1, GATE_LANES, tm), lambda b, t: (b, 0, t))],
        out_shape=[jax.ShapeDtypeStruct((batch, seq, GATE_LANES), F32),
                   jax.ShapeDtypeStruct((batch, GATE_LANES, seq), F32)],
        scratch_shapes=[pltpu.VMEM((GATE_LANES, GATE_LANES), F32)],
        compiler_params=_params("parallel", "arbitrary"),
        name="gates",
    )(h2, wg, wgt, pcol, prow)


def _causal_conv(u, tail, w):
    out = u * w[CONV_K - 1:CONV_K, :]
    row8 = lax.broadcasted_iota(jnp.int32, tail.shape, 0)
    for s in range(1, CONV_K):
        rolled = pltpu.roll(u, s, 0)
        head = jnp.where(row8 < s, pltpu.roll(tail, s, 0), rolled[0:8])
        shifted = jnp.concatenate([head, rolled[8:]], axis=0)
        out = out + shifted * w[CONV_K - 1 - s:CONV_K - s, :]
    return out


INV_LEVELS = (DN_CHUNK // 8).bit_length() - 1
MASK_DIAG8, MASK_EYE, MASK_STRICT = 0, INV_LEVELS + 1, INV_LEVELS + 2
N_MASKS = INV_LEVELS + 3


def _inverse_masks(n):
    r = lax.broadcasted_iota(jnp.int32, (n, n), 0)
    c = lax.broadcasted_iota(jnp.int32, (n, n), 1)
    masks = [(r >> 3) == (c >> 3)]
    for sh in range(3, 3 + INV_LEVELS):
        masks.append(((r >> (sh + 1)) == (c >> (sh + 1))) & ((r >> sh) == (c >> sh) + 1))
    masks += [r == c, r > c]
    return [jnp.where(m, 1.0, 0.0) for m in masks]


def _unit_lower_inverses(lms, mask_ref):
    n0s = [-(lm * mask_ref[MASK_DIAG8]) for lm in lms]
    n2s = [_dot(n0, n0) for n0 in n0s]
    n4s = [_dot(n2, n2) for n2 in n2s]
    ps = [mask_ref[MASK_EYE] + n0 for n0 in n0s]
    ps = [p + _dot(p, n2) for p, n2 in zip(ps, n2s)]
    ps = [p + _dot(p, n4) for p, n4 in zip(ps, n4s)]
    for level in range(1, INV_LEVELS + 1):
        xs = [_dot(lm * mask_ref[level], p) for lm, p in zip(lms, ps)]
        ps = [p - _dot(p, x) for p, x in zip(ps, xs)]
    return ps


def _deltanet_kernel(q_ref, k_ref, v_ref, z_ref, wq_ref, wk_ref, wv_ref, cols_ref, rows_ref, nw_ref,
                     o_ref, s_ref, qt_ref, kt_ref, vt_ref, mask_ref, u_ref, lhs1_ref, lhs2_ref, egl_ref,
                     *, hv, rep, unroll):
    i = pl.program_id(1)
    tb = pl.program_id(2)
    rows_blk = q_ref.shape[0]
    n_chunks = rows_blk // DN_CHUNK
    c_ = DN_CHUNK

    @pl.when(tb == 0)
    def _():
        s_ref[...] = jnp.zeros_like(s_ref)
        qt_ref[...] = jnp.zeros_like(qt_ref)
        kt_ref[...] = jnp.zeros_like(kt_ref)
        vt_ref[...] = jnp.zeros_like(vt_ref)
        for j, m in enumerate(_inverse_masks(c_)):
            mask_ref[j] = m

    r = lax.broadcasted_iota(jnp.int32, (c_, c_), 0)
    c = lax.broadcasted_iota(jnp.int32, (c_, c_), 1)
    lane = lax.broadcasted_iota(jnp.int32, (c_, GATE_LANES), 1)
    nw = nw_ref[...]

    def tail_of(ref, tail_ref, ci, off):
        prev = pl.multiple_of(jnp.maximum(off - 16, 0), 16)
        inside = ref[pl.ds(prev, 16), :].astype(F32)[8:16]
        return jnp.where(ci == 0, tail_ref[...], inside)

    def prepare_body(j, carry):
        chains = []
        for uu in range(unroll):
            ci = j * unroll + uu
            off = pl.multiple_of(ci * c_, c_)
            sl = pl.ds(off, c_)
            q_raw = q_ref[sl, :].astype(F32)
            k_raw = k_ref[sl, :].astype(F32)
            v_raw = v_ref[sl, :].astype(F32)
            qc = _causal_conv(q_raw, tail_of(q_ref, qt_ref, ci, off), wq_ref[...])
            kc = _causal_conv(k_raw, tail_of(k_ref, kt_ref, ci, off), wk_ref[...])
            vc = _causal_conv(v_raw, tail_of(v_ref, vt_ref, ci, off), wv_ref[...])
            qc = qc * _sigmoid(qc)
            kc = kc * _sigmoid(kc)
            vc = vc * _sigmoid(vc)
            qn = qc * lax.rsqrt(jnp.sum(qc * qc, axis=-1, keepdims=True) + EPS) * (HEAD_DIM ** -0.5)
            kn = kc * lax.rsqrt(jnp.sum(kc * kc, axis=-1, keepdims=True) + EPS)
            kk = _dot_nt(kn, kn)
            qk = _dot_nt(qn, kn)
            cols = cols_ref[0, sl, :]
            row0 = pl.multiple_of(((hv + i * rep) // 8) * 8, 8)
            rows8 = rows_ref[0, pl.ds(row0, 8), sl]
            sub8 = lax.broadcasted_iota(jnp.int32, (8, c_), 0)
            for hh in range(rep):
                head = i * rep + hh
                beta_c = jnp.sum(jnp.where(lane == head, cols, 0.0), axis=1, keepdims=True)
                gc_c = jnp.sum(jnp.where(lane == hv + head, cols, 0.0), axis=1, keepdims=True)
                gc_r = jnp.sum(jnp.where(sub8 == (hv + head) % 8, rows8, 0.0), axis=0, keepdims=True)
                g_last = gc_r[:, c_ - 1:c_]
                decay = jnp.exp(jnp.where(r >= c, gc_c - gc_r, -1e30))
                lm = (beta_c * kk) * (decay * mask_ref[MASK_STRICT])
                eg = jnp.exp(gc_c)
                v_h = vc[:, hh * HEAD_DIM:(hh + 1) * HEAD_DIM]
                rhs = jnp.concatenate([v_h * beta_c, kn * (beta_c * eg)], axis=1).astype(BF16)
                kd = kn * jnp.exp(g_last - gc_c)
                lhs2_ref[hh, ci] = jnp.concatenate([qk * decay, kd.T], axis=0).astype(BF16)
                egl_ref[hh, ci] = jnp.broadcast_to(jnp.exp(g_last), (8, HEAD_DIM))
                chains.append((hh, ci, sl, lm, rhs, (qn * eg).astype(BF16)))
        tinvs = _unit_lower_inverses([ch[3] for ch in chains], mask_ref)
        uws = [_dot(tinv, ch[4]) for tinv, ch in zip(tinvs, chains)]
        for uw, (hh, ci, sl, _, _, qg) in zip(uws, chains):
            u_ref[hh, sl, :] = uw[:, :HEAD_DIM]
            lhs1_ref[hh, ci] = jnp.concatenate([uw[:, HEAD_DIM:].astype(BF16), qg], axis=0)
        return carry

    lax.fori_loop(0, n_chunks // unroll, prepare_body, 0)

    def scan_body(ci, carry):
        off = pl.multiple_of(ci * c_, c_)
        sl = pl.ds(off, c_)
        heads = range(rep)
        states = [s_ref[hh] for hh in heads]
        ws_qs = [_dot(lhs1_ref[hh, ci], states[hh]) for hh in heads]
        v_new = [u_ref[hh, sl, :] - ws_qs[hh][:c_] for hh in heads]
        av_kv = [_dot(lhs2_ref[hh, ci], v_new[hh]) for hh in heads]
        for hh in heads:
            s_ref[hh] = states[hh] * egl_ref[hh, ci, 0:1, :] + av_kv[hh][c_:]
            o = ws_qs[hh][c_:] + av_kv[hh][:c_]
            z = z_ref[sl, hh * HEAD_DIM:(hh + 1) * HEAD_DIM].astype(F32)
            out = _rms(o) * nw * (z * _sigmoid(z))
            o_ref[sl, hh * HEAD_DIM:(hh + 1) * HEAD_DIM] = out.astype(o_ref.dtype)
        return carry

    lax.fori_loop(0, n_chunks, scan_body, 0)

    qt_ref[...] = q_ref[rows_blk - 16:, :].astype(F32)[8:16]
    kt_ref[...] = k_ref[rows_blk - 16:, :].astype(F32)[8:16]
    vt_ref[...] = v_ref[rows_blk - 16:, :].astype(F32)[8:16]


def _deltanet(proj, conv_wt, cols, rows, norm_w, batch, seq, hqk, hv):
    rep = hv // hqk
    assert rep == 2 and DN_CHUNK == HEAD_DIM
    tb = _tile(seq, 1024)
    per_b = seq // tb
    w2 = rep * HEAD_DIM
    row = lambda b, i, t: b * per_b + t
    n_chunks = tb // DN_CHUNK
    unroll = next(u for u in (4, 2, 1) if n_chunks % u == 0)
    kern = functools.partial(_deltanet_kernel, hv=hv, rep=rep, unroll=unroll)
    return pl.pallas_call(
        kern,
        grid=(batch, hqk, per_b),
        in_specs=[pl.BlockSpec((tb, HEAD_DIM), lambda b, i, t: (row(b, i, t), i)),
                  pl.BlockSpec((tb, HEAD_DIM), lambda b, i, t: (row(b, i, t), hqk + i)),
                  pl.BlockSpec((tb, w2), lambda b, i, t: (row(b, i, t), hqk + i)),
                  pl.BlockSpec((tb, w2), lambda b, i, t: (row(b, i, t), hqk + hv // rep + i)),
                  pl.BlockSpec((CONV_K, HEAD_DIM), lambda b, i, t: (0, i)),
                  pl.BlockSpec((CONV_K, HEAD_DIM), lambda b, i, t: (0, hqk + i)),
                  pl.BlockSpec((CONV_K, w2), lambda b, i, t: (0, hqk + i)),
                  pl.BlockSpec((1, tb, GATE_LANES), lambda b, i, t: (b, t, 0)),
                  pl.BlockSpec((1, GATE_LANES, tb), lambda b, i, t: (b, 0, t)),
                  pl.BlockSpec((1, HEAD_DIM), lambda b, i, t: (0, 0))],
        out_specs=pl.BlockSpec((tb, w2), lambda b, i, t: (row(b, i, t), i)),
        out_shape=jax.ShapeDtypeStruct((batch * seq, hv * HEAD_DIM), BF16),
        scratch_shapes=[pltpu.VMEM((rep, HEAD_DIM, HEAD_DIM), F32),
                        pltpu.VMEM((8, HEAD_DIM), F32),
                        pltpu.VMEM((8, HEAD_DIM), F32),
                        pltpu.VMEM((8, w2), F32),
                        pltpu.VMEM((N_MASKS, DN_CHUNK, DN_CHUNK), F32),
                        pltpu.VMEM((rep, tb, HEAD_DIM), F32),
                        pltpu.VMEM((rep, n_chunks, 2 * DN_CHUNK, HEAD_DIM), BF16),
                        pltpu.VMEM((rep, n_chunks, 2 * DN_CHUNK, DN_CHUNK), BF16),
                        pltpu.VMEM((rep, n_chunks, 8, HEAD_DIM), F32)],
        compiler_params=_params("parallel", "parallel", "arbitrary"),
        name="deltanet",
    )(proj, proj, proj, proj, conv_wt, conv_wt, conv_wt, cols, rows, norm_w.reshape(1, HEAD_DIM))


LOG2E = 1.4426950408889634


MXU_COLS = 256


def _reduce_rows(x, op):
    rows, lanes = x.shape
    slabs = 8
    if rows % (8 * slabs) == 0:
        x3 = x.reshape(slabs, rows // slabs, lanes)
        x = x3[0]
        for j in range(1, slabs):
            x = op(x, x3[j])
    final = jnp.max if op is jnp.maximum else jnp.sum
    return final(x, axis=0, keepdims=True)


def _fox_kernel(q_ref, k_ref, v_ref, f_ref, o_ref, vt_ref, fcol_ref, s_ref, m_ref, l_ref, acc_ref,
                *, scale):
    qi = pl.program_id(2)
    tq = q_ref.shape[0]
    tk = tq
    seq = k_ref.shape[0]
    qt = min(MXU_COLS, tq)
    d = HEAD_DIM

    @pl.when(qi == 0)
    def _():
        def prep(bi, carry):
            off = pl.multiple_of(bi * d, d)
            vt_ref[:, pl.ds(off, d)] = v_ref[pl.ds(off, d), :].astype(F32).T.astype(BF16)
            frow = f_ref[0, 0, :, pl.ds(off, d)] * LOG2E
            fcol_ref[pl.ds(off, d), :] = jnp.broadcast_to(frow, (d, d)).T
            return carry
        lax.fori_loop(0, seq // d, prep, 0)

    m_ref[...] = jnp.full_like(m_ref, -1e30)
    l_ref[...] = jnp.zeros_like(l_ref)
    acc_ref[...] = jnp.zeros_like(acc_ref)

    def scores(kj, slot):
        off = pl.multiple_of(kj * tk, tk)
        s_ref[slot] = _dot_nt(k_ref[pl.ds(off, tk), :], q_ref[...])

    def consume(kj, slot, masked):
        off = pl.multiple_of(kj * tk, tk)
        fcol = fcol_ref[pl.ds(off, tk), :]
        fcol = jnp.concatenate([fcol] * (qt // d), axis=1)
        vt = vt_ref[:, pl.ds(off, tk)]
        for t in range(tq // qt):
            lanes = slice(t * qt, (t + 1) * qt)
            s = s_ref[slot, :, lanes] * (scale * LOG2E) - fcol
            if masked:
                key = lax.broadcasted_iota(jnp.int32, (tk, qt), 0)
                qry = lax.broadcasted_iota(jnp.int32, (tk, qt), 1) + t * qt
                s = jnp.where(key <= qry, s, -1e30)
            m_prev = m_ref[:, lanes]
            m_new = jnp.maximum(m_prev, _reduce_rows(s, jnp.maximum))
            p = jnp.exp2(s - m_new)
            alpha = jnp.exp2(m_prev - m_new)
            l_ref[:, lanes] = alpha * l_ref[:, lanes] + _reduce_rows(p, jnp.add)
            m_ref[:, lanes] = m_new
            acc_ref[:, lanes] = alpha * acc_ref[:, lanes] + _dot(vt, p)

    scores(0, 0)

    def body(i, carry):
        kj = 2 * i
        scores(kj + 1, 1)
        consume(kj, 0, False)
        scores(kj + 2, 0)
        consume(kj + 1, 1, False)
        return carry

    lax.fori_loop(0, qi // 2, body, 0)

    @pl.when(qi % 2 == 0)
    def _():
        consume(qi, 0, True)

    @pl.when(qi % 2 == 1)
    def _():
        scores(qi, 1)
        consume(qi - 1, 0, False)
        consume(qi, 1, True)

    o_ref[...] = (acc_ref[...] / l_ref[...]).T.astype(o_ref.dtype)


def _fox(proj, f_rows, batch, seq, base, hf):
    tq = _tile(seq, 512)
    nq = seq // tq
    kern = functools.partial(_fox_kernel, scale=HEAD_DIM ** -0.5)
    return pl.pallas_call(
        kern,
        grid=(batch, hf, nq),
        in_specs=[pl.BlockSpec((tq, HEAD_DIM), lambda b, h, qi: (b * nq + qi, base + h)),
                  pl.BlockSpec((seq, HEAD_DIM), lambda b, h, qi: (b, base + hf + h)),
                  pl.BlockSpec((seq, HEAD_DIM), lambda b, h, qi: (b, base + 2 * hf + h)),
                  pl.BlockSpec((1, 1, 1, seq), lambda b, h, qi: (b, h, 0, 0))],
        out_specs=pl.BlockSpec((tq, HEAD_DIM), lambda b, h, qi: (b * nq + qi, h)),
        out_shape=jax.ShapeDtypeStruct((batch * seq, hf * HEAD_DIM), BF16),
        scratch_shapes=[pltpu.VMEM((HEAD_DIM, seq), BF16),
                        pltpu.VMEM((seq, HEAD_DIM), F32),
                        pltpu.VMEM((2, tq, tq), F32),
                        pltpu.VMEM((1, tq), F32),
                        pltpu.VMEM((1, tq), F32),
                        pltpu.VMEM((HEAD_DIM, tq), F32)],
        compiler_params=_params("parallel", "parallel", "arbitrary"),
        name="fox_attention",
    )(proj, proj, proj, f_rows)


def _merge_kernel(a1_ref, w1_ref, a2_ref, w2_ref, m1_ref, m2_ref, o_ref):
    y1 = _dot(a1_ref[...], w1_ref[...])
    y2 = _dot(a2_ref[...], w2_ref[...])
    g1 = _sigmoid(m1_ref[...].astype(F32))
    g2 = _sigmoid(m2_ref[...].astype(F32))
    o_ref[...] = (g1 * y1 + g2 * y2).astype(o_ref.dtype)


def _merge(o_dn, w_dn, o_fox, w_fox, layer, proj, merge_base_cols):
    m, k1 = o_dn.shape
    k2 = o_fox.shape[1]
    d = w_dn.shape[2]
    tm, tn = _tile(m, 512), _tile(d, 512)
    assert merge_base_cols % tn == 0
    mb = merge_base_cols // tn
    return pl.pallas_call(
        _merge_kernel,
        grid=(m // tm, d // tn),
        in_specs=[pl.BlockSpec((tm, k1), lambda i, j: (i, 0)),
                  pl.BlockSpec((None, k1, tn), lambda i, j: (layer, 0, j)),
                  pl.BlockSpec((tm, k2), lambda i, j: (i, 0)),
                  pl.BlockSpec((None, k2, tn), lambda i, j: (layer, 0, j)),
                  pl.BlockSpec((tm, tn), lambda i, j: (i, mb + j)),
                  pl.BlockSpec((tm, tn), lambda i, j: (i, mb + d // tn + j))],
        out_specs=pl.BlockSpec((tm, tn), lambda i, j: (i, j)),
        out_shape=jax.ShapeDtypeStruct((m, d), BF16),
        compiler_params=_params("parallel", "parallel"),
        name="branch_merge",
    )(o_dn, w_dn, o_fox, w_fox, proj, proj)


def _out_res_kernel(a_ref, w_ref, x_ref, g_ref, gate_ref, o_ref, acc_ref):
    k = pl.program_id(1)

    @pl.when(k == 0)
    def _():
        acc_ref[...] = jnp.zeros_like(acc_ref)

    acc_ref[...] += _dot(a_ref[...], w_ref[...])

    @pl.when(k == pl.num_programs(1) - 1)
    def _():
        y = _rms(acc_ref[...]) * g_ref[...]
        o_ref[...] = x_ref[...] + gate_ref[0] * y


def _out_residual(a, w, layer, x2, gain, mod3, gate_idx, seq):
    m, k = a.shape
    d = w.shape[2]
    tm = _tile(seq, 512)
    tk = _tile(k, 512)
    per_b = seq // tm
    return pl.pallas_call(
        _out_res_kernel,
        grid=(m // tm, k // tk),
        in_specs=[pl.BlockSpec((tm, tk), lambda i, kk: (i, kk)),
                  pl.BlockSpec((None, tk, d), lambda i, kk: (layer, kk, 0)),
                  pl.BlockSpec((tm, d), lambda i, kk: (i, 0)),
                  pl.BlockSpec((1, d), lambda i, kk: (0, 0)),
                  pl.BlockSpec((1, 1, d), lambda i, kk: ((i // per_b) * 6 + gate_idx, 0, 0))],
        out_specs=pl.BlockSpec((tm, d), lambda i, kk: (i, 0)),
        out_shape=jax.ShapeDtypeStruct((m, d), F32),
        scratch_shapes=[pltpu.VMEM((tm, d), F32)],
        compiler_params=_params("parallel", "arbitrary"),
        name="proj_residual",
    )(a, w, x2, gain.reshape(1, d), mod3)


def _glu_kernel(a_ref, wg_ref, wu_ref, o_ref):
    a = a_ref[...]
    g = _dot(a, wg_ref[...])
    u = _dot(a, wu_ref[...])
    o_ref[...] = (g * _sigmoid(g) * u).astype(o_ref.dtype)


def _glu(a, wg, wu, layer):
    m, k = a.shape
    n = wg.shape[2]
    tm, tn = _tile(m, 1024), _tile(n, 512)
    return pl.pallas_call(
        _glu_kernel,
        grid=(m // tm, n // tn),
        in_specs=[pl.BlockSpec((tm, k), lambda i, j: (i, 0)),
                  pl.BlockSpec((None, k, tn), lambda i, j: (layer, 0, j)),
                  pl.BlockSpec((None, k, tn), lambda i, j: (layer, 0, j))],
        out_specs=pl.BlockSpec((tm, tn), lambda i, j: (i, j)),
        out_shape=jax.ShapeDtypeStruct((m, n), BF16),
        compiler_params=_params("parallel", "parallel"),
        name="swiglu_up",
    )(a, wg, wu)


def kernel(x, c, w_ada, b_ada, norm_gains, w_in, dn_conv, dn_a_log, dn_dt_bias, dn_norm_w, fox_f_bias,
           w_branch_dn, w_branch_fox, w_out, w_gate, w_up, w_down):
    batch, seq, d = x.shape
    depth = w_ada.shape[0]
    hv = dn_a_log.shape[1]
    hf = fox_f_bias.shape[1]
    v_dim = hv * HEAD_DIM
    conv_dim = dn_conv.shape[1]
    qk_dim = (conv_dim - v_dim) // 2
    hqk = qk_dim // HEAD_DIM
    fox_dim = hf * HEAD_DIM
    assert 2 * hv + hf <= GATE_LANES

    o_z = conv_dim
    o_b = o_z + v_dim
    o_a = o_b + hv
    o_fq = o_a + hv
    o_ff = o_fq + 3 * fox_dim
    o_mg = o_ff + hf
    fox_base = (conv_dim + v_dim) // HEAD_DIM
    merge_base = conv_dim + v_dim + 3 * fox_dim

    mod = _modulation(c, w_ada, b_ada)
    x2 = x.reshape(batch * seq, d)
    pad = GATE_LANES - (2 * hv + hf)
    zpad = jnp.zeros((pad,), F32)

    w_main_all = jnp.concatenate([w_in[:, :, :o_b], w_in[:, :, o_fq:o_ff], w_in[:, :, o_mg:]], axis=2).astype(BF16)
    w_g_all = jnp.concatenate([w_in[:, :, o_b:o_fq], w_in[:, :, o_ff:o_mg],
                               jnp.zeros((depth, d, pad), F32)], axis=2).astype(BF16)
    w_dn_all, w_fox_all, w_out_all = (w.astype(BF16) for w in (w_branch_dn, w_branch_fox, w_out))
    w_gate_all, w_up_all, w_down_all = (w.astype(BF16) for w in (w_gate, w_up, w_down))

    for l in range(depth):
        mod3 = mod[l].reshape(batch * 6, 1, d)
        w_g = w_g_all[l]
        bias = jnp.concatenate([jnp.zeros((hv,), F32), dn_dt_bias[l], fox_f_bias[l], zpad])
        mult = jnp.concatenate([jnp.ones((hv,), F32), -jnp.exp(dn_a_log[l]), jnp.ones((hf,), F32), zpad])
        pcol = jnp.stack([bias, mult], axis=0)

        h = _norm_mod(x2, norm_gains[l, 0], mod3, 1, 0, seq)
        proj = _matmul(h, w_main_all, l)
        cols, rows = _gates(h, w_g, w_g.T, pcol, pcol.T, batch, seq, hv, hf)
        o_dn = _deltanet(proj, dn_conv[l].T, cols, rows, dn_norm_w[l], batch, seq, hqk, hv)
        f_rows = rows[:, 2 * hv:2 * hv + hf, :].reshape(batch, hf, 1, seq)
        o_fox = _fox(proj, f_rows, batch, seq, fox_base, hf)
        ymix = _merge(o_dn, w_dn_all, o_fox, w_fox_all, l, proj, merge_base)
        x2 = _out_residual(ymix, w_out_all, l, x2, norm_gains[l, 1], mod3, 2, seq)

        h = _norm_mod(x2, norm_gains[l, 2], mod3, 4, 3, seq)
        gu = _glu(h, w_gate_all, w_up_all, l)
        x2 = _out_residual(gu, w_down_all, l, x2, norm_gains[l, 3], mod3, 5, seq)

    return x2.reshape(batch, seq, d)
```

```python
import functools

import jax
import jax.numpy as jnp
from jax import lax
from jax.experimental import pallas as pl
from jax.experimental.pallas import tpu as pltpu

EPS = 1e-6
HEAD_DIM = 128
CONV_K = 4
DN_CHUNK = 128
GATE_LANES = 128
VMEM_LIMIT_BYTES = 48 * 1024 * 1024

F32 = jnp.float32
BF16 = jnp.bfloat16


def _params(*sem):
    return pltpu.CompilerParams(dimension_semantics=sem, vmem_limit_bytes=VMEM_LIMIT_BYTES)


def _tile(dim, pref):
    if dim <= pref:
        return dim
    t = pref - pref % HEAD_DIM
    while dim % t:
        t -= HEAD_DIM
    assert t > 0, (dim, pref)
    return t


def _dot(a, b):
    return jnp.dot(a.astype(BF16), b.astype(BF16), preferred_element_type=F32)


def _dot_nt(a, b):
    return lax.dot_general(a.astype(BF16), b.astype(BF16), (((1,), (1,)), ((), ())),
                           preferred_element_type=F32)


def _dot_tn(a, b):
    return lax.dot_general(a.astype(BF16), b.astype(BF16), (((0,), (0,)), ((), ())),
                           preferred_element_type=F32)


def _sigmoid(x):
    return 1.0 / (1.0 + jnp.exp(-x))


def _rms(x):
    return x * lax.rsqrt(jnp.mean(x * x, axis=-1, keepdims=True) + EPS)


def _mod_kernel(c_ref, w_ref, b_ref, o_ref):
    c = c_ref[...]
    cond = c * _sigmoid(c)
    o_ref[0] = _dot(cond, w_ref[0]) + b_ref[0]


def _modulation(c, w_ada, b_ada):
    depth, d, n = w_ada.shape
    b = c.shape[0]
    rows = 16
    c_pad = jnp.zeros((rows, d), F32).at[:b].set(c)
    tn = _tile(n, 1024)
    out = pl.pallas_call(
        _mod_kernel,
        grid=(depth, n // tn),
        in_specs=[pl.BlockSpec((rows, d), lambda l, j: (0, 0)),
                  pl.BlockSpec((1, d, tn), lambda l, j: (l, 0, j)),
                  pl.BlockSpec((1, 1, tn), lambda l, j: (l, 0, j))],
        out_specs=pl.BlockSpec((1, rows, tn), lambda l, j: (l, 0, j)),
        out_shape=jax.ShapeDtypeStruct((depth, rows, n), F32),
        compiler_params=_params("parallel", "parallel"),
        name="adaln_mod",
    )(c_pad, w_ada, b_ada.reshape(depth, 1, n))
    return out[:, :b].reshape(depth, b, 6, d)


def _norm_mod_kernel(x_ref, g_ref, sc_ref, sh_ref, o_ref):
    y = _rms(x_ref[...]) * g_ref[...]
    o_ref[...] = (y * (1.0 + sc_ref[0]) + sh_ref[0]).astype(o_ref.dtype)


def _norm_mod(x2, gain, mod3, sc_idx, sh_idx, seq):
    m, d = x2.shape
    tm = _tile(seq, 512)
    per_b = seq // tm
    return pl.pallas_call(
        _norm_mod_kernel,
        grid=(m // tm,),
        in_specs=[pl.BlockSpec((tm, d), lambda i: (i, 0)),
                  pl.BlockSpec((1, d), lambda i: (0, 0)),
                  pl.BlockSpec((1, 1, d), lambda i: ((i // per_b) * 6 + sc_idx, 0, 0)),
                  pl.BlockSpec((1, 1, d), lambda i: ((i // per_b) * 6 + sh_idx, 0, 0))],
        out_specs=pl.BlockSpec((tm, d), lambda i: (i, 0)),
        out_shape=jax.ShapeDtypeStruct((m, d), BF16),
        compiler_params=_params("parallel"),
        name="norm_mod",
    )(x2, gain.reshape(1, d), mod3, mod3)


W_ROWS = 256


def _in_proj_kernel(a_ref, wa_ref, wb_ref, o_ref, w_ref, *, segments):
    j = pl.program_id(0)
    k = wa_ref.shape[0]

    @pl.when(pl.program_id(1) == 0)
    def _():
        for lo, hi, shift in segments:
            @pl.when((j >= lo) & (j < hi))
            def _():
                for r0 in range(0, k, W_ROWS):
                    rows = slice(r0, min(r0 + W_ROWS, k))
                    if shift == 0:
                        w_ref[rows, :] = wa_ref[rows, :].astype(BF16)
                    else:
                        w = jnp.concatenate([wa_ref[rows, shift:], wb_ref[rows, :shift]], axis=1)
                        w_ref[rows, :] = w.astype(BF16)

    o_ref[...] = _dot(a_ref[...], w_ref[...]).astype(o_ref.dtype)


def _in_proj(a, w_in, layer, seg_cols):
    m, k = a.shape
    n_out = sum(w for _, w in seg_cols)
    tn = 1024
    while any(w % tn for _, w in seg_cols):
        tn //= 2
    assert tn >= HEAD_DIM
    tm = _tile(m, 1024)
    segments, out0 = [], 0
    for src0, width in seg_cols:
        shift = src0 - out0
        assert 0 <= shift < HEAD_DIM
        segments.append((out0 // tn, (out0 + width) // tn, shift))
        out0 += width
    kern = functools.partial(_in_proj_kernel, segments=tuple(segments))
    nb = tn // HEAD_DIM
    last_blk = (w_in.shape[2] - 1) // HEAD_DIM
    return pl.pallas_call(
        kern,
        grid=(n_out // tn, m // tm),
        in_specs=[pl.BlockSpec((tm, k), lambda j, i: (i, 0)),
                  pl.BlockSpec((None, k, tn), lambda j, i: (layer, 0, j)),
                  pl.BlockSpec((None, k, HEAD_DIM),
                               lambda j, i: (layer, 0, jnp.minimum((j + 1) * nb, last_blk)))],
        out_specs=pl.BlockSpec((tm, tn), lambda j, i: (i, j)),
        out_shape=jax.ShapeDtypeStruct((m, n_out), BF16),
        scratch_shapes=[pltpu.VMEM((k, tn), BF16)],
        compiler_params=_params("parallel", "arbitrary"),
        name="in_proj",
    )(a, w_in, w_in)


def _split3(x):
    hi = x.astype(BF16)
    r1 = x - hi.astype(F32)
    mid = r1.astype(BF16)
    lo = (r1 - mid.astype(F32)).astype(BF16)
    return hi, mid, lo


def _gate_act(x, bias, mult, idx, hv, hf):
    xb = x + bias
    e = jnp.exp(-jnp.abs(xb))
    l1p = jnp.log(1.0 + e)
    sig = jnp.where(xb >= 0, 1.0, e) / (1.0 + e)
    softplus = jnp.maximum(xb, 0.0) + l1p
    logsig = jnp.minimum(xb, 0.0) - l1p
    return jnp.where(idx < hv, sig,
                     jnp.where(idx < 2 * hv, mult * softplus,
                               jnp.where(idx < 2 * hv + hf, logsig, 0.0)))


def _gates_kernel(h_ref, wg_ref, wgt_ref, pcol_ref, prow_ref, cols_ref, rows_ref, carry_ref,
                  *, hv, hf, chunk):
    t = pl.program_id(1)
    tm = h_ref.shape[0]

    @pl.when(t == 0)
    def _():
        carry_ref[...] = jnp.zeros_like(carry_ref)

    h = h_ref[...]
    g_cols = _dot(h, wg_ref[...])
    g_rows = _dot_nt(wgt_ref[...], h)

    lane = lax.broadcasted_iota(jnp.int32, (tm, GATE_LANES), 1)
    val_c = _gate_act(g_cols, pcol_ref[0:1, :], pcol_ref[1:2, :], lane, hv, hf)
    sub = lax.broadcasted_iota(jnp.int32, (GATE_LANES, tm), 0)
    val_r = _gate_act(g_rows, prow_ref[:, 0:1], prow_ref[:, 1:2], sub, hv, hf)

    r = lax.broadcasted_iota(jnp.int32, (tm, tm), 0)
    c = lax.broadcasted_iota(jnp.int32, (tm, tm), 1)
    sh = chunk.bit_length() - 1
    same = (r >> sh) == (c >> sh)
    tril_blk = jnp.where((r >= c) & same, 1.0, 0.0).astype(BF16)
    triu_blk = jnp.where((r <= c) & same, 1.0, 0.0).astype(BF16)
    triu_all = jnp.where(r <= c, 1.0, 0.0).astype(BF16)

    cum_c = sum(jnp.dot(tril_blk, p, preferred_element_type=F32) for p in _split3(val_c))
    pieces_r = _split3(val_r)
    cum_r_blk = sum(jnp.dot(p, triu_blk, preferred_element_type=F32) for p in pieces_r)
    cum_r_all = sum(jnp.dot(p, triu_all, preferred_element_type=F32) for p in pieces_r)
    cum_r_all = cum_r_all + carry_ref[:, 0:1]

    is_decay_c = (lane >= hv) & (lane < 2 * hv)
    cols_ref[0] = jnp.where(is_decay_c, cum_c, val_c)
    is_decay_r = (sub >= hv) & (sub < 2 * hv)
    is_forget_r = (sub >= 2 * hv) & (sub < 2 * hv + hf)
    rows = jnp.where(is_decay_r, cum_r_blk, jnp.where(is_forget_r, cum_r_all, val_r))
    rows_ref[0] = rows
    carry_ref[...] = jnp.broadcast_to(cum_r_all[:, tm - 1:tm], carry_ref.shape)


def _gates(h2, wg, wgt, pcol, prow, batch, seq, hv, hf):
    m, d = h2.shape
    tm = _tile(seq, 512)
    per_b = seq // tm
    kern = functools.partial(_gates_kernel, hv=hv, hf=hf, chunk=DN_CHUNK)
    return pl.pallas_call(
        kern,
        grid=(batch, per_b),
        in_specs=[pl.BlockSpec((tm, d), lambda b, t: (b * per_b + t, 0)),
                  pl.BlockSpec((d, GATE_LANES), lambda b, t: (0, 0)),
                  pl.BlockSpec((GATE_LANES, d), lambda b, t: (0, 0)),
                  pl.BlockSpec((2, GATE_LANES), lambda b, t: (0, 0)),
                  pl.BlockSpec((GATE_LANES, 2), lambda b, t: (0, 0))],
        out_specs=[pl.BlockSpec((1, tm, GATE_LANES), lambda b, t: (b, t, 0)),
                   pl.BlockSpec((1, GATE_LANES, tm), lambda b, t: (b, 0, t))],
        out_shape=[jax.ShapeDtypeStruct((batch, seq, GATE_LANES), F32),
                   jax.ShapeDtypeStruct((batch, GATE_LANES, seq), F32)],
        scratch_shapes=[pltpu.VMEM((GATE_LANES, GATE_LANES), F32)],
        compiler_params=_params("parallel", "arbitrary"),
        name="gates",
    )(h2, wg, wgt, pcol, prow)


def _causal_conv(u, tail, w):
    out = u * w[CONV_K - 1:CONV_K, :]
    row8 = lax.broadcasted_iota(jnp.int32, tail.shape, 0)
    for s in range(1, CONV_K):
        rolled = pltpu.roll(u, s, 0)
        head = jnp.where(row8 < s, pltpu.roll(tail, s, 0), rolled[0:8])
        shifted = jnp.concatenate([head, rolled[8:]], axis=0)
        out = out + shifted * w[CONV_K - 1 - s:CONV_K - s, :]
    return out


INV_LEVELS = (DN_CHUNK // 8).bit_length() - 1
MASK_DIAG8, MASK_EYE, MASK_STRICT = 0, INV_LEVELS + 1, INV_LEVELS + 2
N_MASKS = INV_LEVELS + 3


def _inverse_masks(n):
    r = lax.broadcasted_iota(jnp.int32, (n, n), 0)
    c = lax.broadcasted_iota(jnp.int32, (n, n), 1)
    masks = [(r >> 3) == (c >> 3)]
    for sh in range(3, 3 + INV_LEVELS):
        masks.append(((r >> (sh + 1)) == (c >> (sh + 1))) & ((r >> sh) == (c >> sh) + 1))
    masks += [r == c, r > c]
    return [jnp.where(m, 1.0, 0.0) for m in masks]


def _unit_lower_inverses(lms, mask_ref):
    n0s = [-(lm * mask_ref[MASK_DIAG8]) for lm in lms]
    n2s = [_dot(n0, n0) for n0 in n0s]
    n4s = [_dot(n2, n2) for n2 in n2s]
    ps = [mask_ref[MASK_EYE] + n0 for n0 in n0s]
    ps = [p + _dot(p, n2) for p, n2 in zip(ps, n2s)]
    ps = [p + _dot(p, n4) for p, n4 in zip(ps, n4s)]
    for level in range(1, INV_LEVELS + 1):
        xs = [_dot(lm * mask_ref[level], p) for lm, p in zip(lms, ps)]
        ps = [p - _dot(p, x) for p, x in zip(ps, xs)]
    return ps


def _deltanet_kernel(q_ref, k_ref, v_ref, z_ref, wq_ref, wk_ref, wv_ref, cols_ref, rows_ref, nw_ref,
                     o_ref, s_ref, qt_ref, kt_ref, vt_ref, mask_ref, u_ref, lhs1_ref, lhs2_ref, egl_ref,
                     *, hv, rep, groups, unroll, scan_unroll):
    i = pl.program_id(1)
    tb = pl.program_id(2)
    rows_blk = q_ref.shape[0]
    n_chunks = rows_blk // DN_CHUNK
    c_ = DN_CHUNK
    d = HEAD_DIM
    nh = groups * rep

    @pl.when(tb == 0)
    def _():
        s_ref[...] = jnp.zeros_like(s_ref)
        qt_ref[...] = jnp.zeros_like(qt_ref)
        kt_ref[...] = jnp.zeros_like(kt_ref)
        vt_ref[...] = jnp.zeros_like(vt_ref)
        for j, m in enumerate(_inverse_masks(c_)):
            mask_ref[j] = m

    r = lax.broadcasted_iota(jnp.int32, (c_, c_), 0)
    c = lax.broadcasted_iota(jnp.int32, (c_, c_), 1)
    lane = lax.broadcasted_iota(jnp.int32, (c_, GATE_LANES), 1)
    nw = nw_ref[...]

    def tail_of(ref, tail_ref, ci, off):
        prev = pl.multiple_of(jnp.maximum(off - 16, 0), 16)
        inside = ref[pl.ds(prev, 16), :].astype(F32)[8:16]
        return jnp.where(ci == 0, tail_ref[...], inside)

    def prepare_body(j, carry):
        chains = []
        for uu in range(unroll):
            ci = j * unroll + uu
            off = pl.multiple_of(ci * c_, c_)
            sl = pl.ds(off, c_)
            q_raw = q_ref[sl, :].astype(F32)
            k_raw = k_ref[sl, :].astype(F32)
            v_raw = v_ref[sl, :].astype(F32)
            qc = _causal_conv(q_raw, tail_of(q_ref, qt_ref, ci, off), wq_ref[...])
            kc = _causal_conv(k_raw, tail_of(k_ref, kt_ref, ci, off), wk_ref[...])
            vc = _causal_conv(v_raw, tail_of(v_ref, vt_ref, ci, off), wv_ref[...])
            qc = qc * _sigmoid(qc)
            kc = kc * _sigmoid(kc)
            vc = vc * _sigmoid(vc)
            cols = cols_ref[0, sl, :]
            row0 = pl.multiple_of(((hv + i * nh) // 8) * 8, 8)
            rows8 = rows_ref[0, pl.ds(row0, 8), sl]
            sub8 = lax.broadcasted_iota(jnp.int32, (8, c_), 0)
            for g in range(groups):
                qg_ = qc[:, g * d:(g + 1) * d]
                kg_ = kc[:, g * d:(g + 1) * d]
                qn = qg_ * lax.rsqrt(jnp.sum(qg_ * qg_, axis=-1, keepdims=True) + EPS) * (d ** -0.5)
                kn = kg_ * lax.rsqrt(jnp.sum(kg_ * kg_, axis=-1, keepdims=True) + EPS)
                kk = _dot_nt(kn, kn)
                qk = _dot_nt(qn, kn)
                for hh in range(rep):
                    hl = g * rep + hh
                    head = i * nh + hl
                    beta_c = jnp.sum(jnp.where(lane == head, cols, 0.0), axis=1, keepdims=True)
                    gc_c = jnp.sum(jnp.where(lane == hv + head, cols, 0.0), axis=1, keepdims=True)
                    gc_r = jnp.sum(jnp.where(sub8 == (hv + head) % 8, rows8, 0.0), axis=0, keepdims=True)
                    g_last = gc_r[:, c_ - 1:c_]
                    decay = jnp.exp(jnp.where(r >= c, gc_c - gc_r, -1e30))
                    lm = (beta_c * kk) * (decay * mask_ref[MASK_STRICT])
                    eg = jnp.exp(gc_c)
                    v_h = vc[:, hl * d:(hl + 1) * d]
                    rhs = jnp.concatenate([v_h * beta_c, kn * (beta_c * eg)], axis=1).astype(BF16)
                    kd = kn * jnp.exp(g_last - gc_c)
                    lhs2_ref[hl, ci] = jnp.concatenate([qk * decay, kd.T], axis=0).astype(BF16)
                    egl_ref[hl, ci] = jnp.broadcast_to(jnp.exp(g_last), (8, d))
                    chains.append((hl, ci, sl, lm, rhs, (qn * eg).astype(BF16)))
        tinvs = _unit_lower_inverses([ch[3] for ch in chains], mask_ref)
        uws = [_dot(tinv, ch[4]) for tinv, ch in zip(tinvs, chains)]
        for uw, (hl, ci, sl, _, _, qg) in zip(uws, chains):
            u_ref[hl, sl, :] = uw[:, :d]
            lhs1_ref[hl, ci] = jnp.concatenate([uw[:, d:].astype(BF16), qg], axis=0)
        return carry

    lax.fori_loop(0, n_chunks // unroll, prepare_body, 0)

    heads = range(nh)

    def scan_chunk(ci):
        off = pl.multiple_of(ci * c_, c_)
        sl = pl.ds(off, c_)
        states = [s_ref[hl] for hl in heads]
        ws_qs = [_dot(lhs1_ref[hl, ci], states[hl]) for hl in heads]
        v_new = [u_ref[hl, sl, :] - ws_qs[hl][:c_] for hl in heads]
        av_kv = [_dot(lhs2_ref[hl, ci], v_new[hl]) for hl in heads]
        for hl in heads:
            s_ref[hl] = states[hl] * egl_ref[hl, ci, 0:1, :] + av_kv[hl][c_:]
        for hl in heads:
            o = ws_qs[hl][c_:] + av_kv[hl][:c_]
            z = z_ref[sl, hl * d:(hl + 1) * d].astype(F32)
            out = _rms(o) * nw * (z * _sigmoid(z))
            o_ref[sl, hl * d:(hl + 1) * d] = out.astype(o_ref.dtype)

    def scan_body(j, carry):
        for uu in range(scan_unroll):
            scan_chunk(j * scan_unroll + uu)
        return carry

    lax.fori_loop(0, n_chunks // scan_unroll, scan_body, 0)

    qt_ref[...] = q_ref[rows_blk - 16:, :].astype(F32)[8:16]
    kt_ref[...] = k_ref[rows_blk - 16:, :].astype(F32)[8:16]
    vt_ref[...] = v_ref[rows_blk - 16:, :].astype(F32)[8:16]


def _deltanet(proj, conv_wt, cols, rows, norm_w, batch, seq, hqk, hv):
    rep = hv // hqk
    assert hv == rep * hqk and DN_CHUNK == HEAD_DIM
    groups = 2 if hqk % 2 == 0 else 1
    nh = groups * rep
    assert 8 % nh == 0 and hv % nh == 0 and (2 * hqk) % nh == 0
    tb = _tile(seq, 1024)
    per_b = seq // tb
    wqk, wv = groups * HEAD_DIM, nh * HEAD_DIM
    nqk_blk, nv_blk = hqk // groups, hv // nh
    k_blk0 = nqk_blk
    v_blk0 = (2 * hqk) // nh
    z_blk0 = v_blk0 + nv_blk
    row = lambda b, i, t: b * per_b + t
    n_chunks = tb // DN_CHUNK
    unroll = next(u for u in (4, 2, 1) if n_chunks % u == 0 and u * nh <= 8)
    scan_unroll = 2 if n_chunks % 2 == 0 else 1
    kern = functools.partial(_deltanet_kernel, hv=hv, rep=rep, groups=groups, unroll=unroll,
                             scan_unroll=scan_unroll)
    return pl.pallas_call(
        kern,
        grid=(batch, nqk_blk, per_b),
        in_specs=[pl.BlockSpec((tb, wqk), lambda b, i, t: (row(b, i, t), i)),
                  pl.BlockSpec((tb, wqk), lambda b, i, t: (row(b, i, t), k_blk0 + i)),
                  pl.BlockSpec((tb, wv), lambda b, i, t: (row(b, i, t), v_blk0 + i)),
                  pl.BlockSpec((tb, wv), lambda b, i, t: (row(b, i, t), z_blk0 + i)),
                  pl.BlockSpec((CONV_K, wqk), lambda b, i, t: (0, i)),
                  pl.BlockSpec((CONV_K, wqk), lambda b, i, t: (0, k_blk0 + i)),
                  pl.BlockSpec((CONV_K, wv), lambda b, i, t: (0, v_blk0 + i)),
                  pl.BlockSpec((1, tb, GATE_LANES), lambda b, i, t: (b, t, 0)),
                  pl.BlockSpec((1, GATE_LANES, tb), lambda b, i, t: (b, 0, t)),
                  pl.BlockSpec((1, HEAD_DIM), lambda b, i, t: (0, 0))],
        out_specs=pl.BlockSpec((tb, wv), lambda b, i, t: (row(b, i, t), i)),
        out_shape=jax.ShapeDtypeStruct((batch * seq, hv * HEAD_DIM), BF16),
        scratch_shapes=[pltpu.VMEM((nh, HEAD_DIM, HEAD_DIM), F32),
                        pltpu.VMEM((8, wqk), F32),
                        pltpu.VMEM((8, wqk), F32),
                        pltpu.VMEM((8, wv), F32),
                        pltpu.VMEM((N_MASKS, DN_CHUNK, DN_CHUNK), F32),
                        pltpu.VMEM((nh, tb, HEAD_DIM), F32),
                        pltpu.VMEM((nh, n_chunks, 2 * DN_CHUNK, HEAD_DIM), BF16),
                        pltpu.VMEM((nh, n_chunks, 2 * DN_CHUNK, DN_CHUNK), BF16),
                        pltpu.VMEM((nh, n_chunks, 8, HEAD_DIM), F32)],
        compiler_params=_params("parallel", "parallel", "arbitrary"),
        name="deltanet",
    )(proj, proj, proj, proj, conv_wt, conv_wt, conv_wt, cols, rows, norm_w.reshape(1, HEAD_DIM))


LOG2E = 1.4426950408889634


MXU_COLS = 256


def _reduce_rows(x, op):
    rows, lanes = x.shape
    slabs = 8
    if rows % (8 * slabs) == 0:
        x3 = x.reshape(slabs, rows // slabs, lanes)
        x = x3[0]
        for j in range(1, slabs):
            x = op(x, x3[j])
    final = jnp.max if op is jnp.maximum else jnp.sum
    return final(x, axis=0, keepdims=True)


def _fox_kernel(q_ref, k_ref, v_ref, f_ref, o_ref, vt_ref, fcol_ref, s_ref, m_ref, l_ref, acc_ref,
                *, scale):
    qi = pl.program_id(2)
    tq = q_ref.shape[0]
    tk = tq
    seq = k_ref.shape[0]
    qt = min(MXU_COLS, tq)
    d = HEAD_DIM

    @pl.when(qi == 0)
    def _():
        def prep(bi, carry):
            off = pl.multiple_of(bi * d, d)
            vt_ref[:, pl.ds(off, d)] = v_ref[pl.ds(off, d), :].astype(F32).T.astype(BF16)
            frow = f_ref[0, 0, :, pl.ds(off, d)] * LOG2E
            fcol_ref[pl.ds(off, d), :] = jnp.broadcast_to(frow, (d, d)).T
            return carry
        lax.fori_loop(0, seq // d, prep, 0)

    m_ref[...] = jnp.full_like(m_ref, -1e30)
    l_ref[...] = jnp.zeros_like(l_ref)
    acc_ref[...] = jnp.zeros_like(acc_ref)

    def scores(kj, slot):
        off = pl.multiple_of(kj * tk, tk)
        s_ref[slot] = _dot_nt(k_ref[pl.ds(off, tk), :], q_ref[...])

    def consume(kj, slot, masked):
        off = pl.multiple_of(kj * tk, tk)
        fcol = fcol_ref[pl.ds(off, tk), :]
        fcol = jnp.concatenate([fcol] * (qt // d), axis=1)
        vt = vt_ref[:, pl.ds(off, tk)]
        for t in range(tq // qt):
            lanes = slice(t * qt, (t + 1) * qt)
            s = s_ref[slot, :, lanes] * (scale * LOG2E) - fcol
            if masked:
                key = lax.broadcasted_iota(jnp.int32, (tk, qt), 0)
                qry = lax.broadcasted_iota(jnp.int32, (tk, qt), 1) + t * qt
                s = jnp.where(key <= qry, s, -1e30)
            m_prev = m_ref[:, lanes]
            m_new = jnp.maximum(m_prev, _reduce_rows(s, jnp.maximum))
            p = jnp.exp2(s - m_new)
            alpha = jnp.exp2(m_prev - m_new)
            l_ref[:, lanes] = alpha * l_ref[:, lanes] + _reduce_rows(p, jnp.add)
            m_ref[:, lanes] = m_new
            acc_ref[:, lanes] = alpha * acc_ref[:, lanes] + _dot(vt, p)

    scores(0, 0)

    def body(i, carry):
        kj = 2 * i
        scores(kj + 1, 1)
        consume(kj, 0, False)
        scores(kj + 2, 0)
        consume(kj + 1, 1, False)
        return carry

    lax.fori_loop(0, qi // 2, body, 0)

    @pl.when(qi % 2 == 0)
    def _():
        consume(qi, 0, True)

    @pl.when(qi % 2 == 1)
    def _():
        scores(qi, 1)
        consume(qi - 1, 0, False)
        consume(qi, 1, True)

    o_ref[...] = (acc_ref[...] / l_ref[...]).T.astype(o_ref.dtype)


def _fox(proj, f_rows, batch, seq, base, hf):
    tq = _tile(seq, 512)
    nq = seq // tq
    kern = functools.partial(_fox_kernel, scale=HEAD_DIM ** -0.5)
    return pl.pallas_call(
        kern,
        grid=(batch, hf, nq),
        in_specs=[pl.BlockSpec((tq, HEAD_DIM), lambda b, h, qi: (b * nq + qi, base + h)),
                  pl.BlockSpec((seq, HEAD_DIM), lambda b, h, qi: (b, base + hf + h)),
                  pl.BlockSpec((seq, HEAD_DIM), lambda b, h, qi: (b, base + 2 * hf + h)),
                  pl.BlockSpec((1, 1, 1, seq), lambda b, h, qi: (b, h, 0, 0))],
        out_specs=pl.BlockSpec((tq, HEAD_DIM), lambda b, h, qi: (b * nq + qi, h)),
        out_shape=jax.ShapeDtypeStruct((batch * seq, hf * HEAD_DIM), BF16),
        scratch_shapes=[pltpu.VMEM((HEAD_DIM, seq), BF16),
                        pltpu.VMEM((seq, HEAD_DIM), F32),
                        pltpu.VMEM((2, tq, tq), F32),
                        pltpu.VMEM((1, tq), F32),
                        pltpu.VMEM((1, tq), F32),
                        pltpu.VMEM((HEAD_DIM, tq), F32)],
        compiler_params=_params("parallel", "parallel", "arbitrary"),
        name="fox_attention",
    )(proj, proj, proj, f_rows)


def _merge_kernel(a1_ref, w1_ref, a2_ref, w2_ref, m1_ref, m2_ref, o_ref):
    y1 = _dot(a1_ref[...], w1_ref[...])
    y2 = _dot(a2_ref[...], w2_ref[...])
    g1 = _sigmoid(m1_ref[...].astype(F32))
    g2 = _sigmoid(m2_ref[...].astype(F32))
    o_ref[...] = (g1 * y1 + g2 * y2).astype(o_ref.dtype)


def _merge(o_dn, w_dn, o_fox, w_fox, layer, proj, merge_base_cols):
    m, k1 = o_dn.shape
    k2 = o_fox.shape[1]
    d = w_dn.shape[2]
    tm, tn = _tile(m, 512), _tile(d, 512)
    assert merge_base_cols % tn == 0
    mb = merge_base_cols // tn
    return pl.pallas_call(
        _merge_kernel,
        grid=(m // tm, d // tn),
        in_specs=[pl.BlockSpec((tm, k1), lambda i, j: (i, 0)),
                  pl.BlockSpec((None, k1, tn), lambda i, j: (layer, 0, j)),
                  pl.BlockSpec((tm, k2), lambda i, j: (i, 0)),
                  pl.BlockSpec((None, k2, tn), lambda i, j: (layer, 0, j)),
                  pl.BlockSpec((tm, tn), lambda i, j: (i, mb + j)),
                  pl.BlockSpec((tm, tn), lambda i, j: (i, mb + d // tn + j))],
        out_specs=pl.BlockSpec((tm, tn), lambda i, j: (i, j)),
        out_shape=jax.ShapeDtypeStruct((m, d), BF16),
        compiler_params=_params("parallel", "parallel"),
        name="branch_merge",
    )(o_dn, w_dn, o_fox, w_fox, proj, proj)


def _out_res_kernel(a_ref, w_ref, x_ref, g_ref, gate_ref, o_ref, acc_ref):
    k = pl.program_id(1)

    @pl.when(k == 0)
    def _():
        acc_ref[...] = jnp.zeros_like(acc_ref)

    acc_ref[...] += _dot(a_ref[...], w_ref[...])

    @pl.when(k == pl.num_programs(1) - 1)
    def _():
        y = _rms(acc_ref[...]) * g_ref[...]
        o_ref[...] = x_ref[...] + gate_ref[0] * y


def _out_residual(a, w, layer, x2, gain, mod3, gate_idx, seq):
    m, k = a.shape
    d = w.shape[2]
    tm = _tile(seq, 512)
    tk = _tile(k, 512)
    per_b = seq // tm
    return pl.pallas_call(
        _out_res_kernel,
        grid=(m // tm, k // tk),
        in_specs=[pl.BlockSpec((tm, tk), lambda i, kk: (i, kk)),
                  pl.BlockSpec((None, tk, d), lambda i, kk: (layer, kk, 0)),
                  pl.BlockSpec((tm, d), lambda i, kk: (i, 0)),
                  pl.BlockSpec((1, d), lambda i, kk: (0, 0)),
                  pl.BlockSpec((1, 1, d), lambda i, kk: ((i // per_b) * 6 + gate_idx, 0, 0))],
        out_specs=pl.BlockSpec((tm, d), lambda i, kk: (i, 0)),
        out_shape=jax.ShapeDtypeStruct((m, d), F32),
        scratch_shapes=[pltpu.VMEM((tm, d), F32)],
        compiler_params=_params("parallel", "arbitrary"),
        name="proj_residual",
    )(a, w, x2, gain.reshape(1, d), mod3)


def _glu_kernel(a_ref, wg_ref, wu_ref, o_ref):
    a = a_ref[...]
    g = _dot(a, wg_ref[...])
    u = _dot(a, wu_ref[...])
    o_ref[...] = (g * _sigmoid(g) * u).astype(o_ref.dtype)


def _glu(a, wg, wu, layer):
    m, k = a.shape
    n = wg.shape[2]
    tm, tn = _tile(m, 1024), _tile(n, 512)
    return pl.pallas_call(
        _glu_kernel,
        grid=(m // tm, n // tn),
        in_specs=[pl.BlockSpec((tm, k), lambda i, j: (i, 0)),
                  pl.BlockSpec((None, k, tn), lambda i, j: (layer, 0, j)),
                  pl.BlockSpec((None, k, tn), lambda i, j: (layer, 0, j))],
        out_specs=pl.BlockSpec((tm, tn), lambda i, j: (i, j)),
        out_shape=jax.ShapeDtypeStruct((m, n), BF16),
        compiler_params=_params("parallel", "parallel"),
        name="swiglu_up",
    )(a, wg, wu)


def kernel(x, c, w_ada, b_ada, norm_gains, w_in, dn_conv, dn_a_log, dn_dt_bias, dn_norm_w, fox_f_bias,
           w_branch_dn, w_branch_fox, w_out, w_gate, w_up, w_down):
    batch, seq, d = x.shape
    depth = w_ada.shape[0]
    hv = dn_a_log.shape[1]
    hf = fox_f_bias.shape[1]
    v_dim = hv * HEAD_DIM
    conv_dim = dn_conv.shape[1]
    qk_dim = (conv_dim - v_dim) // 2
    hqk = qk_dim // HEAD_DIM
    fox_dim = hf * HEAD_DIM
    assert 2 * hv + hf <= GATE_LANES

    o_z = conv_dim
    o_b = o_z + v_dim
    o_a = o_b + hv
    o_fq = o_a + hv
    o_ff = o_fq + 3 * fox_dim
    o_mg = o_ff + hf
    fox_base = (conv_dim + v_dim) // HEAD_DIM
    merge_base = conv_dim + v_dim + 3 * fox_dim

    mod = _modulation(c, w_ada, b_ada)
    x2 = x.reshape(batch * seq, d)
    pad = GATE_LANES - (2 * hv + hf)
    zpad = jnp.zeros((pad,), F32)

    main_cols = ((0, o_b), (o_fq, 3 * fox_dim), (o_mg, w_in.shape[2] - o_mg))
    w_g_all = jnp.concatenate([w_in[:, :, o_b:o_fq], w_in[:, :, o_ff:o_mg],
                               jnp.zeros((depth, d, pad), F32)], axis=2).astype(BF16)
    w_dn_all, w_fox_all, w_out_all = (w.astype(BF16) for w in (w_branch_dn, w_branch_fox, w_out))
    w_gate_all, w_up_all, w_down_all = (w.astype(BF16) for w in (w_gate, w_up, w_down))

    for l in range(depth):
        mod3 = mod[l].reshape(batch * 6, 1, d)
        w_g = w_g_all[l]
        bias = jnp.concatenate([jnp.zeros((hv,), F32), dn_dt_bias[l], fox_f_bias[l], zpad])
        mult = jnp.concatenate([jnp.ones((hv,), F32), -jnp.exp(dn_a_log[l]), jnp.ones((hf,), F32), zpad])
        pcol = jnp.stack([bias, mult], axis=0)

        h = _norm_mod(x2, norm_gains[l, 0], mod3, 1, 0, seq)
        proj = _in_proj(h, w_in, l, main_cols)
        cols, rows = _gates(h, w_g, w_g.T, pcol, pcol.T, batch, seq, hv, hf)
        o_dn = _deltanet(proj, dn_conv[l].T, cols, rows, dn_norm_w[l], batch, seq, hqk, hv)
        f_rows = rows[:, 2 * hv:2 * hv + hf, :].reshape(batch, hf, 1, seq)
        o_fox = _fox(proj, f_rows, batch, seq, fox_base, hf)
        ymix = _merge(o_dn, w_dn_all, o_fox, w_fox_all, l, proj, merge_base)
        x2 = _out_residual(ymix, w_out_all, l, x2, norm_gains[l, 1], mod3, 2, seq)

        h = _norm_mod(x2, norm_gains[l, 2], mod3, 4, 3, seq)
        gu = _glu(h, w_gate_all, w_up_all, l)
        x2 = _out_residual(gu, w_down_all, l, x2, norm_gains[l, 3], mod3, 5, seq)

    return x2.reshape(batch, seq, d)
```

```python
import functools

import jax
import jax.numpy as jnp
from jax import lax
from jax.experimental import pallas as pl
from jax.experimental.pallas import tpu as pltpu

EPS = 1e-6
HEAD_DIM = 128
CONV_K = 4
DN_CHUNK = 128
GATE_LANES = 128
VMEM_LIMIT_BYTES = 48 * 1024 * 1024

F32 = jnp.float32
BF16 = jnp.bfloat16


def _params(*sem):
    return pltpu.CompilerParams(dimension_semantics=sem, vmem_limit_bytes=VMEM_LIMIT_BYTES)


def _tile(dim, pref):
    if dim <= pref:
        return dim
    t = pref - pref % HEAD_DIM
    while dim % t:
        t -= HEAD_DIM
    assert t > 0, (dim, pref)
    return t


def _dot(a, b):
    return jnp.dot(a.astype(BF16), b.astype(BF16), preferred_element_type=F32)


def _dot_nt(a, b):
    return lax.dot_general(a.astype(BF16), b.astype(BF16), (((1,), (1,)), ((), ())),
                           preferred_element_type=F32)


def _dot_tn(a, b):
    return lax.dot_general(a.astype(BF16), b.astype(BF16), (((0,), (0,)), ((), ())),
                           preferred_element_type=F32)


def _sigmoid(x):
    return 0.5 * jnp.tanh(0.5 * x) + 0.5


def _rms(x):
    return x * lax.rsqrt(jnp.mean(x * x, axis=-1, keepdims=True) + EPS)


def _mod_kernel(c_ref, w_ref, b_ref, o_ref):
    c = c_ref[...]
    cond = c * _sigmoid(c)
    o_ref[0] = _dot(cond, w_ref[0]) + b_ref[0]


def _modulation(c, w_ada, b_ada):
    depth, d, n = w_ada.shape
    b = c.shape[0]
    rows = 16
    c_pad = jnp.zeros((rows, d), F32).at[:b].set(c)
    tn = _tile(n, 1024)
    out = pl.pallas_call(
        _mod_kernel,
        grid=(depth, n // tn),
        in_specs=[pl.BlockSpec((rows, d), lambda l, j: (0, 0)),
                  pl.BlockSpec((1, d, tn), lambda l, j: (l, 0, j)),
                  pl.BlockSpec((1, 1, tn), lambda l, j: (l, 0, j))],
        out_specs=pl.BlockSpec((1, rows, tn), lambda l, j: (l, 0, j)),
        out_shape=jax.ShapeDtypeStruct((depth, rows, n), F32),
        compiler_params=_params("parallel", "parallel"),
        name="adaln_mod",
    )(c_pad, w_ada, b_ada.reshape(depth, 1, n))
    return out[:, :b].reshape(depth, b, 6, d)


def _norm_mod_kernel(x_ref, g_ref, sc_ref, sh_ref, o_ref):
    y = _rms(x_ref[...]) * g_ref[...]
    o_ref[...] = (y * (1.0 + sc_ref[0]) + sh_ref[0]).astype(o_ref.dtype)


def _norm_mod(x2, gain, mod3, sc_idx, sh_idx, seq):
    m, d = x2.shape
    tm = _tile(seq, 512)
    per_b = seq // tm
    return pl.pallas_call(
        _norm_mod_kernel,
        grid=(m // tm,),
        in_specs=[pl.BlockSpec((tm, d), lambda i: (i, 0)),
                  pl.BlockSpec((1, d), lambda i: (0, 0)),
                  pl.BlockSpec((1, 1, d), lambda i: ((i // per_b) * 6 + sc_idx, 0, 0)),
                  pl.BlockSpec((1, 1, d), lambda i: ((i // per_b) * 6 + sh_idx, 0, 0))],
        out_specs=pl.BlockSpec((tm, d), lambda i: (i, 0)),
        out_shape=jax.ShapeDtypeStruct((m, d), BF16),
        compiler_params=_params("parallel"),
        name="norm_mod",
    )(x2, gain.reshape(1, d), mod3, mod3)


W_ROWS = 256


def _in_proj_kernel(a_ref, wa_ref, wb_ref, o_ref, w_ref, *, segments):
    j = pl.program_id(0)
    k = wa_ref.shape[0]

    @pl.when(pl.program_id(1) == 0)
    def _():
        for lo, hi, shift in segments:
            @pl.when((j >= lo) & (j < hi))
            def _():
                for r0 in range(0, k, W_ROWS):
                    rows = slice(r0, min(r0 + W_ROWS, k))
                    if shift == 0:
                        w_ref[rows, :] = wa_ref[rows, :].astype(BF16)
                    else:
                        w = jnp.concatenate([wa_ref[rows, shift:], wb_ref[rows, :shift]], axis=1)
                        w_ref[rows, :] = w.astype(BF16)

    o_ref[...] = _dot(a_ref[...], w_ref[...]).astype(o_ref.dtype)


def _in_proj(a, w_in, layer, seg_cols):
    m, k = a.shape
    n_out = sum(w for _, w in seg_cols)
    tn = 1024
    while any(w % tn for _, w in seg_cols):
        tn //= 2
    assert tn >= HEAD_DIM
    tm = _tile(m, 1024)
    segments, out0 = [], 0
    for src0, width in seg_cols:
        shift = src0 - out0
        assert 0 <= shift < HEAD_DIM
        segments.append((out0 // tn, (out0 + width) // tn, shift))
        out0 += width
    kern = functools.partial(_in_proj_kernel, segments=tuple(segments))
    nb = tn // HEAD_DIM
    last_blk = (w_in.shape[2] - 1) // HEAD_DIM
    return pl.pallas_call(
        kern,
        grid=(n_out // tn, m // tm),
        in_specs=[pl.BlockSpec((tm, k), lambda j, i: (i, 0)),
                  pl.BlockSpec((None, k, tn), lambda j, i: (layer, 0, j)),
                  pl.BlockSpec((None, k, HEAD_DIM),
                               lambda j, i: (layer, 0, jnp.minimum((j + 1) * nb, last_blk)))],
        out_specs=pl.BlockSpec((tm, tn), lambda j, i: (i, j)),
        out_shape=jax.ShapeDtypeStruct((m, n_out), BF16),
        scratch_shapes=[pltpu.VMEM((k, tn), BF16)],
        compiler_params=_params("parallel", "arbitrary"),
        name="in_proj",
    )(a, w_in, w_in)


def _split3(x):
    hi = x.astype(BF16)
    r1 = x - hi.astype(F32)
    mid = r1.astype(BF16)
    lo = (r1 - mid.astype(F32)).astype(BF16)
    return hi, mid, lo


def _gate_act(x, bias, mult, idx, hv, hf):
    xb = x + bias
    e = jnp.exp(-jnp.abs(xb))
    l1p = jnp.log(1.0 + e)
    sig = jnp.where(xb >= 0, 1.0, e) / (1.0 + e)
    softplus = jnp.maximum(xb, 0.0) + l1p
    logsig = jnp.minimum(xb, 0.0) - l1p
    return jnp.where(idx < hv, sig,
                     jnp.where(idx < 2 * hv, mult * softplus,
                               jnp.where(idx < 2 * hv + hf, logsig, 0.0)))


def _gates_kernel(h_ref, wa_ref, wb_ref, pcol_ref, cols_ref, rows_ref, carry_ref, *, hv, hf, chunk):
    t = pl.program_id(1)
    tm = h_ref.shape[0]

    @pl.when(t == 0)
    def _():
        carry_ref[...] = jnp.zeros_like(carry_ref)

    lane_w = lax.broadcasted_iota(jnp.int32, wa_ref.shape, 1)
    w = jnp.where(lane_w < 2 * hv, wa_ref[...], jnp.where(lane_w < 2 * hv + hf, wb_ref[...], 0.0))
    g_cols = _dot(h_ref[...], w)

    lane = lax.broadcasted_iota(jnp.int32, (tm, GATE_LANES), 1)
    val_c = _gate_act(g_cols, pcol_ref[0:1, :], pcol_ref[1:2, :], lane, hv, hf)
    sub = lax.broadcasted_iota(jnp.int32, (GATE_LANES, tm), 0)
    val_r = val_c.T

    r = lax.broadcasted_iota(jnp.int32, (tm, tm), 0)
    c = lax.broadcasted_iota(jnp.int32, (tm, tm), 1)
    sh = chunk.bit_length() - 1
    same = (r >> sh) == (c >> sh)
    tril_blk = jnp.where((r >= c) & same, 1.0, 0.0).astype(BF16)
    triu_blk = jnp.where((r <= c) & same, 1.0, 0.0).astype(BF16)
    triu_all = jnp.where(r <= c, 1.0, 0.0).astype(BF16)

    cum_c = sum(jnp.dot(tril_blk, p, preferred_element_type=F32) for p in _split3(val_c))
    pieces_r = _split3(val_r)
    cum_r_blk = sum(jnp.dot(p, triu_blk, preferred_element_type=F32) for p in pieces_r)
    cum_r_all = sum(jnp.dot(p, triu_all, preferred_element_type=F32) for p in pieces_r)
    cum_r_all = cum_r_all + carry_ref[:, 0:1]

    is_decay_c = (lane >= hv) & (lane < 2 * hv)
    cols_ref[0] = jnp.where(is_decay_c, cum_c, val_c)
    is_decay_r = (sub >= hv) & (sub < 2 * hv)
    is_forget_r = (sub >= 2 * hv) & (sub < 2 * hv + hf)
    rows = jnp.where(is_decay_r, cum_r_blk, jnp.where(is_forget_r, cum_r_all, val_r))
    rows_ref[0] = rows
    carry_ref[...] = jnp.broadcast_to(cum_r_all[:, tm - 1:tm], carry_ref.shape)


def _gates(h2, w_in, layer, col_ba, col_f, pcol, batch, seq, hv, hf):
    m, d = h2.shape
    tm = _tile(seq, 512)
    per_b = seq // tm
    assert col_ba % GATE_LANES == 0 and col_f % GATE_LANES == 2 * hv
    blk_ba, blk_f = col_ba // GATE_LANES, col_f // GATE_LANES
    kern = functools.partial(_gates_kernel, hv=hv, hf=hf, chunk=DN_CHUNK)
    return pl.pallas_call(
        kern,
        grid=(batch, per_b),
        in_specs=[pl.BlockSpec((tm, d), lambda b, t: (b * per_b + t, 0)),
                  pl.BlockSpec((None, d, GATE_LANES), lambda b, t: (layer, 0, blk_ba)),
                  pl.BlockSpec((None, d, GATE_LANES), lambda b, t: (layer, 0, blk_f)),
                  pl.BlockSpec((2, GATE_LANES), lambda b, t: (0, 0))],
        out_specs=[pl.BlockSpec((1, tm, GATE_LANES), lambda b, t: (b, t, 0)),
                   pl.BlockSpec((1, GATE_LANES, tm), lambda b, t: (b, 0, t))],
        out_shape=[jax.ShapeDtypeStruct((batch, seq, GATE_LANES), F32),
                   jax.ShapeDtypeStruct((batch, GATE_LANES, seq), F32)],
        scratch_shapes=[pltpu.VMEM((GATE_LANES, GATE_LANES), F32)],
        compiler_params=_params("parallel", "arbitrary"),
        name="gates",
    )(h2, w_in, w_in, pcol)


def _causal_conv(u, tail, w):
    out = u * w[CONV_K - 1:CONV_K, :]
    row8 = lax.broadcasted_iota(jnp.int32, tail.shape, 0)
    for s in range(1, CONV_K):
        rolled = pltpu.roll(u, s, 0)
        head = jnp.where(row8 < s, pltpu.roll(tail, s, 0), rolled[0:8])
        shifted = jnp.concatenate([head, rolled[8:]], axis=0)
        out = out + shifted * w[CONV_K - 1 - s:CONV_K - s, :]
    return out


INV_LEVELS = (DN_CHUNK // 8).bit_length() - 1
MASK_DIAG8, MASK_EYE, MASK_STRICT = 0, INV_LEVELS + 1, INV_LEVELS + 2
N_MASKS = INV_LEVELS + 3


def _inverse_masks(n):
    r = lax.broadcasted_iota(jnp.int32, (n, n), 0)
    c = lax.broadcasted_iota(jnp.int32, (n, n), 1)
    masks = [(r >> 3) == (c >> 3)]
    for sh in range(3, 3 + INV_LEVELS):
        masks.append(((r >> (sh + 1)) == (c >> (sh + 1))) & ((r >> sh) == (c >> sh) + 1))
    masks += [r == c, r > c]
    return [jnp.where(m, 1.0, 0.0) for m in masks]


def _unit_lower_inverses(lms, mask_ref):
    n0s = [-(lm * mask_ref[MASK_DIAG8]) for lm in lms]
    n2s = [_dot(n0, n0) for n0 in n0s]
    n4s = [_dot(n2, n2) for n2 in n2s]
    ps = [mask_ref[MASK_EYE] + n0 for n0 in n0s]
    ps = [p + _dot(p, n2) for p, n2 in zip(ps, n2s)]
    ps = [p + _dot(p, n4) for p, n4 in zip(ps, n4s)]
    for level in range(1, INV_LEVELS + 1):
        xs = [_dot(lm * mask_ref[level], p) for lm, p in zip(lms, ps)]
        ps = [p - _dot(p, x) for p, x in zip(ps, xs)]
    return ps


def _deltanet_kernel(q_ref, k_ref, v_ref, z_ref, wq_ref, wk_ref, wv_ref, cols_ref, rows_ref, nw_ref,
                     o_ref, s_ref, qt_ref, kt_ref, vt_ref, mask_ref, u_ref, lhs1_ref, lhs2_ref, egl_ref,
                     *, hv, rep, groups, unroll, scan_unroll):
    i = pl.program_id(1)
    tb = pl.program_id(2)
    rows_blk = q_ref.shape[0]
    n_chunks = rows_blk // DN_CHUNK
    c_ = DN_CHUNK
    d = HEAD_DIM
    nh = groups * rep

    @pl.when(tb == 0)
    def _():
        s_ref[...] = jnp.zeros_like(s_ref)
        qt_ref[...] = jnp.zeros_like(qt_ref)
        kt_ref[...] = jnp.zeros_like(kt_ref)
        vt_ref[...] = jnp.zeros_like(vt_ref)
        for j, m in enumerate(_inverse_masks(c_)):
            mask_ref[j] = m

    r = lax.broadcasted_iota(jnp.int32, (c_, c_), 0)
    c = lax.broadcasted_iota(jnp.int32, (c_, c_), 1)
    lane = lax.broadcasted_iota(jnp.int32, (c_, GATE_LANES), 1)
    nw = nw_ref[...]

    def tail_of(ref, tail_ref, ci, off):
        prev = pl.multiple_of(jnp.maximum(off - 16, 0), 16)
        inside = ref[pl.ds(prev, 16), :].astype(F32)[8:16]
        return jnp.where(ci == 0, tail_ref[...], inside)

    def prepare_body(j, carry):
        chains = []
        for uu in range(unroll):
            ci = j * unroll + uu
            off = pl.multiple_of(ci * c_, c_)
            sl = pl.ds(off, c_)
            q_raw = q_ref[sl, :].astype(F32)
            k_raw = k_ref[sl, :].astype(F32)
            v_raw = v_ref[sl, :].astype(F32)
            qc = _causal_conv(q_raw, tail_of(q_ref, qt_ref, ci, off), wq_ref[...])
            kc = _causal_conv(k_raw, tail_of(k_ref, kt_ref, ci, off), wk_ref[...])
            vc = _causal_conv(v_raw, tail_of(v_ref, vt_ref, ci, off), wv_ref[...])
            qc = qc * _sigmoid(qc)
            kc = kc * _sigmoid(kc)
            vc = vc * _sigmoid(vc)
            cols = cols_ref[0, sl, :]
            row0 = pl.multiple_of(((hv + i * nh) // 8) * 8, 8)
            rows8 = rows_ref[0, pl.ds(row0, 8), sl]
            sub8 = lax.broadcasted_iota(jnp.int32, (8, c_), 0)
            for g in range(groups):
                qg_ = qc[:, g * d:(g + 1) * d]
                kg_ = kc[:, g * d:(g + 1) * d]
                qn = qg_ * lax.rsqrt(jnp.sum(qg_ * qg_, axis=-1, keepdims=True) + EPS) * (d ** -0.5)
                kn = kg_ * lax.rsqrt(jnp.sum(kg_ * kg_, axis=-1, keepdims=True) + EPS)
                kk = _dot_nt(kn, kn)
                qk = _dot_nt(qn, kn)
                for hh in range(rep):
                    hl = g * rep + hh
                    head = i * nh + hl
                    beta_c = jnp.sum(jnp.where(lane == head, cols, 0.0), axis=1, keepdims=True)
                    gc_c = jnp.sum(jnp.where(lane == hv + head, cols, 0.0), axis=1, keepdims=True)
                    gc_r = jnp.sum(jnp.where(sub8 == (hv + head) % 8, rows8, 0.0), axis=0, keepdims=True)
                    g_last = gc_r[:, c_ - 1:c_]
                    decay = jnp.exp(jnp.where(r >= c, gc_c - gc_r, -1e30))
                    lm = (beta_c * kk) * (decay * mask_ref[MASK_STRICT])
                    eg = jnp.exp(gc_c)
                    v_h = vc[:, hl * d:(hl + 1) * d]
                    rhs = jnp.concatenate([v_h * beta_c, kn * (beta_c * eg)], axis=1).astype(BF16)
                    kd = kn * jnp.exp(g_last - gc_c)
                    lhs2_ref[hl, ci] = jnp.concatenate([qk * decay, kd.T], axis=0).astype(BF16)
                    egl_ref[hl, ci] = jnp.broadcast_to(jnp.exp(g_last), (8, d))
                    chains.append((hl, ci, sl, lm, rhs, (qn * eg).astype(BF16)))
        tinvs = _unit_lower_inverses([ch[3] for ch in chains], mask_ref)
        uws = [_dot(tinv, ch[4]) for tinv, ch in zip(tinvs, chains)]
        for uw, (hl, ci, sl, _, _, qg) in zip(uws, chains):
            u_ref[hl, sl, :] = uw[:, :d]
            lhs1_ref[hl, ci] = jnp.concatenate([uw[:, d:].astype(BF16), qg], axis=0)
        return carry

    lax.fori_loop(0, n_chunks // unroll, prepare_body, 0)

    heads = range(nh)

    def scan_chunk(ci):
        off = pl.multiple_of(ci * c_, c_)
        sl = pl.ds(off, c_)
        states = [s_ref[hl] for hl in heads]
        ws_qs = [_dot(lhs1_ref[hl, ci], states[hl]) for hl in heads]
        v_new = [u_ref[hl, sl, :] - ws_qs[hl][:c_] for hl in heads]
        av_kv = [_dot(lhs2_ref[hl, ci], v_new[hl]) for hl in heads]
        for hl in heads:
            s_ref[hl] = states[hl] * egl_ref[hl, ci, 0:1, :] + av_kv[hl][c_:]
        for hl in heads:
            o = ws_qs[hl][c_:] + av_kv[hl][:c_]
            z = z_ref[sl, hl * d:(hl + 1) * d].astype(F32)
            out = _rms(o) * nw * (z * _sigmoid(z))
            o_ref[sl, hl * d:(hl + 1) * d] = out.astype(o_ref.dtype)

    def scan_body(j, carry):
        for uu in range(scan_unroll):
            scan_chunk(j * scan_unroll + uu)
        return carry

    lax.fori_loop(0, n_chunks // scan_unroll, scan_body, 0)

    qt_ref[...] = q_ref[rows_blk - 16:, :].astype(F32)[8:16]
    kt_ref[...] = k_ref[rows_blk - 16:, :].astype(F32)[8:16]
    vt_ref[...] = v_ref[rows_blk - 16:, :].astype(F32)[8:16]


def _deltanet(proj, conv_wt, cols, rows, norm_w, batch, seq, hqk, hv):
    rep = hv // hqk
    assert hv == rep * hqk and DN_CHUNK == HEAD_DIM
    groups = 2 if hqk % 2 == 0 else 1
    nh = groups * rep
    assert 8 % nh == 0 and hv % nh == 0 and (2 * hqk) % nh == 0
    tb = _tile(seq, 1024)
    per_b = seq // tb
    wqk, wv = groups * HEAD_DIM, nh * HEAD_DIM
    nqk_blk, nv_blk = hqk // groups, hv // nh
    k_blk0 = nqk_blk
    v_blk0 = (2 * hqk) // nh
    z_blk0 = v_blk0 + nv_blk
    row = lambda b, i, t: b * per_b + t
    n_chunks = tb // DN_CHUNK
    unroll = next(u for u in (4, 2, 1) if n_chunks % u == 0 and u * nh <= 8)
    scan_unroll = 2 if n_chunks % 2 == 0 else 1
    kern = functools.partial(_deltanet_kernel, hv=hv, rep=rep, groups=groups, unroll=unroll,
                             scan_unroll=scan_unroll)
    return pl.pallas_call(
        kern,
        grid=(batch, nqk_blk, per_b),
        in_specs=[pl.BlockSpec((tb, wqk), lambda b, i, t: (row(b, i, t), i)),
                  pl.BlockSpec((tb, wqk), lambda b, i, t: (row(b, i, t), k_blk0 + i)),
                  pl.BlockSpec((tb, wv), lambda b, i, t: (row(b, i, t), v_blk0 + i)),
                  pl.BlockSpec((tb, wv), lambda b, i, t: (row(b, i, t), z_blk0 + i)),
                  pl.BlockSpec((CONV_K, wqk), lambda b, i, t: (0, i)),
                  pl.BlockSpec((CONV_K, wqk), lambda b, i, t: (0, k_blk0 + i)),
                  pl.BlockSpec((CONV_K, wv), lambda b, i, t: (0, v_blk0 + i)),
                  pl.BlockSpec((1, tb, GATE_LANES), lambda b, i, t: (b, t, 0)),
                  pl.BlockSpec((1, GATE_LANES, tb), lambda b, i, t: (b, 0, t)),
                  pl.BlockSpec((1, HEAD_DIM), lambda b, i, t: (0, 0))],
        out_specs=pl.BlockSpec((tb, wv), lambda b, i, t: (row(b, i, t), i)),
        out_shape=jax.ShapeDtypeStruct((batch * seq, hv * HEAD_DIM), BF16),
        scratch_shapes=[pltpu.VMEM((nh, HEAD_DIM, HEAD_DIM), F32),
                        pltpu.VMEM((8, wqk), F32),
                        pltpu.VMEM((8, wqk), F32),
                        pltpu.VMEM((8, wv), F32),
                        pltpu.VMEM((N_MASKS, DN_CHUNK, DN_CHUNK), F32),
                        pltpu.VMEM((nh, tb, HEAD_DIM), F32),
                        pltpu.VMEM((nh, n_chunks, 2 * DN_CHUNK, HEAD_DIM), BF16),
                        pltpu.VMEM((nh, n_chunks, 2 * DN_CHUNK, DN_CHUNK), BF16),
                        pltpu.VMEM((nh, n_chunks, 8, HEAD_DIM), F32)],
        compiler_params=_params("parallel", "parallel", "arbitrary"),
        name="deltanet",
    )(proj, proj, proj, proj, conv_wt, conv_wt, conv_wt, cols, rows, norm_w.reshape(1, HEAD_DIM))


LOG2E = 1.4426950408889634


MXU_COLS = 256


def _reduce_rows(x, op):
    rows, lanes = x.shape
    slabs = 8
    if rows % (8 * slabs) == 0:
        x3 = x.reshape(slabs, rows // slabs, lanes)
        x = x3[0]
        for j in range(1, slabs):
            x = op(x, x3[j])
    final = jnp.max if op is jnp.maximum else jnp.sum
    return final(x, axis=0, keepdims=True)


def _fox_kernel(q_ref, k_ref, v_ref, f_ref, o_ref, vt_ref, fcol_ref, s_ref, m_ref, l_ref, acc_ref,
                *, scale):
    qi = pl.program_id(2)
    tq = q_ref.shape[0]
    tk = tq
    seq = k_ref.shape[0]
    qt = min(MXU_COLS, tq)
    d = HEAD_DIM

    @pl.when(qi == 0)
    def _():
        def prep(bi, carry):
            off = pl.multiple_of(bi * d, d)
            vt_ref[:, pl.ds(off, d)] = v_ref[pl.ds(off, d), :].astype(F32).T.astype(BF16)
            frow = f_ref[0, 0, :, pl.ds(off, d)] * LOG2E
            fcol_ref[pl.ds(off, d), :] = jnp.broadcast_to(frow, (d, d)).T
            return carry
        lax.fori_loop(0, seq // d, prep, 0)

    m_ref[...] = jnp.full_like(m_ref, -1e30)
    l_ref[...] = jnp.zeros_like(l_ref)
    acc_ref[...] = jnp.zeros_like(acc_ref)

    def scores(kj, slot):
        off = pl.multiple_of(kj * tk, tk)
        s_ref[slot] = _dot_nt(k_ref[pl.ds(off, tk), :], q_ref[...])

    def consume(kj, slot, masked):
        off = pl.multiple_of(kj * tk, tk)
        fcol = fcol_ref[pl.ds(off, tk), :]
        fcol = jnp.concatenate([fcol] * (qt // d), axis=1)
        vt = vt_ref[:, pl.ds(off, tk)]
        for t in range(tq // qt):
            lanes = slice(t * qt, (t + 1) * qt)
            s = s_ref[slot, :, lanes] * (scale * LOG2E) - fcol
            if masked:
                key = lax.broadcasted_iota(jnp.int32, (tk, qt), 0)
                qry = lax.broadcasted_iota(jnp.int32, (tk, qt), 1) + t * qt
                s = jnp.where(key <= qry, s, -1e30)
            m_prev = m_ref[:, lanes]
            m_new = jnp.maximum(m_prev, _reduce_rows(s, jnp.maximum))
            p = jnp.exp2(s - m_new)
            alpha = jnp.exp2(m_prev - m_new)
            l_ref[:, lanes] = alpha * l_ref[:, lanes] + _reduce_rows(p, jnp.add)
            m_ref[:, lanes] = m_new
            acc_ref[:, lanes] = alpha * acc_ref[:, lanes] + _dot(vt, p)

    scores(0, 0)

    def body(i, carry):
        kj = 2 * i
        scores(kj + 1, 1)
        consume(kj, 0, False)
        scores(kj + 2, 0)
        consume(kj + 1, 1, False)
        return carry

    lax.fori_loop(0, qi // 2, body, 0)

    @pl.when(qi % 2 == 0)
    def _():
        consume(qi, 0, True)

    @pl.when(qi % 2 == 1)
    def _():
        scores(qi, 1)
        consume(qi - 1, 0, False)
        consume(qi, 1, True)

    o_ref[...] = (acc_ref[...] / l_ref[...]).T.astype(o_ref.dtype)


def _fox(proj, f_rows, batch, seq, base, hf):
    tq = _tile(seq, 512)
    nq = seq // tq
    kern = functools.partial(_fox_kernel, scale=HEAD_DIM ** -0.5)
    return pl.pallas_call(
        kern,
        grid=(batch, hf, nq),
        in_specs=[pl.BlockSpec((tq, HEAD_DIM), lambda b, h, qi: (b * nq + qi, base + h)),
                  pl.BlockSpec((seq, HEAD_DIM), lambda b, h, qi: (b, base + hf + h)),
                  pl.BlockSpec((seq, HEAD_DIM), lambda b, h, qi: (b, base + 2 * hf + h)),
                  pl.BlockSpec((1, 1, 1, seq), lambda b, h, qi: (b, h, 0, 0))],
        out_specs=pl.BlockSpec((tq, HEAD_DIM), lambda b, h, qi: (b * nq + qi, h)),
        out_shape=jax.ShapeDtypeStruct((batch * seq, hf * HEAD_DIM), BF16),
        scratch_shapes=[pltpu.VMEM((HEAD_DIM, seq), BF16),
                        pltpu.VMEM((seq, HEAD_DIM), F32),
                        pltpu.VMEM((2, tq, tq), F32),
                        pltpu.VMEM((1, tq), F32),
                        pltpu.VMEM((1, tq), F32),
                        pltpu.VMEM((HEAD_DIM, tq), F32)],
        compiler_params=_params("parallel", "parallel", "arbitrary"),
        name="fox_attention",
    )(proj, proj, proj, f_rows)


def _merge_kernel(a1_ref, w1_ref, a2_ref, w2_ref, m1_ref, m2_ref, o_ref):
    y1 = _dot(a1_ref[...], w1_ref[...])
    y2 = _dot(a2_ref[...], w2_ref[...])
    g1 = _sigmoid(m1_ref[...].astype(F32))
    g2 = _sigmoid(m2_ref[...].astype(F32))
    o_ref[...] = (g1 * y1 + g2 * y2).astype(o_ref.dtype)


def _merge(o_dn, w_dn, o_fox, w_fox, layer, proj, merge_base_cols):
    m, k1 = o_dn.shape
    k2 = o_fox.shape[1]
    d = w_dn.shape[2]
    tm, tn = _tile(m, 512), _tile(d, 512)
    assert merge_base_cols % tn == 0
    mb = merge_base_cols // tn
    return pl.pallas_call(
        _merge_kernel,
        grid=(m // tm, d // tn),
        in_specs=[pl.BlockSpec((tm, k1), lambda i, j: (i, 0)),
                  pl.BlockSpec((None, k1, tn), lambda i, j: (layer, 0, j)),
                  pl.BlockSpec((tm, k2), lambda i, j: (i, 0)),
                  pl.BlockSpec((None, k2, tn), lambda i, j: (layer, 0, j)),
                  pl.BlockSpec((tm, tn), lambda i, j: (i, mb + j)),
                  pl.BlockSpec((tm, tn), lambda i, j: (i, mb + d // tn + j))],
        out_specs=pl.BlockSpec((tm, tn), lambda i, j: (i, j)),
        out_shape=jax.ShapeDtypeStruct((m, d), BF16),
        compiler_params=_params("parallel", "parallel"),
        name="branch_merge",
    )(o_dn, w_dn, o_fox, w_fox, proj, proj)


def _mm_f32_kernel(a_ref, w_ref, o_ref):
    o_ref[...] = _dot(a_ref[...], w_ref[...])


def _out_proj(a, w, layer):
    m, k = a.shape
    d = w.shape[2]
    pref = 512 if k > 4096 else 1024
    tm, tn = _tile(m, pref), _tile(d, pref)
    return pl.pallas_call(
        _mm_f32_kernel,
        grid=(m // tm, d // tn),
        in_specs=[pl.BlockSpec((tm, k), lambda i, j: (i, 0)),
                  pl.BlockSpec((None, k, tn), lambda i, j: (layer, 0, j))],
        out_specs=pl.BlockSpec((tm, tn), lambda i, j: (i, j)),
        out_shape=jax.ShapeDtypeStruct((m, d), F32),
        compiler_params=_params("parallel", "parallel"),
        name="out_proj",
    )(a, w)


def _res_norm_kernel(y_ref, x_ref, g_ref, gate_ref, *rest, emit_h):
    x_new = x_ref[...] + gate_ref[0] * (_rms(y_ref[...]) * g_ref[...])
    if emit_h:
        g2_ref, sc_ref, sh_ref, o_ref, h_ref = rest
        o_ref[...] = x_new
        h_ref[...] = ((_rms(x_new) * g2_ref[...]) * (1.0 + sc_ref[0]) + sh_ref[0]).astype(h_ref.dtype)
    else:
        (o_ref,) = rest
        o_ref[...] = x_new


def _residual_norm(y, x2, gain, mod3, gate_idx, seq, nxt=None):
    m, d = x2.shape
    tm = _tile(seq, 256)
    per_b = seq // tm
    row = pl.BlockSpec((tm, d), lambda i: (i, 0))
    vec = pl.BlockSpec((1, d), lambda i: (0, 0))
    modv = lambda idx: pl.BlockSpec((1, 1, d), lambda i: ((i // per_b) * 6 + idx, 0, 0))
    in_specs = [row, row, vec, modv(gate_idx)]
    args = [y, x2, gain.reshape(1, d), mod3]
    out_specs, out_shape = [row], [jax.ShapeDtypeStruct((m, d), F32)]
    if nxt is not None:
        gain2, mod3n, sc_idx, sh_idx = nxt
        in_specs += [vec, modv(sc_idx), modv(sh_idx)]
        args += [gain2.reshape(1, d), mod3n, mod3n]
        out_specs.append(row)
        out_shape.append(jax.ShapeDtypeStruct((m, d), BF16))
    out = pl.pallas_call(
        functools.partial(_res_norm_kernel, emit_h=nxt is not None),
        grid=(m // tm,),
        in_specs=in_specs,
        out_specs=out_specs,
        out_shape=out_shape,
        compiler_params=_params("parallel"),
        name="residual_norm",
    )(*args)
    return (out[0], out[1]) if nxt is not None else (out[0], None)


def _glu_kernel(a_ref, wg_ref, wu_ref, o_ref):
    a = a_ref[...]
    g = _dot(a, wg_ref[...])
    u = _dot(a, wu_ref[...])
    o_ref[...] = (g * _sigmoid(g) * u).astype(o_ref.dtype)


def _glu(a, wg, wu, layer):
    m, k = a.shape
    n = wg.shape[2]
    tm, tn = _tile(m, 1024), _tile(n, 512)
    return pl.pallas_call(
        _glu_kernel,
        grid=(m // tm, n // tn),
        in_specs=[pl.BlockSpec((tm, k), lambda i, j: (i, 0)),
                  pl.BlockSpec((None, k, tn), lambda i, j: (layer, 0, j)),
                  pl.BlockSpec((None, k, tn), lambda i, j: (layer, 0, j))],
        out_specs=pl.BlockSpec((tm, tn), lambda i, j: (i, j)),
        out_shape=jax.ShapeDtypeStruct((m, n), BF16),
        compiler_params=_params("parallel", "parallel"),
        name="swiglu_up",
    )(a, wg, wu)


def kernel(x, c, w_ada, b_ada, norm_gains, w_in, dn_conv, dn_a_log, dn_dt_bias, dn_norm_w, fox_f_bias,
           w_branch_dn, w_branch_fox, w_out, w_gate, w_up, w_down):
    batch, seq, d = x.shape
    depth = w_ada.shape[0]
    hv = dn_a_log.shape[1]
    hf = fox_f_bias.shape[1]
    v_dim = hv * HEAD_DIM
    conv_dim = dn_conv.shape[1]
    qk_dim = (conv_dim - v_dim) // 2
    hqk = qk_dim // HEAD_DIM
    fox_dim = hf * HEAD_DIM
    assert 2 * hv + hf <= GATE_LANES

    o_z = conv_dim
    o_b = o_z + v_dim
    o_a = o_b + hv
    o_fq = o_a + hv
    o_ff = o_fq + 3 * fox_dim
    o_mg = o_ff + hf
    fox_base = (conv_dim + v_dim) // HEAD_DIM
    merge_base = conv_dim + v_dim + 3 * fox_dim

    mod = _modulation(c, w_ada, b_ada)
    x2 = x.reshape(batch * seq, d)
    pad = GATE_LANES - (2 * hv + hf)
    zpad = jnp.zeros((pad,), F32)

    main_cols = ((0, o_b), (o_fq, 3 * fox_dim), (o_mg, w_in.shape[2] - o_mg))
    w_dn_all, w_fox_all, w_out_all = (w.astype(BF16) for w in (w_branch_dn, w_branch_fox, w_out))
    w_gate_all, w_up_all, w_down_all = (w.astype(BF16) for w in (w_gate, w_up, w_down))
    mod3s = [mod[l].reshape(batch * 6, 1, d) for l in range(depth)]

    h = _norm_mod(x2, norm_gains[0, 0], mod3s[0], 1, 0, seq)
    for l in range(depth):
        mod3 = mod3s[l]
        bias = jnp.concatenate([jnp.zeros((hv,), F32), dn_dt_bias[l], fox_f_bias[l], zpad])
        mult = jnp.concatenate([jnp.ones((hv,), F32), -jnp.exp(dn_a_log[l]), jnp.ones((hf,), F32), zpad])
        pcol = jnp.stack([bias, mult], axis=0)

        proj = _in_proj(h, w_in, l, main_cols)
        cols, rows = _gates(h, w_in, l, o_b, o_ff, pcol, batch, seq, hv, hf)
        o_dn = _deltanet(proj, dn_conv[l].T, cols, rows, dn_norm_w[l], batch, seq, hqk, hv)
        f_rows = rows[:, 2 * hv:2 * hv + hf, :].reshape(batch, hf, 1, seq)
        o_fox = _fox(proj, f_rows, batch, seq, fox_base, hf)
        ymix = _merge(o_dn, w_dn_all, o_fox, w_fox_all, l, proj, merge_base)
        y = _out_proj(ymix, w_out_all, l)
        x2, h = _residual_norm(y, x2, norm_gains[l, 1], mod3, 2, seq,
                               nxt=(norm_gains[l, 2], mod3, 4, 3))

        gu = _glu(h, w_gate_all, w_up_all, l)
        y = _out_proj(gu, w_down_all, l)
        nxt = (norm_gains[l + 1, 0], mod3s[l + 1], 1, 0) if l + 1 < depth else None
        x2, h = _residual_norm(y, x2, norm_gains[l, 3], mod3, 5, seq, nxt=nxt)

    return x2.reshape(batch, seq, d)
```

```python
import functools

import jax
import jax.numpy as jnp
from jax import lax
from jax.experimental import pallas as pl
from jax.experimental.pallas import tpu as pltpu

EPS = 1e-6
HEAD_DIM = 128
CONV_K = 4
DN_CHUNK = 128
GATE_LANES = 128
VMEM_LIMIT_BYTES = 48 * 1024 * 1024

F32 = jnp.float32
BF16 = jnp.bfloat16


def _params(*sem):
    return pltpu.CompilerParams(dimension_semantics=sem, vmem_limit_bytes=VMEM_LIMIT_BYTES)


def _tile(dim, pref):
    if dim <= pref:
        return dim
    t = pref - pref % HEAD_DIM
    while dim % t:
        t -= HEAD_DIM
    assert t > 0, (dim, pref)
    return t


def _dot(a, b):
    return jnp.dot(a.astype(BF16), b.astype(BF16), preferred_element_type=F32)


def _dot_nt(a, b):
    return lax.dot_general(a.astype(BF16), b.astype(BF16), (((1,), (1,)), ((), ())),
                           preferred_element_type=F32)


def _dot_tn(a, b):
    return lax.dot_general(a.astype(BF16), b.astype(BF16), (((0,), (0,)), ((), ())),
                           preferred_element_type=F32)


def _sigmoid(x):
    return 0.5 * jnp.tanh(0.5 * x) + 0.5


def _rms(x):
    return x * lax.rsqrt(jnp.mean(x * x, axis=-1, keepdims=True) + EPS)


def _mod_kernel(c_ref, w_ref, b_ref, o_ref):
    c = c_ref[...]
    cond = c * _sigmoid(c)
    o_ref[0] = _dot(cond, w_ref[0]) + b_ref[0]


def _modulation(c, w_ada, b_ada):
    depth, d, n = w_ada.shape
    b = c.shape[0]
    rows = 16
    c_pad = jnp.zeros((rows, d), F32).at[:b].set(c)
    tn = _tile(n, 1024)
    out = pl.pallas_call(
        _mod_kernel,
        grid=(depth, n // tn),
        in_specs=[pl.BlockSpec((rows, d), lambda l, j: (0, 0)),
                  pl.BlockSpec((1, d, tn), lambda l, j: (l, 0, j)),
                  pl.BlockSpec((1, 1, tn), lambda l, j: (l, 0, j))],
        out_specs=pl.BlockSpec((1, rows, tn), lambda l, j: (l, 0, j)),
        out_shape=jax.ShapeDtypeStruct((depth, rows, n), F32),
        compiler_params=_params("parallel", "parallel"),
        name="adaln_mod",
    )(c_pad, w_ada, b_ada.reshape(depth, 1, n))
    return out[:, :b].reshape(depth, b, 6, d)


def _norm_mod_kernel(x_ref, g_ref, sc_ref, sh_ref, o_ref):
    y = _rms(x_ref[...]) * g_ref[...]
    o_ref[...] = (y * (1.0 + sc_ref[0]) + sh_ref[0]).astype(o_ref.dtype)


def _norm_mod(x2, gain, mod3, sc_idx, sh_idx, seq):
    m, d = x2.shape
    tm = _tile(seq, 512)
    per_b = seq // tm
    return pl.pallas_call(
        _norm_mod_kernel,
        grid=(m // tm,),
        in_specs=[pl.BlockSpec((tm, d), lambda i: (i, 0)),
                  pl.BlockSpec((1, d), lambda i: (0, 0)),
                  pl.BlockSpec((1, 1, d), lambda i: ((i // per_b) * 6 + sc_idx, 0, 0)),
                  pl.BlockSpec((1, 1, d), lambda i: ((i // per_b) * 6 + sh_idx, 0, 0))],
        out_specs=pl.BlockSpec((tm, d), lambda i: (i, 0)),
        out_shape=jax.ShapeDtypeStruct((m, d), BF16),
        compiler_params=_params("parallel"),
        name="norm_mod",
    )(x2, gain.reshape(1, d), mod3, mod3)


W_ROWS = 256


def _in_proj_kernel(a_ref, wa_ref, wb_ref, o_ref, w_ref, *, segments):
    j = pl.program_id(0)
    tn = wa_ref.shape[0]

    @pl.when(pl.program_id(1) == 0)
    def _():
        for lo, hi, shift in segments:
            @pl.when((j >= lo) & (j < hi))
            def _():
                for r0 in range(0, tn - shift, W_ROWS):
                    r1 = min(r0 + W_ROWS, tn - shift)
                    w_ref[r0:r1, :] = wa_ref[r0 + shift:r1 + shift, :].astype(BF16)
                if shift:
                    w_ref[tn - shift:, :] = wb_ref[:shift, :].astype(BF16)

    o_ref[...] = _dot_nt(a_ref[...], w_ref[...]).astype(o_ref.dtype)


def _in_proj(a, w_in_t, layer, seg_cols):
    m, k = a.shape
    n_out = sum(w for _, w in seg_cols)
    tn = 1024
    while any(w % tn for _, w in seg_cols):
        tn //= 2
    assert tn >= HEAD_DIM
    tm = _tile(m, 1024)
    segments, out0 = [], 0
    for src0, width in seg_cols:
        shift = src0 - out0
        assert 0 <= shift < HEAD_DIM
        segments.append((out0 // tn, (out0 + width) // tn, shift))
        out0 += width
    kern = functools.partial(_in_proj_kernel, segments=tuple(segments))
    nb = tn // HEAD_DIM
    last_blk = (w_in_t.shape[1] - 1) // HEAD_DIM
    return pl.pallas_call(
        kern,
        grid=(n_out // tn, m // tm),
        in_specs=[pl.BlockSpec((tm, k), lambda j, i: (i, 0)),
                  pl.BlockSpec((None, tn, k), lambda j, i: (layer, j, 0)),
                  pl.BlockSpec((None, HEAD_DIM, k),
                               lambda j, i: (layer, jnp.minimum((j + 1) * nb, last_blk), 0))],
        out_specs=pl.BlockSpec((tm, tn), lambda j, i: (i, j)),
        out_shape=jax.ShapeDtypeStruct((m, n_out), BF16),
        scratch_shapes=[pltpu.VMEM((tn, k), BF16)],
        compiler_params=_params("parallel", "arbitrary"),
        name="in_proj",
    )(a, w_in_t, w_in_t)


def _split3(x):
    hi = x.astype(BF16)
    r1 = x - hi.astype(F32)
    mid = r1.astype(BF16)
    lo = (r1 - mid.astype(F32)).astype(BF16)
    return hi, mid, lo


def _gate_act(x, bias, mult, idx, hv, hf):
    xb = x + bias
    e = jnp.exp(-jnp.abs(xb))
    l1p = jnp.log(1.0 + e)
    sig = jnp.where(xb >= 0, 1.0, e) / (1.0 + e)
    softplus = jnp.maximum(xb, 0.0) + l1p
    logsig = jnp.minimum(xb, 0.0) - l1p
    return jnp.where(idx < hv, sig,
                     jnp.where(idx < 2 * hv, mult * softplus,
                               jnp.where(idx < 2 * hv + hf, logsig, 0.0)))


def _gates_kernel(h_ref, wa_ref, wb_ref, pcol_ref, cols_ref, rows_ref, carry_ref, *, hv, hf, chunk):
    t = pl.program_id(1)
    tm = h_ref.shape[0]

    @pl.when(t == 0)
    def _():
        carry_ref[...] = jnp.zeros_like(carry_ref)

    row_w = lax.broadcasted_iota(jnp.int32, wa_ref.shape, 0)
    w = jnp.where(row_w < 2 * hv, wa_ref[...], jnp.where(row_w < 2 * hv + hf, wb_ref[...], 0.0))
    g_cols = _dot_nt(h_ref[...], w)

    lane = lax.broadcasted_iota(jnp.int32, (tm, GATE_LANES), 1)
    val_c = _gate_act(g_cols, pcol_ref[0:1, :], pcol_ref[1:2, :], lane, hv, hf)
    sub = lax.broadcasted_iota(jnp.int32, (GATE_LANES, tm), 0)
    val_r = val_c.T

    r = lax.broadcasted_iota(jnp.int32, (tm, tm), 0)
    c = lax.broadcasted_iota(jnp.int32, (tm, tm), 1)
    sh = chunk.bit_length() - 1
    same = (r >> sh) == (c >> sh)
    tril_blk = jnp.where((r >= c) & same, 1.0, 0.0).astype(BF16)
    triu_blk = jnp.where((r <= c) & same, 1.0, 0.0).astype(BF16)
    triu_all = jnp.where(r <= c, 1.0, 0.0).astype(BF16)

    cum_c = sum(jnp.dot(tril_blk, p, preferred_element_type=F32) for p in _split3(val_c))
    pieces_r = _split3(val_r)
    cum_r_blk = sum(jnp.dot(p, triu_blk, preferred_element_type=F32) for p in pieces_r)
    cum_r_all = sum(jnp.dot(p, triu_all, preferred_element_type=F32) for p in pieces_r)
    cum_r_all = cum_r_all + carry_ref[:, 0:1]

    is_decay_c = (lane >= hv) & (lane < 2 * hv)
    cols_ref[0] = jnp.where(is_decay_c, cum_c, val_c)
    is_decay_r = (sub >= hv) & (sub < 2 * hv)
    is_forget_r = (sub >= 2 * hv) & (sub < 2 * hv + hf)
    rows = jnp.where(is_decay_r, cum_r_blk, jnp.where(is_forget_r, cum_r_all, val_r))
    rows_ref[0] = rows
    carry_ref[...] = jnp.broadcast_to(cum_r_all[:, tm - 1:tm], carry_ref.shape)


def _gates(h2, w_in, layer, col_ba, col_f, pcol, batch, seq, hv, hf):
    m, d = h2.shape
    tm = _tile(seq, 512)
    per_b = seq // tm
    assert col_ba % GATE_LANES == 0 and col_f % GATE_LANES == 2 * hv
    blk_ba, blk_f = col_ba // GATE_LANES, col_f // GATE_LANES
    kern = functools.partial(_gates_kernel, hv=hv, hf=hf, chunk=DN_CHUNK)
    return pl.pallas_call(
        kern,
        grid=(batch, per_b),
        in_specs=[pl.BlockSpec((tm, d), lambda b, t: (b * per_b + t, 0)),
                  pl.BlockSpec((None, GATE_LANES, d), lambda b, t: (layer, blk_ba, 0)),
                  pl.BlockSpec((None, GATE_LANES, d), lambda b, t: (layer, blk_f, 0)),
                  pl.BlockSpec((2, GATE_LANES), lambda b, t: (0, 0))],
        out_specs=[pl.BlockSpec((1, tm, GATE_LANES), lambda b, t: (b, t, 0)),
                   pl.BlockSpec((1, GATE_LANES, tm), lambda b, t: (b, 0, t))],
        out_shape=[jax.ShapeDtypeStruct((batch, seq, GATE_LANES), F32),
                   jax.ShapeDtypeStruct((batch, GATE_LANES, seq), F32)],
        scratch_shapes=[pltpu.VMEM((GATE_LANES, GATE_LANES), F32)],
        compiler_params=_params("parallel", "arbitrary"),
        name="gates",
    )(h2, w_in, w_in, pcol)


def _causal_conv(u, tail, w):
    out = u * w[CONV_K - 1:CONV_K, :]
    row8 = lax.broadcasted_iota(jnp.int32, tail.shape, 0)
    for s in range(1, CONV_K):
        rolled = pltpu.roll(u, s, 0)
        head = jnp.where(row8 < s, pltpu.roll(tail, s, 0), rolled[0:8])
        shifted = jnp.concatenate([head, rolled[8:]], axis=0)
        out = out + shifted * w[CONV_K - 1 - s:CONV_K - s, :]
    return out


INV_LEVELS = (DN_CHUNK // 8).bit_length() - 1
MASK_DIAG8, MASK_EYE, MASK_STRICT = 0, INV_LEVELS + 1, INV_LEVELS + 2
N_MASKS = INV_LEVELS + 3


def _inverse_masks(n):
    r = lax.broadcasted_iota(jnp.int32, (n, n), 0)
    c = lax.broadcasted_iota(jnp.int32, (n, n), 1)
    masks = [(r >> 3) == (c >> 3)]
    for sh in range(3, 3 + INV_LEVELS):
        masks.append(((r >> (sh + 1)) == (c >> (sh + 1))) & ((r >> sh) == (c >> sh) + 1))
    masks += [r == c, r > c]
    return [jnp.where(m, 1.0, 0.0) for m in masks]


def _unit_lower_inverses(lms, mask_ref):
    n0s = [-(lm * mask_ref[MASK_DIAG8]) for lm in lms]
    n2s = [_dot(n0, n0) for n0 in n0s]
    n4s = [_dot(n2, n2) for n2 in n2s]
    ps = [mask_ref[MASK_EYE] + n0 for n0 in n0s]
    ps = [p + _dot(p, n2) for p, n2 in zip(ps, n2s)]
    ps = [p + _dot(p, n4) for p, n4 in zip(ps, n4s)]
    for level in range(1, INV_LEVELS + 1):
        xs = [_dot(lm * mask_ref[level], p) for lm, p in zip(lms, ps)]
        ps = [p - _dot(p, x) for p, x in zip(ps, xs)]
    return ps


def _deltanet_kernel(q_ref, k_ref, v_ref, z_ref, wq_ref, wk_ref, wv_ref, cols_ref, rows_ref, nw_ref,
                     o_ref, s_ref, qt_ref, kt_ref, vt_ref, mask_ref, u_ref, lhs1_ref, lhs2_ref, egl_ref,
                     *, hv, rep, groups, unroll, scan_unroll):
    i = pl.program_id(1)
    tb = pl.program_id(2)
    rows_blk = q_ref.shape[0]
    n_chunks = rows_blk // DN_CHUNK
    c_ = DN_CHUNK
    d = HEAD_DIM
    nh = groups * rep

    @pl.when(tb == 0)
    def _():
        s_ref[...] = jnp.zeros_like(s_ref)
        qt_ref[...] = jnp.zeros_like(qt_ref)
        kt_ref[...] = jnp.zeros_like(kt_ref)
        vt_ref[...] = jnp.zeros_like(vt_ref)
        for j, m in enumerate(_inverse_masks(c_)):
            mask_ref[j] = m

    r = lax.broadcasted_iota(jnp.int32, (c_, c_), 0)
    c = lax.broadcasted_iota(jnp.int32, (c_, c_), 1)
    lane = lax.broadcasted_iota(jnp.int32, (c_, GATE_LANES), 1)
    nw = nw_ref[...]

    def tail_of(ref, tail_ref, ci, off):
        prev = pl.multiple_of(jnp.maximum(off - 16, 0), 16)
        inside = ref[pl.ds(prev, 16), :].astype(F32)[8:16]
        return jnp.where(ci == 0, tail_ref[...], inside)

    def prepare_body(j, carry):
        chains = []
        for uu in range(unroll):
            ci = j * unroll + uu
            off = pl.multiple_of(ci * c_, c_)
            sl = pl.ds(off, c_)
            q_raw = q_ref[sl, :].astype(F32)
            k_raw = k_ref[sl, :].astype(F32)
            v_raw = v_ref[sl, :].astype(F32)
            qc = _causal_conv(q_raw, tail_of(q_ref, qt_ref, ci, off), wq_ref[...])
            kc = _causal_conv(k_raw, tail_of(k_ref, kt_ref, ci, off), wk_ref[...])
            vc = _causal_conv(v_raw, tail_of(v_ref, vt_ref, ci, off), wv_ref[...])
            qc = qc * _sigmoid(qc)
            kc = kc * _sigmoid(kc)
            vc = vc * _sigmoid(vc)
            cols = cols_ref[0, sl, :]
            row0 = pl.multiple_of(((hv + i * nh) // 8) * 8, 8)
            rows8 = rows_ref[0, pl.ds(row0, 8), sl]
            sub8 = lax.broadcasted_iota(jnp.int32, (8, c_), 0)
            for g in range(groups):
                qg_ = qc[:, g * d:(g + 1) * d]
                kg_ = kc[:, g * d:(g + 1) * d]
                qn = qg_ * lax.rsqrt(jnp.sum(qg_ * qg_, axis=-1, keepdims=True) + EPS) * (d ** -0.5)
                kn = kg_ * lax.rsqrt(jnp.sum(kg_ * kg_, axis=-1, keepdims=True) + EPS)
                kk = _dot_nt(kn, kn)
                qk = _dot_nt(qn, kn)
                for hh in range(rep):
                    hl = g * rep + hh
                    head = i * nh + hl
                    beta_c = jnp.sum(jnp.where(lane == head, cols, 0.0), axis=1, keepdims=True)
                    gc_c = jnp.sum(jnp.where(lane == hv + head, cols, 0.0), axis=1, keepdims=True)
                    gc_r = jnp.sum(jnp.where(sub8 == (hv + head) % 8, rows8, 0.0), axis=0, keepdims=True)
                    g_last = gc_r[:, c_ - 1:c_]
                    decay = jnp.exp(jnp.where(r >= c, gc_c - gc_r, -1e30))
                    lm = (beta_c * kk) * (decay * mask_ref[MASK_STRICT])
                    eg = jnp.exp(gc_c)
                    v_h = vc[:, hl * d:(hl + 1) * d]
                    rhs = jnp.concatenate([v_h * beta_c, kn * (beta_c * eg)], axis=1).astype(BF16)
                    kd = kn * jnp.exp(g_last - gc_c)
                    lhs2_ref[hl, ci] = jnp.concatenate([qk * decay, kd.T], axis=0).astype(BF16)
                    egl_ref[hl, ci] = jnp.broadcast_to(jnp.exp(g_last), (8, d))
                    chains.append((hl, ci, sl, lm, rhs, (qn * eg).astype(BF16)))
        tinvs = _unit_lower_inverses([ch[3] for ch in chains], mask_ref)
        uws = [_dot(tinv, ch[4]) for tinv, ch in zip(tinvs, chains)]
        for uw, (hl, ci, sl, _, _, qg) in zip(uws, chains):
            u_ref[hl, sl, :] = uw[:, :d]
            lhs1_ref[hl, ci] = jnp.concatenate([uw[:, d:].astype(BF16), qg], axis=0)
        return carry

    lax.fori_loop(0, n_chunks // unroll, prepare_body, 0)

    heads = range(nh)

    def scan_chunk(ci):
        off = pl.multiple_of(ci * c_, c_)
        sl = pl.ds(off, c_)
        states = [s_ref[hl] for hl in heads]
        ws_qs = [_dot(lhs1_ref[hl, ci], states[hl]) for hl in heads]
        v_new = [u_ref[hl, sl, :] - ws_qs[hl][:c_] for hl in heads]
        av_kv = [_dot(lhs2_ref[hl, ci], v_new[hl]) for hl in heads]
        for hl in heads:
            s_ref[hl] = states[hl] * egl_ref[hl, ci, 0:1, :] + av_kv[hl][c_:]
        for hl in heads:
            o = ws_qs[hl][c_:] + av_kv[hl][:c_]
            z = z_ref[sl, hl * d:(hl + 1) * d].astype(F32)
            out = _rms(o) * nw * (z * _sigmoid(z))
            o_ref[sl, hl * d:(hl + 1) * d] = out.astype(o_ref.dtype)

    def scan_body(j, carry):
        for uu in range(scan_unroll):
            scan_chunk(j * scan_unroll + uu)
        return carry

    lax.fori_loop(0, n_chunks // scan_unroll, scan_body, 0)

    qt_ref[...] = q_ref[rows_blk - 16:, :].astype(F32)[8:16]
    kt_ref[...] = k_ref[rows_blk - 16:, :].astype(F32)[8:16]
    vt_ref[...] = v_ref[rows_blk - 16:, :].astype(F32)[8:16]


def _deltanet(proj, conv_wt, cols, rows, norm_w, batch, seq, hqk, hv):
    rep = hv // hqk
    assert hv == rep * hqk and DN_CHUNK == HEAD_DIM
    groups = 2 if hqk % 2 == 0 else 1
    nh = groups * rep
    assert 8 % nh == 0 and hv % nh == 0 and (2 * hqk) % nh == 0
    tb = _tile(seq, 1024)
    per_b = seq // tb
    wqk, wv = groups * HEAD_DIM, nh * HEAD_DIM
    nqk_blk, nv_blk = hqk // groups, hv // nh
    k_blk0 = nqk_blk
    v_blk0 = (2 * hqk) // nh
    z_blk0 = v_blk0 + nv_blk
    row = lambda b, i, t: b * per_b + t
    n_chunks = tb // DN_CHUNK
    unroll = next(u for u in (4, 2, 1) if n_chunks % u == 0 and u * nh <= 8)
    scan_unroll = 2 if n_chunks % 2 == 0 else 1
    kern = functools.partial(_deltanet_kernel, hv=hv, rep=rep, groups=groups, unroll=unroll,
                             scan_unroll=scan_unroll)
    return pl.pallas_call(
        kern,
        grid=(batch, nqk_blk, per_b),
        in_specs=[pl.BlockSpec((tb, wqk), lambda b, i, t: (row(b, i, t), i)),
                  pl.BlockSpec((tb, wqk), lambda b, i, t: (row(b, i, t), k_blk0 + i)),
                  pl.BlockSpec((tb, wv), lambda b, i, t: (row(b, i, t), v_blk0 + i)),
                  pl.BlockSpec((tb, wv), lambda b, i, t: (row(b, i, t), z_blk0 + i)),
                  pl.BlockSpec((CONV_K, wqk), lambda b, i, t: (0, i)),
                  pl.BlockSpec((CONV_K, wqk), lambda b, i, t: (0, k_blk0 + i)),
                  pl.BlockSpec((CONV_K, wv), lambda b, i, t: (0, v_blk0 + i)),
                  pl.BlockSpec((1, tb, GATE_LANES), lambda b, i, t: (b, t, 0)),
                  pl.BlockSpec((1, GATE_LANES, tb), lambda b, i, t: (b, 0, t)),
                  pl.BlockSpec((1, HEAD_DIM), lambda b, i, t: (0, 0))],
        out_specs=pl.BlockSpec((tb, wv), lambda b, i, t: (row(b, i, t), i)),
        out_shape=jax.ShapeDtypeStruct((batch * seq, hv * HEAD_DIM), BF16),
        scratch_shapes=[pltpu.VMEM((nh, HEAD_DIM, HEAD_DIM), F32),
                        pltpu.VMEM((8, wqk), F32),
                        pltpu.VMEM((8, wqk), F32),
                        pltpu.VMEM((8, wv), F32),
                        pltpu.VMEM((N_MASKS, DN_CHUNK, DN_CHUNK), F32),
                        pltpu.VMEM((nh, tb, HEAD_DIM), F32),
                        pltpu.VMEM((nh, n_chunks, 2 * DN_CHUNK, HEAD_DIM), BF16),
                        pltpu.VMEM((nh, n_chunks, 2 * DN_CHUNK, DN_CHUNK), BF16),
                        pltpu.VMEM((nh, n_chunks, 8, HEAD_DIM), F32)],
        compiler_params=_params("parallel", "parallel", "arbitrary"),
        name="deltanet",
    )(proj, proj, proj, proj, conv_wt, conv_wt, conv_wt, cols, rows, norm_w.reshape(1, HEAD_DIM))


LOG2E = 1.4426950408889634


MXU_COLS = 256


def _reduce_rows(x, op):
    rows, lanes = x.shape
    slabs = 8
    if rows % (8 * slabs) == 0:
        x3 = x.reshape(slabs, rows // slabs, lanes)
        x = x3[0]
        for j in range(1, slabs):
            x = op(x, x3[j])
    final = jnp.max if op is jnp.maximum else jnp.sum
    return final(x, axis=0, keepdims=True)


def _fox_kernel(q_ref, k_ref, v_ref, f_ref, o_ref, vt_ref, fcol_ref, s_ref, m_ref, l_ref, acc_ref,
                *, scale):
    qi = pl.program_id(2)
    tq = q_ref.shape[0]
    tk = tq
    seq = k_ref.shape[0]
    qt = min(MXU_COLS, tq)
    d = HEAD_DIM

    @pl.when(qi == 0)
    def _():
        def prep(bi, carry):
            off = pl.multiple_of(bi * d, d)
            vt_ref[:, pl.ds(off, d)] = v_ref[pl.ds(off, d), :].astype(F32).T.astype(BF16)
            frow = f_ref[0, 0, :, pl.ds(off, d)] * LOG2E
            fcol_ref[pl.ds(off, d), :] = jnp.broadcast_to(frow, (d, d)).T
            return carry
        lax.fori_loop(0, seq // d, prep, 0)

    m_ref[...] = jnp.full_like(m_ref, -1e30)
    l_ref[...] = jnp.zeros_like(l_ref)
    acc_ref[...] = jnp.zeros_like(acc_ref)

    def scores(kj, slot):
        off = pl.multiple_of(kj * tk, tk)
        s_ref[slot] = _dot_nt(k_ref[pl.ds(off, tk), :], q_ref[...])

    def consume(kj, slot, masked):
        off = pl.multiple_of(kj * tk, tk)
        fcol = fcol_ref[pl.ds(off, tk), :]
        fcol = jnp.concatenate([fcol] * (qt // d), axis=1)
        vt = vt_ref[:, pl.ds(off, tk)]
        for t in range(tq // qt):
            lanes = slice(t * qt, (t + 1) * qt)
            s = s_ref[slot, :, lanes] * (scale * LOG2E) - fcol
            if masked:
                key = lax.broadcasted_iota(jnp.int32, (tk, qt), 0)
                qry = lax.broadcasted_iota(jnp.int32, (tk, qt), 1) + t * qt
                s = jnp.where(key <= qry, s, -1e30)
            m_prev = m_ref[:, lanes]
            m_new = jnp.maximum(m_prev, _reduce_rows(s, jnp.maximum))
            p = jnp.exp2(s - m_new)
            alpha = jnp.exp2(m_prev - m_new)
            l_ref[:, lanes] = alpha * l_ref[:, lanes] + _reduce_rows(p, jnp.add)
            m_ref[:, lanes] = m_new
            acc_ref[:, lanes] = alpha * acc_ref[:, lanes] + _dot(vt, p)

    scores(0, 0)

    def body(i, carry):
        kj = 2 * i
        scores(kj + 1, 1)
        consume(kj, 0, False)
        scores(kj + 2, 0)
        consume(kj + 1, 1, False)
        return carry

    lax.fori_loop(0, qi // 2, body, 0)

    @pl.when(qi % 2 == 0)
    def _():
        consume(qi, 0, True)

    @pl.when(qi % 2 == 1)
    def _():
        scores(qi, 1)
        consume(qi - 1, 0, False)
        consume(qi, 1, True)

    o_ref[...] = (acc_ref[...] / l_ref[...]).T.astype(o_ref.dtype)


def _fox(proj, f_rows, batch, seq, base, hf):
    tq = _tile(seq, 512)
    nq = seq // tq
    kern = functools.partial(_fox_kernel, scale=HEAD_DIM ** -0.5)
    return pl.pallas_call(
        kern,
        grid=(batch, hf, nq),
        in_specs=[pl.BlockSpec((tq, HEAD_DIM), lambda b, h, qi: (b * nq + qi, base + h)),
                  pl.BlockSpec((seq, HEAD_DIM), lambda b, h, qi: (b, base + hf + h)),
                  pl.BlockSpec((seq, HEAD_DIM), lambda b, h, qi: (b, base + 2 * hf + h)),
                  pl.BlockSpec((1, 1, 1, seq), lambda b, h, qi: (b, h, 0, 0))],
        out_specs=pl.BlockSpec((tq, HEAD_DIM), lambda b, h, qi: (b * nq + qi, h)),
        out_shape=jax.ShapeDtypeStruct((batch * seq, hf * HEAD_DIM), BF16),
        scratch_shapes=[pltpu.VMEM((HEAD_DIM, seq), BF16),
                        pltpu.VMEM((seq, HEAD_DIM), F32),
                        pltpu.VMEM((2, tq, tq), F32),
                        pltpu.VMEM((1, tq), F32),
                        pltpu.VMEM((1, tq), F32),
                        pltpu.VMEM((HEAD_DIM, tq), F32)],
        compiler_params=_params("parallel", "parallel", "arbitrary"),
        name="fox_attention",
    )(proj, proj, proj, f_rows)


def _merge_kernel(a1_ref, w1_ref, a2_ref, w2_ref, m1_ref, m2_ref, o_ref):
    y1 = _dot(a1_ref[...], w1_ref[...])
    y2 = _dot(a2_ref[...], w2_ref[...])
    g1 = _sigmoid(m1_ref[...].astype(F32))
    g2 = _sigmoid(m2_ref[...].astype(F32))
    o_ref[...] = (g1 * y1 + g2 * y2).astype(o_ref.dtype)


def _merge(o_dn, w_dn, o_fox, w_fox, layer, proj, merge_base_cols):
    m, k1 = o_dn.shape
    k2 = o_fox.shape[1]
    d = w_dn.shape[2]
    tm, tn = _tile(m, 512), _tile(d, 512)
    assert merge_base_cols % tn == 0
    mb = merge_base_cols // tn
    return pl.pallas_call(
        _merge_kernel,
        grid=(m // tm, d // tn),
        in_specs=[pl.BlockSpec((tm, k1), lambda i, j: (i, 0)),
                  pl.BlockSpec((None, k1, tn), lambda i, j: (layer, 0, j)),
                  pl.BlockSpec((tm, k2), lambda i, j: (i, 0)),
                  pl.BlockSpec((None, k2, tn), lambda i, j: (layer, 0, j)),
                  pl.BlockSpec((tm, tn), lambda i, j: (i, mb + j)),
                  pl.BlockSpec((tm, tn), lambda i, j: (i, mb + d // tn + j))],
        out_specs=pl.BlockSpec((tm, tn), lambda i, j: (i, j)),
        out_shape=jax.ShapeDtypeStruct((m, d), BF16),
        compiler_params=_params("parallel", "parallel"),
        name="branch_merge",
    )(o_dn, w_dn, o_fox, w_fox, proj, proj)


def _mm_f32_kernel(a_ref, w_ref, o_ref):
    o_ref[...] = _dot(a_ref[...], w_ref[...])


def _out_proj(a, w, layer):
    m, k = a.shape
    d = w.shape[2]
    pref = 512 if k > 4096 else 1024
    tm, tn = _tile(m, pref), _tile(d, pref)
    return pl.pallas_call(
        _mm_f32_kernel,
        grid=(m // tm, d // tn),
        in_specs=[pl.BlockSpec((tm, k), lambda i, j: (i, 0)),
                  pl.BlockSpec((None, k, tn), lambda i, j: (layer, 0, j))],
        out_specs=pl.BlockSpec((tm, tn), lambda i, j: (i, j)),
        out_shape=jax.ShapeDtypeStruct((m, d), F32),
        compiler_params=_params("parallel", "parallel"),
        name="out_proj",
    )(a, w)


def _res_norm_kernel(y_ref, x_ref, g_ref, gate_ref, *rest, emit_h):
    x_new = x_ref[...] + gate_ref[0] * (_rms(y_ref[...]) * g_ref[...])
    if emit_h:
        g2_ref, sc_ref, sh_ref, o_ref, h_ref = rest
        o_ref[...] = x_new
        h_ref[...] = ((_rms(x_new) * g2_ref[...]) * (1.0 + sc_ref[0]) + sh_ref[0]).astype(h_ref.dtype)
    else:
        (o_ref,) = rest
        o_ref[...] = x_new


def _residual_norm(y, x2, gain, mod3, gate_idx, seq, nxt=None):
    m, d = x2.shape
    tm = _tile(seq, 256)
    per_b = seq // tm
    row = pl.BlockSpec((tm, d), lambda i: (i, 0))
    vec = pl.BlockSpec((1, d), lambda i: (0, 0))
    modv = lambda idx: pl.BlockSpec((1, 1, d), lambda i: ((i // per_b) * 6 + idx, 0, 0))
    in_specs = [row, row, vec, modv(gate_idx)]
    args = [y, x2, gain.reshape(1, d), mod3]
    out_specs, out_shape = [row], [jax.ShapeDtypeStruct((m, d), F32)]
    if nxt is not None:
        gain2, mod3n, sc_idx, sh_idx = nxt
        in_specs += [vec, modv(sc_idx), modv(sh_idx)]
        args += [gain2.reshape(1, d), mod3n, mod3n]
        out_specs.append(row)
        out_shape.append(jax.ShapeDtypeStruct((m, d), BF16))
    out = pl.pallas_call(
        functools.partial(_res_norm_kernel, emit_h=nxt is not None),
        grid=(m // tm,),
        in_specs=in_specs,
        out_specs=out_specs,
        out_shape=out_shape,
        compiler_params=_params("parallel"),
        name="residual_norm",
    )(*args)
    return (out[0], out[1]) if nxt is not None else (out[0], None)


def _glu_kernel(a_ref, wg_ref, wu_ref, o_ref):
    a = a_ref[...]
    g = _dot(a, wg_ref[...])
    u = _dot(a, wu_ref[...])
    o_ref[...] = (g * _sigmoid(g) * u).astype(o_ref.dtype)


def _glu(a, wg, wu, layer):
    m, k = a.shape
    n = wg.shape[2]
    tm, tn = _tile(m, 1024), _tile(n, 512)
    return pl.pallas_call(
        _glu_kernel,
        grid=(m // tm, n // tn),
        in_specs=[pl.BlockSpec((tm, k), lambda i, j: (i, 0)),
                  pl.BlockSpec((None, k, tn), lambda i, j: (layer, 0, j)),
                  pl.BlockSpec((None, k, tn), lambda i, j: (layer, 0, j))],
        out_specs=pl.BlockSpec((tm, tn), lambda i, j: (i, j)),
        out_shape=jax.ShapeDtypeStruct((m, n), BF16),
        compiler_params=_params("parallel", "parallel"),
        name="swiglu_up",
    )(a, wg, wu)


def kernel(x, c, w_ada, b_ada, norm_gains, w_in, dn_conv, dn_a_log, dn_dt_bias, dn_norm_w, fox_f_bias,
           w_branch_dn, w_branch_fox, w_out, w_gate, w_up, w_down):
    batch, seq, d = x.shape
    depth = w_ada.shape[0]
    hv = dn_a_log.shape[1]
    hf = fox_f_bias.shape[1]
    v_dim = hv * HEAD_DIM
    conv_dim = dn_conv.shape[1]
    qk_dim = (conv_dim - v_dim) // 2
    hqk = qk_dim // HEAD_DIM
    fox_dim = hf * HEAD_DIM
    assert 2 * hv + hf <= GATE_LANES

    o_z = conv_dim
    o_b = o_z + v_dim
    o_a = o_b + hv
    o_fq = o_a + hv
    o_ff = o_fq + 3 * fox_dim
    o_mg = o_ff + hf
    fox_base = (conv_dim + v_dim) // HEAD_DIM
    merge_base = conv_dim + v_dim + 3 * fox_dim

    mod = _modulation(c, w_ada, b_ada)
    x2 = x.reshape(batch * seq, d)
    pad = GATE_LANES - (2 * hv + hf)
    zpad = jnp.zeros((pad,), F32)

    main_cols = ((0, o_b), (o_fq, 3 * fox_dim), (o_mg, w_in.shape[2] - o_mg))
    w_in_t = jnp.swapaxes(w_in, 1, 2)
    w_dn_all, w_fox_all, w_out_all = (w.astype(BF16) for w in (w_branch_dn, w_branch_fox, w_out))
    w_gate_all, w_up_all, w_down_all = (w.astype(BF16) for w in (w_gate, w_up, w_down))
    mod3s = [mod[l].reshape(batch * 6, 1, d) for l in range(depth)]

    h = _norm_mod(x2, norm_gains[0, 0], mod3s[0], 1, 0, seq)
    for l in range(depth):
        mod3 = mod3s[l]
        bias = jnp.concatenate([jnp.zeros((hv,), F32), dn_dt_bias[l], fox_f_bias[l], zpad])
        mult = jnp.concatenate([jnp.ones((hv,), F32), -jnp.exp(dn_a_log[l]), jnp.ones((hf,), F32), zpad])
        pcol = jnp.stack([bias, mult], axis=0)

        proj = _in_proj(h, w_in_t, l, main_cols)
        cols, rows = _gates(h, w_in_t, l, o_b, o_ff, pcol, batch, seq, hv, hf)
        o_dn = _deltanet(proj, dn_conv[l].T, cols, rows, dn_norm_w[l], batch, seq, hqk, hv)
        f_rows = rows[:, 2 * hv:2 * hv + hf, :].reshape(batch, hf, 1, seq)
        o_fox = _fox(proj, f_rows, batch, seq, fox_base, hf)
        ymix = _merge(o_dn, w_dn_all, o_fox, w_fox_all, l, proj, merge_base)
        y = _out_proj(ymix, w_out_all, l)
        x2, h = _residual_norm(y, x2, norm_gains[l, 1], mod3, 2, seq,
                               nxt=(norm_gains[l, 2], mod3, 4, 3))

        gu = _glu(h, w_gate_all, w_up_all, l)
        y = _out_proj(gu, w_down_all, l)
        nxt = (norm_gains[l + 1, 0], mod3s[l + 1], 1, 0) if l + 1 < depth else None
        x2, h = _residual_norm(y, x2, norm_gains[l, 3], mod3, 5, seq, nxt=nxt)

    return x2.reshape(batch, seq, d)
```

```python
import functools

import jax
import jax.numpy as jnp
from jax import lax
from jax.experimental import pallas as pl
from jax.experimental.pallas import tpu as pltpu

EPS = 1e-6
HEAD_DIM = 128
CONV_K = 4
DN_CHUNK = 128
GATE_LANES = 128
VMEM_LIMIT_BYTES = 48 * 1024 * 1024

F32 = jnp.float32
BF16 = jnp.bfloat16


def _params(*sem):
    return pltpu.CompilerParams(dimension_semantics=sem, vmem_limit_bytes=VMEM_LIMIT_BYTES)


def _tile(dim, pref):
    if dim <= pref:
        return dim
    t = pref - pref % HEAD_DIM
    while dim % t:
        t -= HEAD_DIM
    assert t > 0, (dim, pref)
    return t


def _dot(a, b):
    return jnp.dot(a.astype(BF16), b.astype(BF16), preferred_element_type=F32)


def _dot_nt(a, b):
    return lax.dot_general(a.astype(BF16), b.astype(BF16), (((1,), (1,)), ((), ())),
                           preferred_element_type=F32)


def _dot_tn(a, b):
    return lax.dot_general(a.astype(BF16), b.astype(BF16), (((0,), (0,)), ((), ())),
                           preferred_element_type=F32)


def _sigmoid(x):
    return 0.5 * jnp.tanh(0.5 * x) + 0.5


def _rms(x):
    return x * lax.rsqrt(jnp.mean(x * x, axis=-1, keepdims=True) + EPS)


def _mod_kernel(c_ref, w_ref, b_ref, o_ref):
    c = c_ref[...]
    cond = c * _sigmoid(c)
    o_ref[0] = _dot(cond, w_ref[0]) + b_ref[0]


def _modulation(c, w_ada, b_ada):
    depth, d, n = w_ada.shape
    b = c.shape[0]
    rows = 16
    c_pad = jnp.zeros((rows, d), F32).at[:b].set(c)
    tn = _tile(n, 1024)
    out = pl.pallas_call(
        _mod_kernel,
        grid=(depth, n // tn),
        in_specs=[pl.BlockSpec((rows, d), lambda l, j: (0, 0)),
                  pl.BlockSpec((1, d, tn), lambda l, j: (l, 0, j)),
                  pl.BlockSpec((1, 1, tn), lambda l, j: (l, 0, j))],
        out_specs=pl.BlockSpec((1, rows, tn), lambda l, j: (l, 0, j)),
        out_shape=jax.ShapeDtypeStruct((depth, rows, n), F32),
        compiler_params=_params("parallel", "parallel"),
        name="adaln_mod",
    )(c_pad, w_ada, b_ada.reshape(depth, 1, n))
    return out[:, :b].reshape(depth, b, 6, d)


def _norm_mod_kernel(x_ref, g_ref, sc_ref, sh_ref, o_ref):
    y = _rms(x_ref[...]) * g_ref[...]
    o_ref[...] = (y * (1.0 + sc_ref[0]) + sh_ref[0]).astype(o_ref.dtype)


def _norm_mod(x2, gain, mod3, sc_idx, sh_idx, seq):
    m, d = x2.shape
    tm = _tile(seq, 512)
    per_b = seq // tm
    return pl.pallas_call(
        _norm_mod_kernel,
        grid=(m // tm,),
        in_specs=[pl.BlockSpec((tm, d), lambda i: (i, 0)),
                  pl.BlockSpec((1, d), lambda i: (0, 0)),
                  pl.BlockSpec((1, 1, d), lambda i: ((i // per_b) * 6 + sc_idx, 0, 0)),
                  pl.BlockSpec((1, 1, d), lambda i: ((i // per_b) * 6 + sh_idx, 0, 0))],
        out_specs=pl.BlockSpec((tm, d), lambda i: (i, 0)),
        out_shape=jax.ShapeDtypeStruct((m, d), BF16),
        compiler_params=_params("parallel"),
        name="norm_mod",
    )(x2, gain.reshape(1, d), mod3, mod3)


W_ROWS = 256


def _in_proj_kernel(a_ref, wa_ref, wb_ref, o_ref, w_ref, *, segments):
    j = pl.program_id(0)
    tn = wa_ref.shape[0]

    @pl.when(pl.program_id(1) == 0)
    def _():
        for lo, hi, shift in segments:
            @pl.when((j >= lo) & (j < hi))
            def _():
                for r0 in range(0, tn - shift, W_ROWS):
                    r1 = min(r0 + W_ROWS, tn - shift)
                    w_ref[r0:r1, :] = wa_ref[r0 + shift:r1 + shift, :].astype(BF16)
                if shift:
                    w_ref[tn - shift:, :] = wb_ref[:shift, :].astype(BF16)

    o_ref[...] = _dot_nt(a_ref[...], w_ref[...]).astype(o_ref.dtype)


def _in_proj(a, w_in_t, layer, seg_cols):
    m, k = a.shape
    n_out = sum(w for _, w in seg_cols)
    tn = 1024
    while any(w % tn for _, w in seg_cols):
        tn //= 2
    assert tn >= HEAD_DIM
    tm = _tile(m, 1024)
    segments, out0 = [], 0
    for src0, width in seg_cols:
        shift = src0 - out0
        assert 0 <= shift < HEAD_DIM
        segments.append((out0 // tn, (out0 + width) // tn, shift))
        out0 += width
    kern = functools.partial(_in_proj_kernel, segments=tuple(segments))
    nb = tn // HEAD_DIM
    last_blk = (w_in_t.shape[1] - 1) // HEAD_DIM
    return pl.pallas_call(
        kern,
        grid=(n_out // tn, m // tm),
        in_specs=[pl.BlockSpec((tm, k), lambda j, i: (i, 0)),
                  pl.BlockSpec((None, tn, k), lambda j, i: (layer, j, 0)),
                  pl.BlockSpec((None, HEAD_DIM, k),
                               lambda j, i: (layer, jnp.minimum((j + 1) * nb, last_blk), 0))],
        out_specs=pl.BlockSpec((tm, tn), lambda j, i: (i, j)),
        out_shape=jax.ShapeDtypeStruct((m, n_out), BF16),
        scratch_shapes=[pltpu.VMEM((tn, k), BF16)],
        compiler_params=_params("parallel", "arbitrary"),
        name="in_proj",
    )(a, w_in_t, w_in_t)


def _split3(x):
    hi = x.astype(BF16)
    r1 = x - hi.astype(F32)
    mid = r1.astype(BF16)
    lo = (r1 - mid.astype(F32)).astype(BF16)
    return hi, mid, lo


def _gate_act(x, bias, mult, idx, hv, hf):
    xb = x + bias
    e = jnp.exp(-jnp.abs(xb))
    l1p = jnp.log(1.0 + e)
    sig = jnp.where(xb >= 0, 1.0, e) / (1.0 + e)
    softplus = jnp.maximum(xb, 0.0) + l1p
    logsig = jnp.minimum(xb, 0.0) - l1p
    return jnp.where(idx < hv, sig,
                     jnp.where(idx < 2 * hv, mult * softplus,
                               jnp.where(idx < 2 * hv + hf, logsig, 0.0)))


def _gates_kernel(h_ref, wa_ref, wb_ref, pcol_ref, cols_ref, rows_ref, carry_ref, *, hv, hf, chunk):
    t = pl.program_id(1)
    tm = h_ref.shape[0]

    @pl.when(t == 0)
    def _():
        carry_ref[...] = jnp.zeros_like(carry_ref)

    row_w = lax.broadcasted_iota(jnp.int32, wa_ref.shape, 0)
    w = jnp.where(row_w < 2 * hv, wa_ref[...], jnp.where(row_w < 2 * hv + hf, wb_ref[...], 0.0))
    g_cols = _dot_nt(h_ref[...], w)

    lane = lax.broadcasted_iota(jnp.int32, (tm, GATE_LANES), 1)
    val_c = _gate_act(g_cols, pcol_ref[0:1, :], pcol_ref[1:2, :], lane, hv, hf)
    sub = lax.broadcasted_iota(jnp.int32, (GATE_LANES, tm), 0)
    val_r = val_c.T

    r = lax.broadcasted_iota(jnp.int32, (tm, tm), 0)
    c = lax.broadcasted_iota(jnp.int32, (tm, tm), 1)
    sh = chunk.bit_length() - 1
    same = (r >> sh) == (c >> sh)
    tril_blk = jnp.where((r >= c) & same, 1.0, 0.0).astype(BF16)
    triu_blk = jnp.where((r <= c) & same, 1.0, 0.0).astype(BF16)
    triu_all = jnp.where(r <= c, 1.0, 0.0).astype(BF16)

    cum_c = sum(jnp.dot(tril_blk, p, preferred_element_type=F32) for p in _split3(val_c))
    pieces_r = _split3(val_r)
    cum_r_blk = sum(jnp.dot(p, triu_blk, preferred_element_type=F32) for p in pieces_r)
    cum_r_all = sum(jnp.dot(p, triu_all, preferred_element_type=F32) for p in pieces_r)
    cum_r_all = cum_r_all + carry_ref[:, 0:1]

    is_decay_c = (lane >= hv) & (lane < 2 * hv)
    cols_ref[0] = jnp.where(is_decay_c, cum_c, val_c)
    is_decay_r = (sub >= hv) & (sub < 2 * hv)
    is_forget_r = (sub >= 2 * hv) & (sub < 2 * hv + hf)
    rows = jnp.where(is_decay_r, cum_r_blk, jnp.where(is_forget_r, cum_r_all, val_r))
    rows_ref[0] = rows
    carry_ref[...] = jnp.broadcast_to(cum_r_all[:, tm - 1:tm], carry_ref.shape)


def _gates(h2, w_in, layer, col_ba, col_f, pcol, batch, seq, hv, hf):
    m, d = h2.shape
    tm = _tile(seq, 512)
    per_b = seq // tm
    assert col_ba % GATE_LANES == 0 and col_f % GATE_LANES == 2 * hv
    blk_ba, blk_f = col_ba // GATE_LANES, col_f // GATE_LANES
    kern = functools.partial(_gates_kernel, hv=hv, hf=hf, chunk=DN_CHUNK)
    return pl.pallas_call(
        kern,
        grid=(batch, per_b),
        in_specs=[pl.BlockSpec((tm, d), lambda b, t: (b * per_b + t, 0)),
                  pl.BlockSpec((None, GATE_LANES, d), lambda b, t: (layer, blk_ba, 0)),
                  pl.BlockSpec((None, GATE_LANES, d), lambda b, t: (layer, blk_f, 0)),
                  pl.BlockSpec((2, GATE_LANES), lambda b, t: (0, 0))],
        out_specs=[pl.BlockSpec((1, tm, GATE_LANES), lambda b, t: (b, t, 0)),
                   pl.BlockSpec((1, GATE_LANES, tm), lambda b, t: (b, 0, t))],
        out_shape=[jax.ShapeDtypeStruct((batch, seq, GATE_LANES), F32),
                   jax.ShapeDtypeStruct((batch, GATE_LANES, seq), F32)],
        scratch_shapes=[pltpu.VMEM((GATE_LANES, GATE_LANES), F32)],
        compiler_params=_params("parallel", "arbitrary"),
        name="gates",
    )(h2, w_in, w_in, pcol)


def _causal_conv(u, tail, w):
    out = u * w[CONV_K - 1:CONV_K, :]
    row8 = lax.broadcasted_iota(jnp.int32, tail.shape, 0)
    for s in range(1, CONV_K):
        rolled = pltpu.roll(u, s, 0)
        head = jnp.where(row8 < s, pltpu.roll(tail, s, 0), rolled[0:8])
        shifted = jnp.concatenate([head, rolled[8:]], axis=0)
        out = out + shifted * w[CONV_K - 1 - s:CONV_K - s, :]
    return out


INV_LEVELS = (DN_CHUNK // 8).bit_length() - 1
MASK_DIAG8, MASK_EYE, MASK_STRICT = 0, INV_LEVELS + 1, INV_LEVELS + 2
N_MASKS = INV_LEVELS + 3


def _inverse_masks(n):
    r = lax.broadcasted_iota(jnp.int32, (n, n), 0)
    c = lax.broadcasted_iota(jnp.int32, (n, n), 1)
    masks = [(r >> 3) == (c >> 3)]
    for sh in range(3, 3 + INV_LEVELS):
        masks.append(((r >> (sh + 1)) == (c >> (sh + 1))) & ((r >> sh) == (c >> sh) + 1))
    masks += [r == c, r > c]
    return [jnp.where(m, 1.0, 0.0) for m in masks]


def _unit_lower_inverses(lms, mask_ref):
    n0s = [-(lm * mask_ref[MASK_DIAG8]) for lm in lms]
    n2s = [_dot(n0, n0) for n0 in n0s]
    n4s = [_dot(n2, n2) for n2 in n2s]
    ps = [mask_ref[MASK_EYE] + n0 for n0 in n0s]
    ps = [p + _dot(p, n2) for p, n2 in zip(ps, n2s)]
    ps = [p + _dot(p, n4) for p, n4 in zip(ps, n4s)]
    for level in range(1, INV_LEVELS + 1):
        xs = [_dot(lm * mask_ref[level], p) for lm, p in zip(lms, ps)]
        ps = [p - _dot(p, x) for p, x in zip(ps, xs)]
    return ps


def _deltanet_kernel(q_ref, k_ref, v_ref, z_ref, wq_ref, wk_ref, wv_ref, cols_ref, rows_ref, nw_ref,
                     o_ref, s_ref, qt_ref, kt_ref, vt_ref, mask_ref, u_ref, lhs1_ref, lhs2_ref, egl_ref,
                     *, hv, rep, groups, unroll, scan_unroll):
    i = pl.program_id(1)
    tb = pl.program_id(2)
    rows_blk = q_ref.shape[0]
    n_chunks = rows_blk // DN_CHUNK
    c_ = DN_CHUNK
    d = HEAD_DIM
    nh = groups * rep

    @pl.when(tb == 0)
    def _():
        s_ref[...] = jnp.zeros_like(s_ref)
        qt_ref[...] = jnp.zeros_like(qt_ref)
        kt_ref[...] = jnp.zeros_like(kt_ref)
        vt_ref[...] = jnp.zeros_like(vt_ref)
        for j, m in enumerate(_inverse_masks(c_)):
            mask_ref[j] = m

    r = lax.broadcasted_iota(jnp.int32, (c_, c_), 0)
    c = lax.broadcasted_iota(jnp.int32, (c_, c_), 1)
    lane = lax.broadcasted_iota(jnp.int32, (c_, GATE_LANES), 1)
    nw = nw_ref[...]

    def tail_of(ref, tail_ref, ci, off):
        prev = pl.multiple_of(jnp.maximum(off - 16, 0), 16)
        inside = ref[pl.ds(prev, 16), :].astype(F32)[8:16]
        return jnp.where(ci == 0, tail_ref[...], inside)

    def prepare_body(j, carry):
        chains = []
        for uu in range(unroll):
            ci = j * unroll + uu
            off = pl.multiple_of(ci * c_, c_)
            sl = pl.ds(off, c_)
            q_raw = q_ref[sl, :].astype(F32)
            k_raw = k_ref[sl, :].astype(F32)
            v_raw = v_ref[sl, :].astype(F32)
            qc = _causal_conv(q_raw, tail_of(q_ref, qt_ref, ci, off), wq_ref[...])
            kc = _causal_conv(k_raw, tail_of(k_ref, kt_ref, ci, off), wk_ref[...])
            vc = _causal_conv(v_raw, tail_of(v_ref, vt_ref, ci, off), wv_ref[...])
            qc = qc * _sigmoid(qc)
            kc = kc * _sigmoid(kc)
            vc = vc * _sigmoid(vc)
            cols = cols_ref[0, sl, :]
            row0 = pl.multiple_of(((hv + i * nh) // 8) * 8, 8)
            rows8 = rows_ref[0, pl.ds(row0, 8), sl]
            sub8 = lax.broadcasted_iota(jnp.int32, (8, c_), 0)
            for g in range(groups):
                qg_ = qc[:, g * d:(g + 1) * d]
                kg_ = kc[:, g * d:(g + 1) * d]
                qn = qg_ * lax.rsqrt(jnp.sum(qg_ * qg_, axis=-1, keepdims=True) + EPS) * (d ** -0.5)
                kn = kg_ * lax.rsqrt(jnp.sum(kg_ * kg_, axis=-1, keepdims=True) + EPS)
                kk = _dot_nt(kn, kn)
                qk = _dot_nt(qn, kn)
                for hh in range(rep):
                    hl = g * rep + hh
                    head = i * nh + hl
                    beta_c = jnp.sum(jnp.where(lane == head, cols, 0.0), axis=1, keepdims=True)
                    gc_c = jnp.sum(jnp.where(lane == hv + head, cols, 0.0), axis=1, keepdims=True)
                    gc_r = jnp.sum(jnp.where(sub8 == (hv + head) % 8, rows8, 0.0), axis=0, keepdims=True)
                    g_last = gc_r[:, c_ - 1:c_]
                    decay = jnp.exp(jnp.where(r >= c, gc_c - gc_r, -1e30))
                    lm = (beta_c * kk) * (decay * mask_ref[MASK_STRICT])
                    eg = jnp.exp(gc_c)
                    v_h = vc[:, hl * d:(hl + 1) * d]
                    rhs = jnp.concatenate([v_h * beta_c, kn * (beta_c * eg)], axis=1).astype(BF16)
                    kd = kn * jnp.exp(g_last - gc_c)
                    lhs2_ref[hl, ci] = jnp.concatenate([qk * decay, kd.T], axis=0).astype(BF16)
                    egl_ref[hl, ci] = jnp.broadcast_to(jnp.exp(g_last), (8, d))
                    chains.append((hl, ci, sl, lm, rhs, (qn * eg).astype(BF16)))
        tinvs = _unit_lower_inverses([ch[3] for ch in chains], mask_ref)
        uws = [_dot(tinv, ch[4]) for tinv, ch in zip(tinvs, chains)]
        for uw, (hl, ci, sl, _, _, qg) in zip(uws, chains):
            u_ref[hl, sl, :] = uw[:, :d]
            lhs1_ref[hl, ci] = jnp.concatenate([uw[:, d:].astype(BF16), qg], axis=0)
        return carry

    lax.fori_loop(0, n_chunks // unroll, prepare_body, 0)

    heads = range(nh)

    def scan_chunk(ci):
        off = pl.multiple_of(ci * c_, c_)
        sl = pl.ds(off, c_)
        states = [s_ref[hl] for hl in heads]
        ws_qs = [_dot(lhs1_ref[hl, ci], states[hl]) for hl in heads]
        v_new = [u_ref[hl, sl, :] - ws_qs[hl][:c_] for hl in heads]
        av_kv = [_dot(lhs2_ref[hl, ci], v_new[hl]) for hl in heads]
        for hl in heads:
            s_ref[hl] = states[hl] * egl_ref[hl, ci, 0:1, :] + av_kv[hl][c_:]
        for hl in heads:
            o = ws_qs[hl][c_:] + av_kv[hl][:c_]
            z = z_ref[sl, hl * d:(hl + 1) * d].astype(F32)
            out = _rms(o) * nw * (z * _sigmoid(z))
            o_ref[sl, hl * d:(hl + 1) * d] = out.astype(o_ref.dtype)

    def scan_body(j, carry):
        for uu in range(scan_unroll):
            scan_chunk(j * scan_unroll + uu)
        return carry

    lax.fori_loop(0, n_chunks // scan_unroll, scan_body, 0)

    qt_ref[...] = q_ref[rows_blk - 16:, :].astype(F32)[8:16]
    kt_ref[...] = k_ref[rows_blk - 16:, :].astype(F32)[8:16]
    vt_ref[...] = v_ref[rows_blk - 16:, :].astype(F32)[8:16]


def _deltanet(proj, conv_wt, cols, rows, norm_w, batch, seq, hqk, hv):
    rep = hv // hqk
    assert hv == rep * hqk and DN_CHUNK == HEAD_DIM
    groups = 2 if hqk % 2 == 0 else 1
    nh = groups * rep
    assert 8 % nh == 0 and hv % nh == 0 and (2 * hqk) % nh == 0
    tb = _tile(seq, 1024)
    per_b = seq // tb
    wqk, wv = groups * HEAD_DIM, nh * HEAD_DIM
    nqk_blk, nv_blk = hqk // groups, hv // nh
    k_blk0 = nqk_blk
    v_blk0 = (2 * hqk) // nh
    z_blk0 = v_blk0 + nv_blk
    row = lambda b, i, t: b * per_b + t
    n_chunks = tb // DN_CHUNK
    unroll = next(u for u in (4, 2, 1) if n_chunks % u == 0 and u * nh <= 16)
    scan_unroll = 2 if n_chunks % 2 == 0 else 1
    kern = functools.partial(_deltanet_kernel, hv=hv, rep=rep, groups=groups, unroll=unroll,
                             scan_unroll=scan_unroll)
    return pl.pallas_call(
        kern,
        grid=(batch, nqk_blk, per_b),
        in_specs=[pl.BlockSpec((tb, wqk), lambda b, i, t: (row(b, i, t), i)),
                  pl.BlockSpec((tb, wqk), lambda b, i, t: (row(b, i, t), k_blk0 + i)),
                  pl.BlockSpec((tb, wv), lambda b, i, t: (row(b, i, t), v_blk0 + i)),
                  pl.BlockSpec((tb, wv), lambda b, i, t: (row(b, i, t), z_blk0 + i)),
                  pl.BlockSpec((CONV_K, wqk), lambda b, i, t: (0, i)),
                  pl.BlockSpec((CONV_K, wqk), lambda b, i, t: (0, k_blk0 + i)),
                  pl.BlockSpec((CONV_K, wv), lambda b, i, t: (0, v_blk0 + i)),
                  pl.BlockSpec((1, tb, GATE_LANES), lambda b, i, t: (b, t, 0)),
                  pl.BlockSpec((1, GATE_LANES, tb), lambda b, i, t: (b, 0, t)),
                  pl.BlockSpec((1, HEAD_DIM), lambda b, i, t: (0, 0))],
        out_specs=pl.BlockSpec((tb, wv), lambda b, i, t: (row(b, i, t), i)),
        out_shape=jax.ShapeDtypeStruct((batch * seq, hv * HEAD_DIM), BF16),
        scratch_shapes=[pltpu.VMEM((nh, HEAD_DIM, HEAD_DIM), F32),
                        pltpu.VMEM((8, wqk), F32),
                        pltpu.VMEM((8, wqk), F32),
                        pltpu.VMEM((8, wv), F32),
                        pltpu.VMEM((N_MASKS, DN_CHUNK, DN_CHUNK), F32),
                        pltpu.VMEM((nh, tb, HEAD_DIM), F32),
                        pltpu.VMEM((nh, n_chunks, 2 * DN_CHUNK, HEAD_DIM), BF16),
                        pltpu.VMEM((nh, n_chunks, 2 * DN_CHUNK, DN_CHUNK), BF16),
                        pltpu.VMEM((nh, n_chunks, 8, HEAD_DIM), F32)],
        compiler_params=_params("parallel", "parallel", "arbitrary"),
        name="deltanet",
    )(proj, proj, proj, proj, conv_wt, conv_wt, conv_wt, cols, rows, norm_w.reshape(1, HEAD_DIM))


LOG2E = 1.4426950408889634


MXU_COLS = 256


def _reduce_rows(x, op):
    rows, lanes = x.shape
    slabs = 8
    if rows % (8 * slabs) == 0:
        x3 = x.reshape(slabs, rows // slabs, lanes)
        x = x3[0]
        for j in range(1, slabs):
            x = op(x, x3[j])
    final = jnp.max if op is jnp.maximum else jnp.sum
    return final(x, axis=0, keepdims=True)


def _fox_kernel(q_ref, k_ref, v_ref, f_ref, o_ref, vt_ref, fcol_ref, s_ref, m_ref, l_ref, acc_ref,
                *, scale):
    qi = pl.program_id(2)
    tq = q_ref.shape[0]
    tk = tq
    seq = k_ref.shape[0]
    qt = min(MXU_COLS, tq)
    d = HEAD_DIM

    @pl.when(qi == 0)
    def _():
        n_blk = seq // d
        group = next(g for g in (4, 2, 1) if n_blk % g == 0)

        def prep(bi, carry):
            for j in range(group):
                off = pl.multiple_of((bi * group + j) * d, d)
                vt_ref[:, pl.ds(off, d)] = v_ref[pl.ds(off, d), :].astype(F32).T.astype(BF16)
                frow = f_ref[0, 0, :, pl.ds(off, d)] * LOG2E
                fcol_ref[pl.ds(off, d), :] = jnp.broadcast_to(frow, (d, d)).T
            return carry
        lax.fori_loop(0, n_blk // group, prep, 0)

    m_ref[...] = jnp.full_like(m_ref, -1e30)
    l_ref[...] = jnp.zeros_like(l_ref)
    acc_ref[...] = jnp.zeros_like(acc_ref)

    def scores(kj, slot):
        off = pl.multiple_of(kj * tk, tk)
        s_ref[slot] = _dot_nt(k_ref[pl.ds(off, tk), :], q_ref[...])

    def consume(kj, slot, masked):
        off = pl.multiple_of(kj * tk, tk)
        fcol = fcol_ref[pl.ds(off, tk), :]
        fcol = jnp.concatenate([fcol] * (qt // d), axis=1)
        vt = vt_ref[:, pl.ds(off, tk)]
        for t in range(tq // qt):
            lanes = slice(t * qt, (t + 1) * qt)
            nk = min((t + 1) * qt, tk) if masked else tk
            s = s_ref[slot, 0:nk, lanes] * (scale * LOG2E) - fcol[0:nk]
            if masked:
                key = lax.broadcasted_iota(jnp.int32, (nk, qt), 0)
                qry = lax.broadcasted_iota(jnp.int32, (nk, qt), 1) + t * qt
                s = jnp.where(key <= qry, s, -1e30)
            m_prev = m_ref[:, lanes]
            m_new = jnp.maximum(m_prev, _reduce_rows(s, jnp.maximum))
            p = jnp.exp2(s - m_new)
            alpha = jnp.exp2(m_prev - m_new)
            l_ref[:, lanes] = alpha * l_ref[:, lanes] + _reduce_rows(p, jnp.add)
            m_ref[:, lanes] = m_new
            acc_ref[:, lanes] = alpha * acc_ref[:, lanes] + _dot(vt[:, 0:nk], p)

    scores(0, 0)

    def body(i, carry):
        kj = 2 * i
        scores(kj + 1, 1)
        consume(kj, 0, False)
        scores(kj + 2, 0)
        consume(kj + 1, 1, False)
        return carry

    lax.fori_loop(0, qi // 2, body, 0)

    @pl.when(qi % 2 == 0)
    def _():
        consume(qi, 0, True)

    @pl.when(qi % 2 == 1)
    def _():
        scores(qi, 1)
        consume(qi - 1, 0, False)
        consume(qi, 1, True)

    o_ref[...] = (acc_ref[...] / l_ref[...]).T.astype(o_ref.dtype)


def _fox(proj, f_rows, batch, seq, base, hf):
    tq = _tile(seq, 512)
    nq = seq // tq
    kern = functools.partial(_fox_kernel, scale=HEAD_DIM ** -0.5)
    return pl.pallas_call(
        kern,
        grid=(batch, hf, nq),
        in_specs=[pl.BlockSpec((tq, HEAD_DIM), lambda b, h, qi: (b * nq + qi, base + h)),
                  pl.BlockSpec((seq, HEAD_DIM), lambda b, h, qi: (b, base + hf + h)),
                  pl.BlockSpec((seq, HEAD_DIM), lambda b, h, qi: (b, base + 2 * hf + h)),
                  pl.BlockSpec((1, 1, 1, seq), lambda b, h, qi: (b, h, 0, 0))],
        out_specs=pl.BlockSpec((tq, HEAD_DIM), lambda b, h, qi: (b * nq + qi, h)),
        out_shape=jax.ShapeDtypeStruct((batch * seq, hf * HEAD_DIM), BF16),
        scratch_shapes=[pltpu.VMEM((HEAD_DIM, seq), BF16),
                        pltpu.VMEM((seq, HEAD_DIM), F32),
                        pltpu.VMEM((2, tq, tq), F32),
                        pltpu.VMEM((1, tq), F32),
                        pltpu.VMEM((1, tq), F32),
                        pltpu.VMEM((HEAD_DIM, tq), F32)],
        compiler_params=_params("parallel", "parallel", "arbitrary"),
        name="fox_attention",
    )(proj, proj, proj, f_rows)


def _merge_kernel(a1_ref, w1_ref, a2_ref, w2_ref, m1_ref, m2_ref, o_ref):
    y1 = _dot(a1_ref[...], w1_ref[...])
    y2 = _dot(a2_ref[...], w2_ref[...])
    g1 = _sigmoid(m1_ref[...].astype(F32))
    g2 = _sigmoid(m2_ref[...].astype(F32))
    o_ref[...] = (g1 * y1 + g2 * y2).astype(o_ref.dtype)


def _merge(o_dn, w_dn, o_fox, w_fox, layer, proj, merge_base_cols):
    m, k1 = o_dn.shape
    k2 = o_fox.shape[1]
    d = w_dn.shape[2]
    tm, tn = _tile(m, 512), _tile(d, 512)
    assert merge_base_cols % tn == 0
    mb = merge_base_cols // tn
    return pl.pallas_call(
        _merge_kernel,
        grid=(m // tm, d // tn),
        in_specs=[pl.BlockSpec((tm, k1), lambda i, j: (i, 0)),
                  pl.BlockSpec((None, k1, tn), lambda i, j: (layer, 0, j)),
                  pl.BlockSpec((tm, k2), lambda i, j: (i, 0)),
                  pl.BlockSpec((None, k2, tn), lambda i, j: (layer, 0, j)),
                  pl.BlockSpec((tm, tn), lambda i, j: (i, mb + j)),
                  pl.BlockSpec((tm, tn), lambda i, j: (i, mb + d // tn + j))],
        out_specs=pl.BlockSpec((tm, tn), lambda i, j: (i, j)),
        out_shape=jax.ShapeDtypeStruct((m, d), BF16),
        compiler_params=_params("parallel", "parallel"),
        name="branch_merge",
    )(o_dn, w_dn, o_fox, w_fox, proj, proj)


def _cast_weight_once(w_ref, wb_ref):
    @pl.when(pl.program_id(1) == 0)
    def _():
        k = w_ref.shape[0]
        for r0 in range(0, k, W_ROWS):
            r1 = min(r0 + W_ROWS, k)
            wb_ref[r0:r1, :] = w_ref[r0:r1, :].astype(BF16)


def _mm_f32_kernel(a_ref, w_ref, o_ref, wb_ref):
    _cast_weight_once(w_ref, wb_ref)
    o_ref[...] = _dot(a_ref[...], wb_ref[...])


def _out_proj(a, w, layer):
    m, k = a.shape
    d = w.shape[2]
    pref = 512 if k > 4096 else 1024
    tm, tn = _tile(m, pref), _tile(d, pref)
    return pl.pallas_call(
        _mm_f32_kernel,
        grid=(d // tn, m // tm),
        in_specs=[pl.BlockSpec((tm, k), lambda j, i: (i, 0)),
                  pl.BlockSpec((None, k, tn), lambda j, i: (layer, 0, j))],
        out_specs=pl.BlockSpec((tm, tn), lambda j, i: (i, j)),
        out_shape=jax.ShapeDtypeStruct((m, d), F32),
        scratch_shapes=[pltpu.VMEM((k, tn), BF16)],
        compiler_params=_params("parallel", "arbitrary"),
        name="out_proj",
    )(a, w)


def _res_norm_kernel(y_ref, x_ref, g_ref, gate_ref, *rest, emit_h):
    x_new = x_ref[...] + gate_ref[0] * (_rms(y_ref[...]) * g_ref[...])
    if emit_h:
        g2_ref, sc_ref, sh_ref, o_ref, h_ref = rest
        o_ref[...] = x_new
        h_ref[...] = ((_rms(x_new) * g2_ref[...]) * (1.0 + sc_ref[0]) + sh_ref[0]).astype(h_ref.dtype)
    else:
        (o_ref,) = rest
        o_ref[...] = x_new


def _residual_norm(y, x2, gain, mod3, gate_idx, seq, nxt=None):
    m, d = x2.shape
    tm = _tile(seq, 256)
    per_b = seq // tm
    row = pl.BlockSpec((tm, d), lambda i: (i, 0))
    vec = pl.BlockSpec((1, d), lambda i: (0, 0))
    modv = lambda idx: pl.BlockSpec((1, 1, d), lambda i: ((i // per_b) * 6 + idx, 0, 0))
    in_specs = [row, row, vec, modv(gate_idx)]
    args = [y, x2, gain.reshape(1, d), mod3]
    out_specs, out_shape = [row], [jax.ShapeDtypeStruct((m, d), F32)]
    if nxt is not None:
        gain2, mod3n, sc_idx, sh_idx = nxt
        in_specs += [vec, modv(sc_idx), modv(sh_idx)]
        args += [gain2.reshape(1, d), mod3n, mod3n]
        out_specs.append(row)
        out_shape.append(jax.ShapeDtypeStruct((m, d), BF16))
    out = pl.pallas_call(
        functools.partial(_res_norm_kernel, emit_h=nxt is not None),
        grid=(m // tm,),
        in_specs=in_specs,
        out_specs=out_specs,
        out_shape=out_shape,
        compiler_params=_params("parallel"),
        name="residual_norm",
    )(*args)
    return (out[0], out[1]) if nxt is not None else (out[0], None)


def _glu_kernel(a_ref, wg_ref, wu_ref, o_ref, wgb_ref, wub_ref):
    _cast_weight_once(wg_ref, wgb_ref)
    _cast_weight_once(wu_ref, wub_ref)
    a = a_ref[...]
    g = _dot(a, wgb_ref[...])
    u = _dot(a, wub_ref[...])
    o_ref[...] = (g * _sigmoid(g) * u).astype(o_ref.dtype)


def _glu(a, wg, wu, layer):
    m, k = a.shape
    n = wg.shape[2]
    tm, tn = _tile(m, 1024), _tile(n, 512)
    return pl.pallas_call(
        _glu_kernel,
        grid=(n // tn, m // tm),
        in_specs=[pl.BlockSpec((tm, k), lambda j, i: (i, 0)),
                  pl.BlockSpec((None, k, tn), lambda j, i: (layer, 0, j)),
                  pl.BlockSpec((None, k, tn), lambda j, i: (layer, 0, j))],
        out_specs=pl.BlockSpec((tm, tn), lambda j, i: (i, j)),
        out_shape=jax.ShapeDtypeStruct((m, n), BF16),
        scratch_shapes=[pltpu.VMEM((k, tn), BF16), pltpu.VMEM((k, tn), BF16)],
        compiler_params=_params("parallel", "arbitrary"),
        name="swiglu_up",
    )(a, wg, wu)


def kernel(x, c, w_ada, b_ada, norm_gains, w_in, dn_conv, dn_a_log, dn_dt_bias, dn_norm_w, fox_f_bias,
           w_branch_dn, w_branch_fox, w_out, w_gate, w_up, w_down):
    batch, seq, d = x.shape
    depth = w_ada.shape[0]
    hv = dn_a_log.shape[1]
    hf = fox_f_bias.shape[1]
    v_dim = hv * HEAD_DIM
    conv_dim = dn_conv.shape[1]
    qk_dim = (conv_dim - v_dim) // 2
    hqk = qk_dim // HEAD_DIM
    fox_dim = hf * HEAD_DIM
    assert 2 * hv + hf <= GATE_LANES

    o_z = conv_dim
    o_b = o_z + v_dim
    o_a = o_b + hv
    o_fq = o_a + hv
    o_ff = o_fq + 3 * fox_dim
    o_mg = o_ff + hf
    fox_base = (conv_dim + v_dim) // HEAD_DIM
    merge_base = conv_dim + v_dim + 3 * fox_dim

    mod = _modulation(c, w_ada, b_ada)
    x2 = x.reshape(batch * seq, d)
    pad = GATE_LANES - (2 * hv + hf)
    zpad = jnp.zeros((pad,), F32)

    main_cols = ((0, o_b), (o_fq, 3 * fox_dim), (o_mg, w_in.shape[2] - o_mg))
    w_in_t = jnp.swapaxes(w_in, 1, 2)
    w_dn_all, w_fox_all = w_branch_dn.astype(BF16), w_branch_fox.astype(BF16)
    mod3s = [mod[l].reshape(batch * 6, 1, d) for l in range(depth)]

    h = _norm_mod(x2, norm_gains[0, 0], mod3s[0], 1, 0, seq)
    for l in range(depth):
        mod3 = mod3s[l]
        bias = jnp.concatenate([jnp.zeros((hv,), F32), dn_dt_bias[l], fox_f_bias[l], zpad])
        mult = jnp.concatenate([jnp.ones((hv,), F32), -jnp.exp(dn_a_log[l]), jnp.ones((hf,), F32), zpad])
        pcol = jnp.stack([bias, mult], axis=0)

        proj = _in_proj(h, w_in_t, l, main_cols)
        cols, rows = _gates(h, w_in_t, l, o_b, o_ff, pcol, batch, seq, hv, hf)
        o_dn = _deltanet(proj, dn_conv[l].T, cols, rows, dn_norm_w[l], batch, seq, hqk, hv)
        f_rows = rows[:, 2 * hv:2 * hv + hf, :].reshape(batch, hf, 1, seq)
        o_fox = _fox(proj, f_rows, batch, seq, fox_base, hf)
        ymix = _merge(o_dn, w_dn_all, o_fox, w_fox_all, l, proj, merge_base)
        y = _out_proj(ymix, w_out, l)
        x2, h = _residual_norm(y, x2, norm_gains[l, 1], mod3, 2, seq,
                               nxt=(norm_gains[l, 2], mod3, 4, 3))

        gu = _glu(h, w_gate, w_up, l)
        y = _out_proj(gu, w_down, l)
        nxt = (norm_gains[l + 1, 0], mod3s[l + 1], 1, 0) if l + 1 < depth else None
        x2, h = _residual_norm(y, x2, norm_gains[l, 3], mod3, 5, seq, nxt=nxt)

    return x2.reshape(batch, seq, d)
```

```python
import functools

import jax
import jax.numpy as jnp
from jax import lax
from jax.experimental import pallas as pl
from jax.experimental.pallas import tpu as pltpu

EPS = 1e-6
HEAD_DIM = 128
CONV_K = 4
DN_CHUNK = 128
GATE_LANES = 128
VMEM_LIMIT_BYTES = 48 * 1024 * 1024

F32 = jnp.float32
BF16 = jnp.bfloat16


def _params(*sem):
    return pltpu.CompilerParams(dimension_semantics=sem, vmem_limit_bytes=VMEM_LIMIT_BYTES)


def _tile(dim, pref):
    if dim <= pref:
        return dim
    t = pref - pref % HEAD_DIM
    while dim % t:
        t -= HEAD_DIM
    assert t > 0, (dim, pref)
    return t


def _dot(a, b):
    return jnp.dot(a.astype(BF16), b.astype(BF16), preferred_element_type=F32)


def _dot_nt(a, b):
    return lax.dot_general(a.astype(BF16), b.astype(BF16), (((1,), (1,)), ((), ())),
                           preferred_element_type=F32)


def _dot_tn(a, b):
    return lax.dot_general(a.astype(BF16), b.astype(BF16), (((0,), (0,)), ((), ())),
                           preferred_element_type=F32)


def _sigmoid(x):
    return 0.5 * jnp.tanh(0.5 * x) + 0.5


def _rms(x):
    return x * lax.rsqrt(jnp.mean(x * x, axis=-1, keepdims=True) + EPS)


def _mod_kernel(c_ref, w_ref, b_ref, o_ref):
    c = c_ref[...]
    cond = c * _sigmoid(c)
    o_ref[0] = _dot(cond, w_ref[0]) + b_ref[0]


def _modulation(c, w_ada, b_ada):
    depth, d, n = w_ada.shape
    b = c.shape[0]
    rows = 16
    c_pad = jnp.zeros((rows, d), F32).at[:b].set(c)
    tn = _tile(n, 1024)
    out = pl.pallas_call(
        _mod_kernel,
        grid=(depth, n // tn),
        in_specs=[pl.BlockSpec((rows, d), lambda l, j: (0, 0)),
                  pl.BlockSpec((1, d, tn), lambda l, j: (l, 0, j)),
                  pl.BlockSpec((1, 1, tn), lambda l, j: (l, 0, j))],
        out_specs=pl.BlockSpec((1, rows, tn), lambda l, j: (l, 0, j)),
        out_shape=jax.ShapeDtypeStruct((depth, rows, n), F32),
        compiler_params=_params("parallel", "parallel"),
        name="adaln_mod",
    )(c_pad, w_ada, b_ada.reshape(depth, 1, n))
    return out[:, :b].reshape(depth, b, 6, d)


def _norm_mod_kernel(x_ref, g_ref, sc_ref, sh_ref, o_ref):
    y = _rms(x_ref[...]) * g_ref[...]
    o_ref[...] = (y * (1.0 + sc_ref[0]) + sh_ref[0]).astype(o_ref.dtype)


def _norm_mod(x2, gain, mod3, sc_idx, sh_idx, seq):
    m, d = x2.shape
    tm = _tile(seq, 512)
    per_b = seq // tm
    return pl.pallas_call(
        _norm_mod_kernel,
        grid=(m // tm,),
        in_specs=[pl.BlockSpec((tm, d), lambda i: (i, 0)),
                  pl.BlockSpec((1, d), lambda i: (0, 0)),
                  pl.BlockSpec((1, 1, d), lambda i: ((i // per_b) * 6 + sc_idx, 0, 0)),
                  pl.BlockSpec((1, 1, d), lambda i: ((i // per_b) * 6 + sh_idx, 0, 0))],
        out_specs=pl.BlockSpec((tm, d), lambda i: (i, 0)),
        out_shape=jax.ShapeDtypeStruct((m, d), BF16),
        compiler_params=_params("parallel"),
        name="norm_mod",
    )(x2, gain.reshape(1, d), mod3, mod3)


W_ROWS = 256


def _in_proj_kernel(a_ref, wa_ref, wb_ref, o_ref, w_ref, *, segments):
    j = pl.program_id(0)
    tn = wa_ref.shape[0]

    @pl.when(pl.program_id(1) == 0)
    def _():
        for lo, hi, shift in segments:
            @pl.when((j >= lo) & (j < hi))
            def _():
                for r0 in range(0, tn - shift, W_ROWS):
                    r1 = min(r0 + W_ROWS, tn - shift)
                    w_ref[r0:r1, :] = wa_ref[r0 + shift:r1 + shift, :].astype(BF16)
                if shift:
                    w_ref[tn - shift:, :] = wb_ref[:shift, :].astype(BF16)

    o_ref[...] = _dot_nt(a_ref[...], w_ref[...]).astype(o_ref.dtype)


def _in_proj(a, w_in_t, layer, seg_cols):
    m, k = a.shape
    n_out = sum(w for _, w in seg_cols)
    tn = 1024
    while any(w % tn for _, w in seg_cols):
        tn //= 2
    assert tn >= HEAD_DIM
    tm = _tile(m, 1024)
    segments, out0 = [], 0
    for src0, width in seg_cols:
        shift = src0 - out0
        assert 0 <= shift < HEAD_DIM
        segments.append((out0 // tn, (out0 + width) // tn, shift))
        out0 += width
    kern = functools.partial(_in_proj_kernel, segments=tuple(segments))
    nb = tn // HEAD_DIM
    last_blk = (w_in_t.shape[1] - 1) // HEAD_DIM
    return pl.pallas_call(
        kern,
        grid=(n_out // tn, m // tm),
        in_specs=[pl.BlockSpec((tm, k), lambda j, i: (i, 0)),
                  pl.BlockSpec((None, tn, k), lambda j, i: (layer, j, 0)),
                  pl.BlockSpec((None, HEAD_DIM, k),
                               lambda j, i: (layer, jnp.minimum((j + 1) * nb, last_blk), 0))],
        out_specs=pl.BlockSpec((tm, tn), lambda j, i: (i, j)),
        out_shape=jax.ShapeDtypeStruct((m, n_out), BF16),
        scratch_shapes=[pltpu.VMEM((tn, k), BF16)],
        compiler_params=_params("parallel", "arbitrary"),
        name="in_proj",
    )(a, w_in_t, w_in_t)


def _split3(x):
    hi = x.astype(BF16)
    r1 = x - hi.astype(F32)
    mid = r1.astype(BF16)
    lo = (r1 - mid.astype(F32)).astype(BF16)
    return hi, mid, lo


def _gate_act(x, bias, mult, idx, hv, hf):
    xb = x + bias
    e = jnp.exp(-jnp.abs(xb))
    l1p = jnp.log(1.0 + e)
    sig = jnp.where(xb >= 0, 1.0, e) / (1.0 + e)
    softplus = jnp.maximum(xb, 0.0) + l1p
    logsig = jnp.minimum(xb, 0.0) - l1p
    return jnp.where(idx < hv, sig,
                     jnp.where(idx < 2 * hv, mult * softplus,
                               jnp.where(idx < 2 * hv + hf, logsig, 0.0)))


def _gates_kernel(h_ref, wa_ref, wb_ref, pcol_ref, rows_ref, carry_ref, *, hv, hf, chunk):
    t = pl.program_id(1)
    tm = h_ref.shape[0]

    @pl.when(t == 0)
    def _():
        carry_ref[...] = jnp.zeros_like(carry_ref)

    row_w = lax.broadcasted_iota(jnp.int32, wa_ref.shape, 0)
    w = jnp.where(row_w < 2 * hv, wa_ref[...], jnp.where(row_w < 2 * hv + hf, wb_ref[...], 0.0))
    g_cols = _dot_nt(h_ref[...], w)

    lane = lax.broadcasted_iota(jnp.int32, (tm, GATE_LANES), 1)
    val_c = _gate_act(g_cols, pcol_ref[0:1, :], pcol_ref[1:2, :], lane, hv, hf)
    sub = lax.broadcasted_iota(jnp.int32, (GATE_LANES, tm), 0)
    val_r = val_c.T

    r = lax.broadcasted_iota(jnp.int32, (tm, tm), 0)
    c = lax.broadcasted_iota(jnp.int32, (tm, tm), 1)
    sh = chunk.bit_length() - 1
    same = (r >> sh) == (c >> sh)
    triu_blk = jnp.where((r <= c) & same, 1.0, 0.0).astype(BF16)
    triu_all = jnp.where(r <= c, 1.0, 0.0).astype(BF16)

    pieces_r = _split3(val_r)
    cum_r_blk = sum(jnp.dot(p, triu_blk, preferred_element_type=F32) for p in pieces_r)
    cum_r_all = sum(jnp.dot(p, triu_all, preferred_element_type=F32) for p in pieces_r)
    cum_r_all = cum_r_all + carry_ref[:, 0:1]

    is_decay_r = (sub >= hv) & (sub < 2 * hv)
    is_forget_r = (sub >= 2 * hv) & (sub < 2 * hv + hf)
    rows = jnp.where(is_decay_r, cum_r_blk, jnp.where(is_forget_r, cum_r_all, val_r))
    rows_ref[0] = rows
    carry_ref[...] = jnp.broadcast_to(cum_r_all[:, tm - 1:tm], carry_ref.shape)


def _gates(h2, w_in, layer, col_ba, col_f, pcol, batch, seq, hv, hf):
    m, d = h2.shape
    tm = _tile(seq, 512)
    per_b = seq // tm
    assert col_ba % GATE_LANES == 0 and col_f % GATE_LANES == 2 * hv
    blk_ba, blk_f = col_ba // GATE_LANES, col_f // GATE_LANES
    kern = functools.partial(_gates_kernel, hv=hv, hf=hf, chunk=DN_CHUNK)
    return pl.pallas_call(
        kern,
        grid=(batch, per_b),
        in_specs=[pl.BlockSpec((tm, d), lambda b, t: (b * per_b + t, 0)),
                  pl.BlockSpec((None, GATE_LANES, d), lambda b, t: (layer, blk_ba, 0)),
                  pl.BlockSpec((None, GATE_LANES, d), lambda b, t: (layer, blk_f, 0)),
                  pl.BlockSpec((2, GATE_LANES), lambda b, t: (0, 0))],
        out_specs=pl.BlockSpec((1, GATE_LANES, tm), lambda b, t: (b, 0, t)),
        out_shape=jax.ShapeDtypeStruct((batch, GATE_LANES, seq), F32),
        scratch_shapes=[pltpu.VMEM((GATE_LANES, GATE_LANES), F32)],
        compiler_params=_params("parallel", "arbitrary"),
        name="gates",
    )(h2, w_in, w_in, pcol)


def _causal_conv(u, tail, w):
    out = u * w[CONV_K - 1:CONV_K, :]
    row8 = lax.broadcasted_iota(jnp.int32, tail.shape, 0)
    for s in range(1, CONV_K):
        rolled = pltpu.roll(u, s, 0)
        head = jnp.where(row8 < s, pltpu.roll(tail, s, 0), rolled[0:8])
        shifted = jnp.concatenate([head, rolled[8:]], axis=0)
        out = out + shifted * w[CONV_K - 1 - s:CONV_K - s, :]
    return out


INV_LEVELS = (DN_CHUNK // 8).bit_length() - 1
MASK_DIAG8, MASK_EYE, MASK_STRICT = 0, INV_LEVELS + 1, INV_LEVELS + 2
N_MASKS = INV_LEVELS + 3


def _inverse_masks(n):
    r = lax.broadcasted_iota(jnp.int32, (n, n), 0)
    c = lax.broadcasted_iota(jnp.int32, (n, n), 1)
    masks = [(r >> 3) == (c >> 3)]
    for sh in range(3, 3 + INV_LEVELS):
        masks.append(((r >> (sh + 1)) == (c >> (sh + 1))) & ((r >> sh) == (c >> sh) + 1))
    masks += [r == c, r > c]
    return [jnp.where(m, 1.0, 0.0) for m in masks]


def _unit_lower_inverses(lms, mask_ref, maskb_ref):
    n0s = [-(lm * mask_ref[MASK_DIAG8]) for lm in lms]
    n2s = [_dot(n0, n0) for n0 in n0s]
    n4s = [_dot(n2, n2) for n2 in n2s]
    ps = [mask_ref[MASK_EYE] + n0 for n0 in n0s]
    ps = [p + _dot(p, n2) for p, n2 in zip(ps, n2s)]
    ps = [p + _dot(p, n4) for p, n4 in zip(ps, n4s)]
    lms_b = [lm.astype(BF16) for lm in lms]
    for level in range(1, INV_LEVELS + 1):
        xs = [_dot(lm_b * maskb_ref[level], p) for lm_b, p in zip(lms_b, ps)]
        ps = [p - _dot(p, x) for p, x in zip(ps, xs)]
    return ps


def _deltanet_kernel(q_ref, k_ref, v_ref, z_ref, wq_ref, wk_ref, wv_ref, rows_ref, nw_ref,
                     o_ref, s_ref, qt_ref, kt_ref, vt_ref, mask_ref, maskb_ref, u_ref, lhs1_ref, lhs2_ref,
                     egl_ref,
                     *, hv, rep, groups, unroll, scan_unroll):
    i = pl.program_id(1)
    tb = pl.program_id(2)
    rows_blk = q_ref.shape[0]
    n_chunks = rows_blk // DN_CHUNK
    c_ = DN_CHUNK
    d = HEAD_DIM
    nh = groups * rep

    @pl.when(tb == 0)
    def _():
        s_ref[...] = jnp.zeros_like(s_ref)
        qt_ref[...] = jnp.zeros_like(qt_ref)
        kt_ref[...] = jnp.zeros_like(kt_ref)
        vt_ref[...] = jnp.zeros_like(vt_ref)
        for j, m in enumerate(_inverse_masks(c_)):
            mask_ref[j] = m
            maskb_ref[j] = m.astype(BF16)

    r = lax.broadcasted_iota(jnp.int32, (c_, c_), 0)
    c = lax.broadcasted_iota(jnp.int32, (c_, c_), 1)
    nw = nw_ref[...]

    def tail_of(ref, tail_ref, ci, off):
        prev = pl.multiple_of(jnp.maximum(off - 16, 0), 16)
        inside = ref[pl.ds(prev, 16), :].astype(F32)[8:16]
        return jnp.where(ci == 0, tail_ref[...], inside)

    def prepare_body(j, carry):
        chains = []
        for uu in range(unroll):
            ci = j * unroll + uu
            off = pl.multiple_of(ci * c_, c_)
            sl = pl.ds(off, c_)
            q_raw = q_ref[sl, :].astype(F32)
            k_raw = k_ref[sl, :].astype(F32)
            v_raw = v_ref[sl, :].astype(F32)
            qc = _causal_conv(q_raw, tail_of(q_ref, qt_ref, ci, off), wq_ref[...])
            kc = _causal_conv(k_raw, tail_of(k_ref, kt_ref, ci, off), wk_ref[...])
            vc = _causal_conv(v_raw, tail_of(v_ref, vt_ref, ci, off), wv_ref[...])
            qc = qc * _sigmoid(qc)
            kc = kc * _sigmoid(kc)
            vc = vc * _sigmoid(vc)
            row_b = pl.multiple_of(((i * nh) // 8) * 8, 8)
            row_g = pl.multiple_of(((hv + i * nh) // 8) * 8, 8)
            rows8_b = rows_ref[0, pl.ds(row_b, 8), sl]
            rows8_g = rows_ref[0, pl.ds(row_g, 8), sl]
            sub8 = lax.broadcasted_iota(jnp.int32, (8, c_), 0)
            for g in range(groups):
                qg_ = qc[:, g * d:(g + 1) * d]
                kg_ = kc[:, g * d:(g + 1) * d]
                qn = qg_ * (lax.rsqrt(jnp.sum(qg_ * qg_, axis=-1, keepdims=True) + EPS) * (d ** -0.5))
                kn = kg_ * lax.rsqrt(jnp.sum(kg_ * kg_, axis=-1, keepdims=True) + EPS)
                kk = _dot_nt(kn, kn)
                qk = _dot_nt(qn, kn)
                for hh in range(rep):
                    hl = g * rep + hh
                    head = i * nh + hl
                    beta_r = jnp.sum(jnp.where(sub8 == head % 8, rows8_b, 0.0), axis=0, keepdims=True)
                    gc_r = jnp.sum(jnp.where(sub8 == (hv + head) % 8, rows8_g, 0.0), axis=0, keepdims=True)
                    beta_c = jnp.broadcast_to(beta_r, (c_, c_)).T
                    gc_c = jnp.broadcast_to(gc_r, (c_, c_)).T
                    g_last = gc_r[:, c_ - 1:c_]
                    decay = jnp.exp(jnp.where(r >= c, gc_c - gc_r, -1e30))
                    lm = (beta_c * kk) * (decay * mask_ref[MASK_STRICT])
                    eg = jnp.exp(gc_c)
                    v_h = vc[:, hl * d:(hl + 1) * d]
                    rhs = jnp.concatenate([v_h * beta_c, kn * (beta_c * eg)], axis=1).astype(BF16)
                    kd = kn * jnp.exp(g_last - gc_c)
                    lhs2_ref[hl, ci] = jnp.concatenate([qk * decay, kd.T], axis=0).astype(BF16)
                    egl_ref[hl, ci] = jnp.broadcast_to(jnp.exp(g_last), (8, d))
                    chains.append((hl, ci, sl, lm, rhs, (qn * eg).astype(BF16)))
        tinvs = _unit_lower_inverses([ch[3] for ch in chains], mask_ref, maskb_ref)
        uws = [_dot(tinv, ch[4]) for tinv, ch in zip(tinvs, chains)]
        for uw, (hl, ci, sl, _, _, qg) in zip(uws, chains):
            u_ref[hl, sl, :] = uw[:, :d]
            lhs1_ref[hl, ci] = jnp.concatenate([uw[:, d:].astype(BF16), qg], axis=0)
        return carry

    lax.fori_loop(0, n_chunks // unroll, prepare_body, 0)

    heads = range(nh)

    def scan_chunk(ci):
        off = pl.multiple_of(ci * c_, c_)
        sl = pl.ds(off, c_)
        states = [s_ref[hl] for hl in heads]
        ws_qs = [_dot(lhs1_ref[hl, ci], states[hl]) for hl in heads]
        v_new = [u_ref[hl, sl, :] - ws_qs[hl][:c_] for hl in heads]
        av_kv = [_dot(lhs2_ref[hl, ci], v_new[hl]) for hl in heads]
        for hl in heads:
            s_ref[hl] = states[hl] * egl_ref[hl, ci, 0:1, :] + av_kv[hl][c_:]
        for hl in heads:
            o = ws_qs[hl][c_:] + av_kv[hl][:c_]
            z = z_ref[sl, hl * d:(hl + 1) * d].astype(F32)
            out = _rms(o) * nw * (z * _sigmoid(z))
            o_ref[sl, hl * d:(hl + 1) * d] = out.astype(o_ref.dtype)

    def scan_body(j, carry):
        for uu in range(scan_unroll):
            scan_chunk(j * scan_unroll + uu)
        return carry

    lax.fori_loop(0, n_chunks // scan_unroll, scan_body, 0)

    qt_ref[...] = q_ref[rows_blk - 16:, :].astype(F32)[8:16]
    kt_ref[...] = k_ref[rows_blk - 16:, :].astype(F32)[8:16]
    vt_ref[...] = v_ref[rows_blk - 16:, :].astype(F32)[8:16]


def _deltanet(proj, conv_wt, rows, norm_w, batch, seq, hqk, hv):
    rep = hv // hqk
    assert hv == rep * hqk and DN_CHUNK == HEAD_DIM
    groups = next(g for g in (4, 2, 1) if hqk % g == 0 and 8 % (g * rep) == 0)
    nh = groups * rep
    assert 8 % nh == 0 and hv % nh == 0 and (2 * hqk) % nh == 0
    tb = _tile(seq, 1024)
    per_b = seq // tb
    wqk, wv = groups * HEAD_DIM, nh * HEAD_DIM
    nqk_blk, nv_blk = hqk // groups, hv // nh
    k_blk0 = nqk_blk
    v_blk0 = (2 * hqk) // nh
    z_blk0 = v_blk0 + nv_blk
    row = lambda b, i, t: b * per_b + t
    n_chunks = tb // DN_CHUNK
    unroll = next(u for u in (4, 2, 1) if n_chunks % u == 0 and u * nh <= 16)
    scan_unroll = 2 if n_chunks % 2 == 0 else 1
    kern = functools.partial(_deltanet_kernel, hv=hv, rep=rep, groups=groups, unroll=unroll,
                             scan_unroll=scan_unroll)
    return pl.pallas_call(
        kern,
        grid=(batch, nqk_blk, per_b),
        in_specs=[pl.BlockSpec((tb, wqk), lambda b, i, t: (row(b, i, t), i)),
                  pl.BlockSpec((tb, wqk), lambda b, i, t: (row(b, i, t), k_blk0 + i)),
                  pl.BlockSpec((tb, wv), lambda b, i, t: (row(b, i, t), v_blk0 + i)),
                  pl.BlockSpec((tb, wv), lambda b, i, t: (row(b, i, t), z_blk0 + i)),
                  pl.BlockSpec((CONV_K, wqk), lambda b, i, t: (0, i)),
                  pl.BlockSpec((CONV_K, wqk), lambda b, i, t: (0, k_blk0 + i)),
                  pl.BlockSpec((CONV_K, wv), lambda b, i, t: (0, v_blk0 + i)),
                  pl.BlockSpec((1, GATE_LANES, tb), lambda b, i, t: (b, 0, t)),
                  pl.BlockSpec((1, HEAD_DIM), lambda b, i, t: (0, 0))],
        out_specs=pl.BlockSpec((tb, wv), lambda b, i, t: (row(b, i, t), i)),
        out_shape=jax.ShapeDtypeStruct((batch * seq, hv * HEAD_DIM), BF16),
        scratch_shapes=[pltpu.VMEM((nh, HEAD_DIM, HEAD_DIM), F32),
                        pltpu.VMEM((8, wqk), F32),
                        pltpu.VMEM((8, wqk), F32),
                        pltpu.VMEM((8, wv), F32),
                        pltpu.VMEM((N_MASKS, DN_CHUNK, DN_CHUNK), F32),
                        pltpu.VMEM((N_MASKS, DN_CHUNK, DN_CHUNK), BF16),
                        pltpu.VMEM((nh, tb, HEAD_DIM), F32),
                        pltpu.VMEM((nh, n_chunks, 2 * DN_CHUNK, HEAD_DIM), BF16),
                        pltpu.VMEM((nh, n_chunks, 2 * DN_CHUNK, DN_CHUNK), BF16),
                        pltpu.VMEM((nh, n_chunks, 8, HEAD_DIM), F32)],
        compiler_params=_params("parallel", "parallel", "arbitrary"),
        name="deltanet",
    )(proj, proj, proj, proj, conv_wt, conv_wt, conv_wt, rows, norm_w.reshape(1, HEAD_DIM))


LOG2E = 1.4426950408889634


MXU_COLS = 256


def _reduce_rows(x, op):
    rows, lanes = x.shape
    slabs = 8
    if rows % (8 * slabs) == 0:
        x3 = x.reshape(slabs, rows // slabs, lanes)
        x = x3[0]
        for j in range(1, slabs):
            x = op(x, x3[j])
    final = jnp.max if op is jnp.maximum else jnp.sum
    return final(x, axis=0, keepdims=True)


ONES_ROWS = 16


def _fox_kernel(q_ref, k_ref, v_ref, f_ref, o_ref, vt_ref, fcol_ref, s_ref, m_ref, acc_ref, *, scale):
    qi = pl.program_id(2)
    tq = q_ref.shape[0]
    tk = tq
    seq = k_ref.shape[0]
    qt = min(MXU_COLS, tq)
    d = HEAD_DIM

    @pl.when(qi == 0)
    def _():
        n_blk = seq // d
        group = next(g for g in (4, 2, 1) if n_blk % g == 0)

        def prep(bi, carry):
            for j in range(group):
                off = pl.multiple_of((bi * group + j) * d, d)
                vt_ref[0:d, pl.ds(off, d)] = v_ref[pl.ds(off, d), :].astype(F32).T.astype(BF16)
                frow = f_ref[0, 0, :, pl.ds(off, d)] * LOG2E
                fcol_ref[pl.ds(off, d), :] = jnp.broadcast_to(frow, (d, d)).T
            return carry
        lax.fori_loop(0, n_blk // group, prep, 0)
        vt_ref[d:, :] = jnp.ones((ONES_ROWS, seq), BF16)

    m_ref[...] = jnp.full_like(m_ref, -1e30)
    acc_ref[...] = jnp.zeros_like(acc_ref)

    def scores(kj, slot):
        off = pl.multiple_of(kj * tk, tk)
        s_ref[slot] = _dot_nt(k_ref[pl.ds(off, tk), :], q_ref[...])

    def consume(kj, slot, masked):
        off = pl.multiple_of(kj * tk, tk)
        fcol = fcol_ref[pl.ds(off, tk), :]
        fcol = jnp.concatenate([fcol] * (qt // d), axis=1)
        vt = vt_ref[:, pl.ds(off, tk)]
        for t in range(tq // qt):
            lanes = slice(t * qt, (t + 1) * qt)
            nk = min((t + 1) * qt, tk) if masked else tk
            s = s_ref[slot, 0:nk, lanes] * (scale * LOG2E) - fcol[0:nk]
            if masked:
                key = lax.broadcasted_iota(jnp.int32, (nk, qt), 0)
                qry = lax.broadcasted_iota(jnp.int32, (nk, qt), 1) + t * qt
                s = jnp.where(key <= qry, s, -1e30)
            m_prev = m_ref[:, lanes]
            m_new = jnp.maximum(m_prev, _reduce_rows(s, jnp.maximum))
            p = jnp.exp2(s - m_new)
            alpha = jnp.exp2(m_prev - m_new)
            m_ref[:, lanes] = m_new
            acc_ref[:, lanes] = alpha * acc_ref[:, lanes] + _dot(vt[:, 0:nk], p)

    scores(0, 0)

    def body(i, carry):
        kj = 2 * i
        scores(kj + 1, 1)
        consume(kj, 0, False)
        scores(kj + 2, 0)
        consume(kj + 1, 1, False)
        return carry

    lax.fori_loop(0, qi // 2, body, 0)

    @pl.when(qi % 2 == 0)
    def _():
        consume(qi, 0, True)

    @pl.when(qi % 2 == 1)
    def _():
        scores(qi, 1)
        consume(qi - 1, 0, False)
        consume(qi, 1, True)

    o_ref[...] = (acc_ref[0:d, :] / acc_ref[d:d + 1, :]).T.astype(o_ref.dtype)


def _fox(proj, f_rows, batch, seq, base, hf):
    tq = _tile(seq, 512)
    nq = seq // tq
    kern = functools.partial(_fox_kernel, scale=HEAD_DIM ** -0.5)
    return pl.pallas_call(
        kern,
        grid=(batch, hf, nq),
        in_specs=[pl.BlockSpec((tq, HEAD_DIM), lambda b, h, qi: (b * nq + qi, base + h)),
                  pl.BlockSpec((seq, HEAD_DIM), lambda b, h, qi: (b, base + hf + h)),
                  pl.BlockSpec((seq, HEAD_DIM), lambda b, h, qi: (b, base + 2 * hf + h)),
                  pl.BlockSpec((1, 1, 1, seq), lambda b, h, qi: (b, h, 0, 0))],
        out_specs=pl.BlockSpec((tq, HEAD_DIM), lambda b, h, qi: (b * nq + qi, h)),
        out_shape=jax.ShapeDtypeStruct((batch * seq, hf * HEAD_DIM), BF16),
        scratch_shapes=[pltpu.VMEM((HEAD_DIM + ONES_ROWS, seq), BF16),
                        pltpu.VMEM((seq, HEAD_DIM), F32),
                        pltpu.VMEM((2, tq, tq), F32),
                        pltpu.VMEM((1, tq), F32),
                        pltpu.VMEM((HEAD_DIM + ONES_ROWS, tq), F32)],
        compiler_params=_params("parallel", "parallel", "arbitrary"),
        name="fox_attention",
    )(proj, proj, proj, f_rows)


def _merge_kernel(a1_ref, w1_ref, a2_ref, w2_ref, m1_ref, m2_ref, o_ref):
    y1 = _dot(a1_ref[...], w1_ref[...])
    y2 = _dot(a2_ref[...], w2_ref[...])
    g1 = _sigmoid(m1_ref[...].astype(F32))
    g2 = _sigmoid(m2_ref[...].astype(F32))
    o_ref[...] = (g1 * y1 + g2 * y2).astype(o_ref.dtype)


def _merge(o_dn, w_dn, o_fox, w_fox, layer, proj, merge_base_cols):
    m, k1 = o_dn.shape
    k2 = o_fox.shape[1]
    d = w_dn.shape[2]
    tm, tn = _tile(m, 512), _tile(d, 512)
    assert merge_base_cols % tn == 0
    mb = merge_base_cols // tn
    return pl.pallas_call(
        _merge_kernel,
        grid=(m // tm, d // tn),
        in_specs=[pl.BlockSpec((tm, k1), lambda i, j: (i, 0)),
                  pl.BlockSpec((None, k1, tn), lambda i, j: (layer, 0, j)),
                  pl.BlockSpec((tm, k2), lambda i, j: (i, 0)),
                  pl.BlockSpec((None, k2, tn), lambda i, j: (layer, 0, j)),
                  pl.BlockSpec((tm, tn), lambda i, j: (i, mb + j)),
                  pl.BlockSpec((tm, tn), lambda i, j: (i, mb + d // tn + j))],
        out_specs=pl.BlockSpec((tm, tn), lambda i, j: (i, j)),
        out_shape=jax.ShapeDtypeStruct((m, d), BF16),
        compiler_params=_params("parallel", "parallel"),
        name="branch_merge",
    )(o_dn, w_dn, o_fox, w_fox, proj, proj)


def _cast_weight_once(w_ref, wb_ref):
    @pl.when(pl.program_id(1) == 0)
    def _():
        k = w_ref.shape[0]
        for r0 in range(0, k, W_ROWS):
            r1 = min(r0 + W_ROWS, k)
            wb_ref[r0:r1, :] = w_ref[r0:r1, :].astype(BF16)


def _mm_f32_kernel(a_ref, w_ref, o_ref, wb_ref):
    _cast_weight_once(w_ref, wb_ref)
    o_ref[...] = _dot(a_ref[...], wb_ref[...])


def _out_proj(a, w, layer):
    m, k = a.shape
    d = w.shape[2]
    pref = 512 if k > 4096 else 1024
    tm, tn = _tile(m, pref), _tile(d, pref)
    return pl.pallas_call(
        _mm_f32_kernel,
        grid=(d // tn, m // tm),
        in_specs=[pl.BlockSpec((tm, k), lambda j, i: (i, 0)),
                  pl.BlockSpec((None, k, tn), lambda j, i: (layer, 0, j))],
        out_specs=pl.BlockSpec((tm, tn), lambda j, i: (i, j)),
        out_shape=jax.ShapeDtypeStruct((m, d), F32),
        scratch_shapes=[pltpu.VMEM((k, tn), BF16)],
        compiler_params=_params("parallel", "arbitrary"),
        name="out_proj",
    )(a, w)


def _res_norm_kernel(y_ref, x_ref, g_ref, gate_ref, *rest, emit_h):
    x_new = x_ref[...] + gate_ref[0] * (_rms(y_ref[...]) * g_ref[...])
    if emit_h:
        g2_ref, sc_ref, sh_ref, o_ref, h_ref = rest
        o_ref[...] = x_new
        h_ref[...] = ((_rms(x_new) * g2_ref[...]) * (1.0 + sc_ref[0]) + sh_ref[0]).astype(h_ref.dtype)
    else:
        (o_ref,) = rest
        o_ref[...] = x_new


def _residual_norm(y, x2, gain, mod3, gate_idx, seq, nxt=None):
    m, d = x2.shape
    tm = _tile(seq, 256)
    per_b = seq // tm
    row = pl.BlockSpec((tm, d), lambda i: (i, 0))
    vec = pl.BlockSpec((1, d), lambda i: (0, 0))
    modv = lambda idx: pl.BlockSpec((1, 1, d), lambda i: ((i // per_b) * 6 + idx, 0, 0))
    in_specs = [row, row, vec, modv(gate_idx)]
    args = [y, x2, gain.reshape(1, d), mod3]
    out_specs, out_shape = [row], [jax.ShapeDtypeStruct((m, d), F32)]
    if nxt is not None:
        gain2, mod3n, sc_idx, sh_idx = nxt
        in_specs += [vec, modv(sc_idx), modv(sh_idx)]
        args += [gain2.reshape(1, d), mod3n, mod3n]
        out_specs.append(row)
        out_shape.append(jax.ShapeDtypeStruct((m, d), BF16))
    out = pl.pallas_call(
        functools.partial(_res_norm_kernel, emit_h=nxt is not None),
        grid=(m // tm,),
        in_specs=in_specs,
        out_specs=out_specs,
        out_shape=out_shape,
        compiler_params=_params("parallel"),
        name="residual_norm",
    )(*args)
    return (out[0], out[1]) if nxt is not None else (out[0], None)


def _glu_kernel(a_ref, wg_ref, wu_ref, o_ref, wgb_ref, wub_ref):
    _cast_weight_once(wg_ref, wgb_ref)
    _cast_weight_once(wu_ref, wub_ref)
    a = a_ref[...]
    g = _dot(a, wgb_ref[...])
    u = _dot(a, wub_ref[...])
    o_ref[...] = (g * _sigmoid(g) * u).astype(o_ref.dtype)


def _glu(a, wg, wu, layer):
    m, k = a.shape
    n = wg.shape[2]
    tm, tn = _tile(m, 1024), _tile(n, 512)
    return pl.pallas_call(
        _glu_kernel,
        grid=(n // tn, m // tm),
        in_specs=[pl.BlockSpec((tm, k), lambda j, i: (i, 0)),
                  pl.BlockSpec((None, k, tn), lambda j, i: (layer, 0, j)),
                  pl.BlockSpec((None, k, tn), lambda j, i: (layer, 0, j))],
        out_specs=pl.BlockSpec((tm, tn), lambda j, i: (i, j)),
        out_shape=jax.ShapeDtypeStruct((m, n), BF16),
        scratch_shapes=[pltpu.VMEM((k, tn), BF16), pltpu.VMEM((k, tn), BF16)],
        compiler_params=_params("parallel", "arbitrary"),
        name="swiglu_up",
    )(a, wg, wu)


def kernel(x, c, w_ada, b_ada, norm_gains, w_in, dn_conv, dn_a_log, dn_dt_bias, dn_norm_w, fox_f_bias,
           w_branch_dn, w_branch_fox, w_out, w_gate, w_up, w_down):
    batch, seq, d = x.shape
    depth = w_ada.shape[0]
    hv = dn_a_log.shape[1]
    hf = fox_f_bias.shape[1]
    v_dim = hv * HEAD_DIM
    conv_dim = dn_conv.shape[1]
    qk_dim = (conv_dim - v_dim) // 2
    hqk = qk_dim // HEAD_DIM
    fox_dim = hf * HEAD_DIM
    assert 2 * hv + hf <= GATE_LANES

    o_z = conv_dim
    o_b = o_z + v_dim
    o_a = o_b + hv
    o_fq = o_a + hv
    o_ff = o_fq + 3 * fox_dim
    o_mg = o_ff + hf
    fox_base = (conv_dim + v_dim) // HEAD_DIM
    merge_base = conv_dim + v_dim + 3 * fox_dim

    mod = _modulation(c, w_ada, b_ada)
    x2 = x.reshape(batch * seq, d)
    pad = GATE_LANES - (2 * hv + hf)
    zpad = jnp.zeros((pad,), F32)

    main_cols = ((0, o_b), (o_fq, 3 * fox_dim), (o_mg, w_in.shape[2] - o_mg))
    w_in_t = jnp.swapaxes(w_in, 1, 2)
    w_dn_all, w_fox_all = w_branch_dn.astype(BF16), w_branch_fox.astype(BF16)
    mod3s = [mod[l].reshape(batch * 6, 1, d) for l in range(depth)]

    h = _norm_mod(x2, norm_gains[0, 0], mod3s[0], 1, 0, seq)
    for l in range(depth):
        mod3 = mod3s[l]
        bias = jnp.concatenate([jnp.zeros((hv,), F32), dn_dt_bias[l], fox_f_bias[l], zpad])
        mult = jnp.concatenate([jnp.ones((hv,), F32), -jnp.exp(dn_a_log[l]), jnp.ones((hf,), F32), zpad])
        pcol = jnp.stack([bias, mult], axis=0)

        proj = _in_proj(h, w_in_t, l, main_cols)
        rows = _gates(h, w_in_t, l, o_b, o_ff, pcol, batch, seq, hv, hf)
        o_dn = _deltanet(proj, dn_conv[l].T, rows, dn_norm_w[l], batch, seq, hqk, hv)
        f_rows = rows[:, 2 * hv:2 * hv + hf, :].reshape(batch, hf, 1, seq)
        o_fox = _fox(proj, f_rows, batch, seq, fox_base, hf)
        ymix = _merge(o_dn, w_dn_all, o_fox, w_fox_all, l, proj, merge_base)
        y = _out_proj(ymix, w_out, l)
        x2, h = _residual_norm(y, x2, norm_gains[l, 1], mod3, 2, seq,
                               nxt=(norm_gains[l, 2], mod3, 4, 3))

        gu = _glu(h, w_gate, w_up, l)
        y = _out_proj(gu, w_down, l)
        nxt = (norm_gains[l + 1, 0], mod3s[l + 1], 1, 0) if l + 1 < depth else None
        x2, h = _residual_norm(y, x2, norm_gains[l, 3], mod3, 5, seq, nxt=nxt)

    return x2.reshape(batch, seq, d)
```

```python
import functools

import jax
import jax.numpy as jnp
from jax import lax
from jax.experimental import pallas as pl
from jax.experimental.pallas import tpu as pltpu

EPS = 1e-6
HEAD_DIM = 128
CONV_K = 4
DN_CHUNK = 128
GATE_LANES = 128
VMEM_LIMIT_BYTES = 48 * 1024 * 1024

F32 = jnp.float32
BF16 = jnp.bfloat16


def _params(*sem):
    return pltpu.CompilerParams(dimension_semantics=sem, vmem_limit_bytes=VMEM_LIMIT_BYTES)


def _tile(dim, pref):
    if dim <= pref:
        return dim
    t = pref - pref % HEAD_DIM
    while dim % t:
        t -= HEAD_DIM
    assert t > 0, (dim, pref)
    return t


def _dot(a, b):
    return jnp.dot(a.astype(BF16), b.astype(BF16), preferred_element_type=F32)


def _dot_nt(a, b):
    return lax.dot_general(a.astype(BF16), b.astype(BF16), (((1,), (1,)), ((), ())),
                           preferred_element_type=F32)


def _dot_tn(a, b):
    return lax.dot_general(a.astype(BF16), b.astype(BF16), (((0,), (0,)), ((), ())),
                           preferred_element_type=F32)


def _sigmoid(x):
    return 0.5 * jnp.tanh(0.5 * x) + 0.5


def _rms(x):
    return x * lax.rsqrt(jnp.mean(x * x, axis=-1, keepdims=True) + EPS)


def _mod_kernel(c_ref, w_ref, b_ref, o_ref):
    c = c_ref[...]
    cond = c * _sigmoid(c)
    o_ref[0] = _dot(cond, w_ref[0]) + b_ref[0]


def _modulation(c, w_ada, b_ada):
    depth, d, n = w_ada.shape
    b = c.shape[0]
    rows = 16
    c_pad = jnp.zeros((rows, d), F32).at[:b].set(c)
    tn = _tile(n, 1024)
    out = pl.pallas_call(
        _mod_kernel,
        grid=(depth, n // tn),
        in_specs=[pl.BlockSpec((rows, d), lambda l, j: (0, 0)),
                  pl.BlockSpec((1, d, tn), lambda l, j: (l, 0, j)),
                  pl.BlockSpec((1, 1, tn), lambda l, j: (l, 0, j))],
        out_specs=pl.BlockSpec((1, rows, tn), lambda l, j: (l, 0, j)),
        out_shape=jax.ShapeDtypeStruct((depth, rows, n), F32),
        compiler_params=_params("parallel", "parallel"),
        name="adaln_mod",
    )(c_pad, w_ada, b_ada.reshape(depth, 1, n))
    return out[:, :b].reshape(depth, b, 6, d)


def _norm_mod_kernel(x_ref, g_ref, sc_ref, sh_ref, o_ref):
    y = _rms(x_ref[...]) * g_ref[...]
    o_ref[...] = (y * (1.0 + sc_ref[0]) + sh_ref[0]).astype(o_ref.dtype)


def _norm_mod(x2, gain, mod3, sc_idx, sh_idx, seq):
    m, d = x2.shape
    tm = _tile(seq, 512)
    per_b = seq // tm
    return pl.pallas_call(
        _norm_mod_kernel,
        grid=(m // tm,),
        in_specs=[pl.BlockSpec((tm, d), lambda i: (i, 0)),
                  pl.BlockSpec((1, d), lambda i: (0, 0)),
                  pl.BlockSpec((1, 1, d), lambda i: ((i // per_b) * 6 + sc_idx, 0, 0)),
                  pl.BlockSpec((1, 1, d), lambda i: ((i // per_b) * 6 + sh_idx, 0, 0))],
        out_specs=pl.BlockSpec((tm, d), lambda i: (i, 0)),
        out_shape=jax.ShapeDtypeStruct((m, d), BF16),
        compiler_params=_params("parallel"),
        name="norm_mod",
    )(x2, gain.reshape(1, d), mod3, mod3)


W_ROWS = 256


def _in_proj_kernel(a_ref, wa_ref, wb_ref, o_ref, w_ref, *, segments):
    j = pl.program_id(0)
    tn = wa_ref.shape[0]

    @pl.when(pl.program_id(1) == 0)
    def _():
        for lo, hi, shift in segments:
            @pl.when((j >= lo) & (j < hi))
            def _():
                for r0 in range(0, tn - shift, W_ROWS):
                    r1 = min(r0 + W_ROWS, tn - shift)
                    w_ref[r0:r1, :] = wa_ref[r0 + shift:r1 + shift, :].astype(BF16)
                if shift:
                    w_ref[tn - shift:, :] = wb_ref[:shift, :].astype(BF16)

    o_ref[...] = _dot_nt(a_ref[...], w_ref[...]).astype(o_ref.dtype)


def _in_proj(a, w_in_t, layer, seg_cols):
    m, k = a.shape
    n_out = sum(w for _, w in seg_cols)
    tn = 1024
    while any(w % tn for _, w in seg_cols):
        tn //= 2
    assert tn >= HEAD_DIM
    tm = _tile(m, 1024)
    segments, out0 = [], 0
    for src0, width in seg_cols:
        shift = src0 - out0
        assert 0 <= shift < HEAD_DIM
        segments.append((out0 // tn, (out0 + width) // tn, shift))
        out0 += width
    kern = functools.partial(_in_proj_kernel, segments=tuple(segments))
    nb = tn // HEAD_DIM
    last_blk = (w_in_t.shape[1] - 1) // HEAD_DIM
    return pl.pallas_call(
        kern,
        grid=(n_out // tn, m // tm),
        in_specs=[pl.BlockSpec((tm, k), lambda j, i: (i, 0)),
                  pl.BlockSpec((None, tn, k), lambda j, i: (layer, j, 0)),
                  pl.BlockSpec((None, HEAD_DIM, k),
                               lambda j, i: (layer, jnp.minimum((j + 1) * nb, last_blk), 0))],
        out_specs=pl.BlockSpec((tm, tn), lambda j, i: (i, j)),
        out_shape=jax.ShapeDtypeStruct((m, n_out), BF16),
        scratch_shapes=[pltpu.VMEM((tn, k), BF16)],
        compiler_params=_params("parallel", "arbitrary"),
        name="in_proj",
    )(a, w_in_t, w_in_t)


def _split3(x):
    hi = x.astype(BF16)
    r1 = x - hi.astype(F32)
    mid = r1.astype(BF16)
    lo = (r1 - mid.astype(F32)).astype(BF16)
    return hi, mid, lo


def _gate_act(x, bias, mult, idx, hv, hf):
    xb = x + bias
    e = jnp.exp(-jnp.abs(xb))
    l1p = jnp.log(1.0 + e)
    sig = jnp.where(xb >= 0, 1.0, e) / (1.0 + e)
    softplus = jnp.maximum(xb, 0.0) + l1p
    logsig = jnp.minimum(xb, 0.0) - l1p
    return jnp.where(idx < hv, sig,
                     jnp.where(idx < 2 * hv, mult * softplus,
                               jnp.where(idx < 2 * hv + hf, logsig, 0.0)))


def _gates_kernel(h_ref, wa_ref, wb_ref, pcol_ref, rows_ref, carry_ref, *, hv, hf, chunk):
    t = pl.program_id(1)
    tm = h_ref.shape[0]

    @pl.when(t == 0)
    def _():
        carry_ref[...] = jnp.zeros_like(carry_ref)

    row_w = lax.broadcasted_iota(jnp.int32, wa_ref.shape, 0)
    w = jnp.where(row_w < 2 * hv, wa_ref[...], jnp.where(row_w < 2 * hv + hf, wb_ref[...], 0.0))
    g_cols = _dot_nt(h_ref[...], w)

    lane = lax.broadcasted_iota(jnp.int32, (tm, GATE_LANES), 1)
    val_c = _gate_act(g_cols, pcol_ref[0:1, :], pcol_ref[1:2, :], lane, hv, hf)
    sub = lax.broadcasted_iota(jnp.int32, (GATE_LANES, tm), 0)
    val_r = val_c.T

    r = lax.broadcasted_iota(jnp.int32, (tm, tm), 0)
    c = lax.broadcasted_iota(jnp.int32, (tm, tm), 1)
    sh = chunk.bit_length() - 1
    same = (r >> sh) == (c >> sh)
    triu_blk = jnp.where((r <= c) & same, 1.0, 0.0).astype(BF16)
    triu_all = jnp.where(r <= c, 1.0, 0.0).astype(BF16)

    pieces_r = _split3(val_r)
    cum_r_blk = sum(jnp.dot(p, triu_blk, preferred_element_type=F32) for p in pieces_r)
    cum_r_all = sum(jnp.dot(p, triu_all, preferred_element_type=F32) for p in pieces_r)
    cum_r_all = cum_r_all + carry_ref[:, 0:1]

    is_decay_r = (sub >= hv) & (sub < 2 * hv)
    is_forget_r = (sub >= 2 * hv) & (sub < 2 * hv + hf)
    rows = jnp.where(is_decay_r, cum_r_blk, jnp.where(is_forget_r, cum_r_all, val_r))
    rows_ref[0] = rows
    carry_ref[...] = jnp.broadcast_to(cum_r_all[:, tm - 1:tm], carry_ref.shape)


def _gates(h2, w_in, layer, col_ba, col_f, pcol, batch, seq, hv, hf):
    m, d = h2.shape
    tm = _tile(seq, 512)
    per_b = seq // tm
    assert col_ba % GATE_LANES == 0 and col_f % GATE_LANES == 2 * hv
    blk_ba, blk_f = col_ba // GATE_LANES, col_f // GATE_LANES
    kern = functools.partial(_gates_kernel, hv=hv, hf=hf, chunk=DN_CHUNK)
    return pl.pallas_call(
        kern,
        grid=(batch, per_b),
        in_specs=[pl.BlockSpec((tm, d), lambda b, t: (b * per_b + t, 0)),
                  pl.BlockSpec((None, GATE_LANES, d), lambda b, t: (layer, blk_ba, 0)),
                  pl.BlockSpec((None, GATE_LANES, d), lambda b, t: (layer, blk_f, 0)),
                  pl.BlockSpec((2, GATE_LANES), lambda b, t: (0, 0))],
        out_specs=pl.BlockSpec((1, GATE_LANES, tm), lambda b, t: (b, 0, t)),
        out_shape=jax.ShapeDtypeStruct((batch, GATE_LANES, seq), F32),
        scratch_shapes=[pltpu.VMEM((GATE_LANES, GATE_LANES), F32)],
        compiler_params=_params("parallel", "arbitrary"),
        name="gates",
    )(h2, w_in, w_in, pcol)


def _causal_conv(u, tail, w):
    out = u * w[CONV_K - 1:CONV_K, :]
    row8 = lax.broadcasted_iota(jnp.int32, tail.shape, 0)
    for s in range(1, CONV_K):
        rolled = pltpu.roll(u, s, 0)
        head = jnp.where(row8 < s, pltpu.roll(tail, s, 0), rolled[0:8])
        shifted = jnp.concatenate([head, rolled[8:]], axis=0)
        out = out + shifted * w[CONV_K - 1 - s:CONV_K - s, :]
    return out


INV_LEVELS = (DN_CHUNK // 8).bit_length() - 1
MASK_DIAG8, MASK_EYE, MASK_STRICT = 0, INV_LEVELS + 1, INV_LEVELS + 2
N_MASKS = INV_LEVELS + 3


def _inverse_masks(n):
    r = lax.broadcasted_iota(jnp.int32, (n, n), 0)
    c = lax.broadcasted_iota(jnp.int32, (n, n), 1)
    masks = [(r >> 3) == (c >> 3)]
    for sh in range(3, 3 + INV_LEVELS):
        masks.append(((r >> (sh + 1)) == (c >> (sh + 1))) & ((r >> sh) == (c >> sh) + 1))
    masks += [r == c, r > c]
    return [jnp.where(m, 1.0, 0.0) for m in masks]


def _unit_lower_inverses(lms, mask_ref, maskb_ref):
    n0s = [-(lm * mask_ref[MASK_DIAG8]) for lm in lms]
    n2s = [_dot(n0, n0) for n0 in n0s]
    n4s = [_dot(n2, n2) for n2 in n2s]
    ps = [mask_ref[MASK_EYE] + n0 for n0 in n0s]
    ps = [p + _dot(p, n2) for p, n2 in zip(ps, n2s)]
    ps = [p + _dot(p, n4) for p, n4 in zip(ps, n4s)]
    lms_b = [lm.astype(BF16) for lm in lms]
    for level in range(1, INV_LEVELS + 1):
        xs = [_dot(lm_b * maskb_ref[level], p) for lm_b, p in zip(lms_b, ps)]
        ps = [p - _dot(p, x) for p, x in zip(ps, xs)]
    return ps


def _deltanet_kernel(q_ref, k_ref, v_ref, z_ref, wq_ref, wk_ref, wv_ref, rows_ref, nw_ref,
                     o_ref, s_ref, qt_ref, kt_ref, vt_ref, mask_ref, maskb_ref, u_ref, lhs1_ref, lhs2_ref,
                     egl_ref,
                     *, hv, rep, groups, unroll, scan_unroll):
    i = pl.program_id(1)
    tb = pl.program_id(2)
    rows_blk = q_ref.shape[0]
    n_chunks = rows_blk // DN_CHUNK
    c_ = DN_CHUNK
    d = HEAD_DIM
    nh = groups * rep

    @pl.when(tb == 0)
    def _():
        s_ref[...] = jnp.zeros_like(s_ref)
        qt_ref[...] = jnp.zeros_like(qt_ref)
        kt_ref[...] = jnp.zeros_like(kt_ref)
        vt_ref[...] = jnp.zeros_like(vt_ref)
        for j, m in enumerate(_inverse_masks(c_)):
            mask_ref[j] = m
            maskb_ref[j] = m.astype(BF16)

    r = lax.broadcasted_iota(jnp.int32, (c_, c_), 0)
    c = lax.broadcasted_iota(jnp.int32, (c_, c_), 1)
    nw = nw_ref[...]

    def tail_of(ref, tail_ref, ci, off):
        prev = pl.multiple_of(jnp.maximum(off - 16, 0), 16)
        inside = ref[pl.ds(prev, 16), :].astype(F32)[8:16]
        return jnp.where(ci == 0, tail_ref[...], inside)

    def prepare_body(j, carry):
        chains = []
        for uu in range(unroll):
            ci = j * unroll + uu
            off = pl.multiple_of(ci * c_, c_)
            sl = pl.ds(off, c_)
            q_raw = q_ref[sl, :].astype(F32)
            k_raw = k_ref[sl, :].astype(F32)
            v_raw = v_ref[sl, :].astype(F32)
            qc = _causal_conv(q_raw, tail_of(q_ref, qt_ref, ci, off), wq_ref[...])
            kc = _causal_conv(k_raw, tail_of(k_ref, kt_ref, ci, off), wk_ref[...])
            vc = _causal_conv(v_raw, tail_of(v_ref, vt_ref, ci, off), wv_ref[...])
            qc = qc * _sigmoid(qc)
            kc = kc * _sigmoid(kc)
            vc = vc * _sigmoid(vc)
            row_b = pl.multiple_of(((i * nh) // 8) * 8, 8)
            row_g = pl.multiple_of(((hv + i * nh) // 8) * 8, 8)
            rows8_b = rows_ref[0, pl.ds(row_b, 8), sl]
            rows8_g = rows_ref[0, pl.ds(row_g, 8), sl]
            sub8 = lax.broadcasted_iota(jnp.int32, (8, c_), 0)
            for g in range(groups):
                qg_ = qc[:, g * d:(g + 1) * d]
                kg_ = kc[:, g * d:(g + 1) * d]
                qn = qg_ * (lax.rsqrt(jnp.sum(qg_ * qg_, axis=-1, keepdims=True) + EPS) * (d ** -0.5))
                kn = kg_ * lax.rsqrt(jnp.sum(kg_ * kg_, axis=-1, keepdims=True) + EPS)
                kk = _dot_nt(kn, kn)
                qk = _dot_nt(qn, kn)
                for hh in range(rep):
                    hl = g * rep + hh
                    head = i * nh + hl
                    beta_r = jnp.sum(jnp.where(sub8 == head % 8, rows8_b, 0.0), axis=0, keepdims=True)
                    gc_r = jnp.sum(jnp.where(sub8 == (hv + head) % 8, rows8_g, 0.0), axis=0, keepdims=True)
                    beta_c = jnp.broadcast_to(beta_r, (c_, c_)).T
                    gc_c = jnp.broadcast_to(gc_r, (c_, c_)).T
                    g_last = gc_r[:, c_ - 1:c_]
                    decay = jnp.exp(jnp.where(r >= c, gc_c - gc_r, -1e30))
                    lm = (beta_c * kk) * (decay * mask_ref[MASK_STRICT])
                    eg = jnp.exp(gc_c)
                    v_h = vc[:, hl * d:(hl + 1) * d]
                    rhs = jnp.concatenate([v_h * beta_c, kn * (beta_c * eg)], axis=1).astype(BF16)
                    kd = kn * jnp.exp(g_last - gc_c)
                    lhs2_ref[hl, ci] = jnp.concatenate([qk * decay, kd.T], axis=0).astype(BF16)
                    egl_ref[hl, ci] = jnp.broadcast_to(jnp.exp(g_last), (8, d))
                    chains.append((hl, ci, sl, lm, rhs, (qn * eg).astype(BF16)))
        tinvs = _unit_lower_inverses([ch[3] for ch in chains], mask_ref, maskb_ref)
        uws = [_dot(tinv, ch[4]) for tinv, ch in zip(tinvs, chains)]
        for uw, (hl, ci, sl, _, _, qg) in zip(uws, chains):
            u_ref[hl, sl, :] = uw[:, :d]
            lhs1_ref[hl, ci] = jnp.concatenate([uw[:, d:].astype(BF16), qg], axis=0)
        return carry

    lax.fori_loop(0, n_chunks // unroll, prepare_body, 0)

    heads = range(nh)

    def scan_chunk(ci):
        off = pl.multiple_of(ci * c_, c_)
        sl = pl.ds(off, c_)
        states = [s_ref[hl] for hl in heads]
        ws_qs = [_dot(lhs1_ref[hl, ci], states[hl]) for hl in heads]
        v_new = [u_ref[hl, sl, :] - ws_qs[hl][:c_] for hl in heads]
        av_kv = [_dot(lhs2_ref[hl, ci], v_new[hl]) for hl in heads]
        for hl in heads:
            s_ref[hl] = states[hl] * egl_ref[hl, ci, 0:1, :] + av_kv[hl][c_:]
        for hl in heads:
            o = ws_qs[hl][c_:] + av_kv[hl][:c_]
            z = z_ref[sl, hl * d:(hl + 1) * d].astype(F32)
            out = _rms(o) * nw * (z * _sigmoid(z))
            o_ref[sl, hl * d:(hl + 1) * d] = out.astype(o_ref.dtype)

    def scan_body(j, carry):
        for uu in range(scan_unroll):
            scan_chunk(j * scan_unroll + uu)
        return carry

    lax.fori_loop(0, n_chunks // scan_unroll, scan_body, 0)

    qt_ref[...] = q_ref[rows_blk - 16:, :].astype(F32)[8:16]
    kt_ref[...] = k_ref[rows_blk - 16:, :].astype(F32)[8:16]
    vt_ref[...] = v_ref[rows_blk - 16:, :].astype(F32)[8:16]


def _deltanet(proj, conv_wt, rows, norm_w, batch, seq, hqk, hv):
    rep = hv // hqk
    assert hv == rep * hqk and DN_CHUNK == HEAD_DIM
    groups = next(g for g in (4, 2, 1) if hqk % g == 0 and 8 % (g * rep) == 0)
    nh = groups * rep
    assert 8 % nh == 0 and hv % nh == 0 and (2 * hqk) % nh == 0
    tb = _tile(seq, 1024)
    per_b = seq // tb
    wqk, wv = groups * HEAD_DIM, nh * HEAD_DIM
    nqk_blk, nv_blk = hqk // groups, hv // nh
    k_blk0 = nqk_blk
    v_blk0 = (2 * hqk) // nh
    z_blk0 = v_blk0 + nv_blk
    row = lambda b, i, t: b * per_b + t
    n_chunks = tb // DN_CHUNK
    unroll = next(u for u in (4, 2, 1) if n_chunks % u == 0 and u * nh <= 16)
    scan_unroll = 2 if n_chunks % 2 == 0 else 1
    kern = functools.partial(_deltanet_kernel, hv=hv, rep=rep, groups=groups, unroll=unroll,
                             scan_unroll=scan_unroll)
    return pl.pallas_call(
        kern,
        grid=(batch, nqk_blk, per_b),
        in_specs=[pl.BlockSpec((tb, wqk), lambda b, i, t: (row(b, i, t), i)),
                  pl.BlockSpec((tb, wqk), lambda b, i, t: (row(b, i, t), k_blk0 + i)),
                  pl.BlockSpec((tb, wv), lambda b, i, t: (row(b, i, t), v_blk0 + i)),
                  pl.BlockSpec((tb, wv), lambda b, i, t: (row(b, i, t), z_blk0 + i)),
                  pl.BlockSpec((CONV_K, wqk), lambda b, i, t: (0, i)),
                  pl.BlockSpec((CONV_K, wqk), lambda b, i, t: (0, k_blk0 + i)),
                  pl.BlockSpec((CONV_K, wv), lambda b, i, t: (0, v_blk0 + i)),
                  pl.BlockSpec((1, GATE_LANES, tb), lambda b, i, t: (b, 0, t)),
                  pl.BlockSpec((1, HEAD_DIM), lambda b, i, t: (0, 0))],
        out_specs=pl.BlockSpec((tb, wv), lambda b, i, t: (row(b, i, t), i)),
        out_shape=jax.ShapeDtypeStruct((batch * seq, hv * HEAD_DIM), BF16),
        scratch_shapes=[pltpu.VMEM((nh, HEAD_DIM, HEAD_DIM), F32),
                        pltpu.VMEM((8, wqk), F32),
                        pltpu.VMEM((8, wqk), F32),
                        pltpu.VMEM((8, wv), F32),
                        pltpu.VMEM((N_MASKS, DN_CHUNK, DN_CHUNK), F32),
                        pltpu.VMEM((N_MASKS, DN_CHUNK, DN_CHUNK), BF16),
                        pltpu.VMEM((nh, tb, HEAD_DIM), F32),
                        pltpu.VMEM((nh, n_chunks, 2 * DN_CHUNK, HEAD_DIM), BF16),
                        pltpu.VMEM((nh, n_chunks, 2 * DN_CHUNK, DN_CHUNK), BF16),
                        pltpu.VMEM((nh, n_chunks, 8, HEAD_DIM), F32)],
        compiler_params=_params("parallel", "parallel", "arbitrary"),
        name="deltanet",
    )(proj, proj, proj, proj, conv_wt, conv_wt, conv_wt, rows, norm_w.reshape(1, HEAD_DIM))


LOG2E = 1.4426950408889634


MXU_COLS = 256


def _reduce_rows(x, op):
    rows, lanes = x.shape
    slabs = 8
    if rows % (8 * slabs) == 0:
        x3 = x.reshape(slabs, rows // slabs, lanes)
        x = x3[0]
        for j in range(1, slabs):
            x = op(x, x3[j])
    final = jnp.max if op is jnp.maximum else jnp.sum
    return final(x, axis=0, keepdims=True)


ONES_ROWS = 16


def _fox_kernel(q_ref, k_ref, v_ref, f_ref, o_ref, vt_ref, fcol_ref, s_ref, m_ref, acc_ref, *, scale):
    qi = pl.program_id(2)
    tq = q_ref.shape[0]
    tk = tq
    seq = k_ref.shape[0]
    qt = min(MXU_COLS, tq)
    d = HEAD_DIM
    nhd = q_ref.shape[1] // d
    hd_lanes = lambda hd: slice(hd * d, (hd + 1) * d)

    @pl.when(qi == 0)
    def _():
        n_blk = seq // d
        group = next(g for g in (4, 2, 1) if n_blk % g == 0)

        def prep(bi, carry):
            for j in range(group):
                off = pl.multiple_of((bi * group + j) * d, d)
                for hd in range(nhd):
                    v_blk = v_ref[pl.ds(off, d), hd_lanes(hd)]
                    vt_ref[hd, 0:d, pl.ds(off, d)] = v_blk.astype(F32).T.astype(BF16)
                    frow = f_ref[0, hd, :, pl.ds(off, d)] * LOG2E
                    fcol_ref[hd, pl.ds(off, d), :] = jnp.broadcast_to(frow, (d, d)).T
            return carry
        lax.fori_loop(0, n_blk // group, prep, 0)
        for hd in range(nhd):
            vt_ref[hd, d:, :] = jnp.ones((ONES_ROWS, seq), BF16)

    m_ref[...] = jnp.full_like(m_ref, -1e30)
    acc_ref[...] = jnp.zeros_like(acc_ref)

    def scores(kj, slot):
        off = pl.multiple_of(kj * tk, tk)
        for hd in range(nhd):
            s_ref[slot, hd] = _dot_nt(k_ref[pl.ds(off, tk), hd_lanes(hd)], q_ref[:, hd_lanes(hd)])

    def consume(kj, slot, masked):
        off = pl.multiple_of(kj * tk, tk)
        for hd in range(nhd):
            fcol = fcol_ref[hd, pl.ds(off, tk), :]
            fcol = jnp.concatenate([fcol] * (qt // d), axis=1)
            vt = vt_ref[hd, :, pl.ds(off, tk)]
            for t in range(tq // qt):
                lanes = slice(t * qt, (t + 1) * qt)
                nk = min((t + 1) * qt, tk) if masked else tk
                s = s_ref[slot, hd, 0:nk, lanes] * (scale * LOG2E) - fcol[0:nk]
                if masked:
                    key = lax.broadcasted_iota(jnp.int32, (nk, qt), 0)
                    qry = lax.broadcasted_iota(jnp.int32, (nk, qt), 1) + t * qt
                    s = jnp.where(key <= qry, s, -1e30)
                m_prev = m_ref[hd, :, lanes]
                m_new = jnp.maximum(m_prev, _reduce_rows(s, jnp.maximum))
                p = jnp.exp2(s - m_new)
                alpha = jnp.exp2(m_prev - m_new)
                m_ref[hd, :, lanes] = m_new
                acc_ref[hd, :, lanes] = alpha * acc_ref[hd, :, lanes] + _dot(vt[:, 0:nk], p)

    scores(0, 0)

    def body(i, carry):
        kj = 2 * i
        scores(kj + 1, 1)
        consume(kj, 0, False)
        scores(kj + 2, 0)
        consume(kj + 1, 1, False)
        return carry

    lax.fori_loop(0, qi // 2, body, 0)

    @pl.when(qi % 2 == 0)
    def _():
        consume(qi, 0, True)

    @pl.when(qi % 2 == 1)
    def _():
        scores(qi, 1)
        consume(qi - 1, 0, False)
        consume(qi, 1, True)

    for hd in range(nhd):
        out = (acc_ref[hd, 0:d, :] / acc_ref[hd, d:d + 1, :]).T
        o_ref[:, hd_lanes(hd)] = out.astype(o_ref.dtype)


def _fox(proj, f_rows, batch, seq, base, hf):
    tq = _tile(seq, 512)
    nq = seq // tq
    nhd = 2 if hf % 2 == 0 and base % 2 == 0 else 1
    w = nhd * HEAD_DIM
    b0, nblk = base // nhd, hf // nhd
    kern = functools.partial(_fox_kernel, scale=HEAD_DIM ** -0.5)
    return pl.pallas_call(
        kern,
        grid=(batch, nblk, nq),
        in_specs=[pl.BlockSpec((tq, w), lambda b, h, qi: (b * nq + qi, b0 + h)),
                  pl.BlockSpec((seq, w), lambda b, h, qi: (b, b0 + nblk + h)),
                  pl.BlockSpec((seq, w), lambda b, h, qi: (b, b0 + 2 * nblk + h)),
                  pl.BlockSpec((1, nhd, 1, seq), lambda b, h, qi: (b, h, 0, 0))],
        out_specs=pl.BlockSpec((tq, w), lambda b, h, qi: (b * nq + qi, h)),
        out_shape=jax.ShapeDtypeStruct((batch * seq, hf * HEAD_DIM), BF16),
        scratch_shapes=[pltpu.VMEM((nhd, HEAD_DIM + ONES_ROWS, seq), BF16),
                        pltpu.VMEM((nhd, seq, HEAD_DIM), F32),
                        pltpu.VMEM((2, nhd, tq, tq), F32),
                        pltpu.VMEM((nhd, 1, tq), F32),
                        pltpu.VMEM((nhd, HEAD_DIM + ONES_ROWS, tq), F32)],
        compiler_params=_params("parallel", "parallel", "arbitrary"),
        name="fox_attention",
    )(proj, proj, proj, f_rows)


def _merge_kernel(a1_ref, w1_ref, a2_ref, w2_ref, m1_ref, m2_ref, o_ref):
    y1 = _dot(a1_ref[...], w1_ref[...])
    y2 = _dot(a2_ref[...], w2_ref[...])
    g1 = _sigmoid(m1_ref[...].astype(F32))
    g2 = _sigmoid(m2_ref[...].astype(F32))
    o_ref[...] = (g1 * y1 + g2 * y2).astype(o_ref.dtype)


def _merge(o_dn, w_dn, o_fox, w_fox, layer, proj, merge_base_cols):
    m, k1 = o_dn.shape
    k2 = o_fox.shape[1]
    d = w_dn.shape[2]
    tm, tn = _tile(m, 512), _tile(d, 512)
    assert merge_base_cols % tn == 0
    mb = merge_base_cols // tn
    return pl.pallas_call(
        _merge_kernel,
        grid=(m // tm, d // tn),
        in_specs=[pl.BlockSpec((tm, k1), lambda i, j: (i, 0)),
                  pl.BlockSpec((None, k1, tn), lambda i, j: (layer, 0, j)),
                  pl.BlockSpec((tm, k2), lambda i, j: (i, 0)),
                  pl.BlockSpec((None, k2, tn), lambda i, j: (layer, 0, j)),
                  pl.BlockSpec((tm, tn), lambda i, j: (i, mb + j)),
                  pl.BlockSpec((tm, tn), lambda i, j: (i, mb + d // tn + j))],
        out_specs=pl.BlockSpec((tm, tn), lambda i, j: (i, j)),
        out_shape=jax.ShapeDtypeStruct((m, d), BF16),
        compiler_params=_params("parallel", "parallel"),
        name="branch_merge",
    )(o_dn, w_dn, o_fox, w_fox, proj, proj)


def _cast_weight_once(w_ref, wb_ref):
    @pl.when(pl.program_id(1) == 0)
    def _():
        k = w_ref.shape[0]
        for r0 in range(0, k, W_ROWS):
            r1 = min(r0 + W_ROWS, k)
            wb_ref[r0:r1, :] = w_ref[r0:r1, :].astype(BF16)


def _mm_f32_kernel(a_ref, w_ref, o_ref, wb_ref):
    _cast_weight_once(w_ref, wb_ref)
    o_ref[...] = _dot(a_ref[...], wb_ref[...])


def _out_proj(a, w, layer):
    m, k = a.shape
    d = w.shape[2]
    pref = 512 if k > 4096 else 1024
    tm, tn = _tile(m, pref), _tile(d, pref)
    return pl.pallas_call(
        _mm_f32_kernel,
        grid=(d // tn, m // tm),
        in_specs=[pl.BlockSpec((tm, k), lambda j, i: (i, 0)),
                  pl.BlockSpec((None, k, tn), lambda j, i: (layer, 0, j))],
        out_specs=pl.BlockSpec((tm, tn), lambda j, i: (i, j)),
        out_shape=jax.ShapeDtypeStruct((m, d), F32),
        scratch_shapes=[pltpu.VMEM((k, tn), BF16)],
        compiler_params=_params("parallel", "arbitrary"),
        name="out_proj",
    )(a, w)


def _res_norm_kernel(y_ref, x_ref, g_ref, gate_ref, *rest, emit_h):
    x_new = x_ref[...] + gate_ref[0] * (_rms(y_ref[...]) * g_ref[...])
    if emit_h:
        g2_ref, sc_ref, sh_ref, o_ref, h_ref = rest
        o_ref[...] = x_new
        h_ref[...] = ((_rms(x_new) * g2_ref[...]) * (1.0 + sc_ref[0]) + sh_ref[0]).astype(h_ref.dtype)
    else:
        (o_ref,) = rest
        o_ref[...] = x_new


def _residual_norm(y, x2, gain, mod3, gate_idx, seq, nxt=None):
    m, d = x2.shape
    tm = _tile(seq, 256)
    per_b = seq // tm
    row = pl.BlockSpec((tm, d), lambda i: (i, 0))
    vec = pl.BlockSpec((1, d), lambda i: (0, 0))
    modv = lambda idx: pl.BlockSpec((1, 1, d), lambda i: ((i // per_b) * 6 + idx, 0, 0))
    in_specs = [row, row, vec, modv(gate_idx)]
    args = [y, x2, gain.reshape(1, d), mod3]
    out_specs, out_shape = [row], [jax.ShapeDtypeStruct((m, d), F32)]
    if nxt is not None:
        gain2, mod3n, sc_idx, sh_idx = nxt
        in_specs += [vec, modv(sc_idx), modv(sh_idx)]
        args += [gain2.reshape(1, d), mod3n, mod3n]
        out_specs.append(row)
        out_shape.append(jax.ShapeDtypeStruct((m, d), BF16))
    out = pl.pallas_call(
        functools.partial(_res_norm_kernel, emit_h=nxt is not None),
        grid=(m // tm,),
        in_specs=in_specs,
        out_specs=out_specs,
        out_shape=out_shape,
        compiler_params=_params("parallel"),
        name="residual_norm",
    )(*args)
    return (out[0], out[1]) if nxt is not None else (out[0], None)


def _glu_kernel(a_ref, wg_ref, wu_ref, o_ref, wgb_ref, wub_ref):
    _cast_weight_once(wg_ref, wgb_ref)
    _cast_weight_once(wu_ref, wub_ref)
    a = a_ref[...]
    g = _dot(a, wgb_ref[...])
    u = _dot(a, wub_ref[...])
    o_ref[...] = (g * _sigmoid(g) * u).astype(o_ref.dtype)


def _glu(a, wg, wu, layer):
    m, k = a.shape
    n = wg.shape[2]
    tm, tn = _tile(m, 1024), _tile(n, 512)
    return pl.pallas_call(
        _glu_kernel,
        grid=(n // tn, m // tm),
        in_specs=[pl.BlockSpec((tm, k), lambda j, i: (i, 0)),
                  pl.BlockSpec((None, k, tn), lambda j, i: (layer, 0, j)),
                  pl.BlockSpec((None, k, tn), lambda j, i: (layer, 0, j))],
        out_specs=pl.BlockSpec((tm, tn), lambda j, i: (i, j)),
        out_shape=jax.ShapeDtypeStruct((m, n), BF16),
        scratch_shapes=[pltpu.VMEM((k, tn), BF16), pltpu.VMEM((k, tn), BF16)],
        compiler_params=_params("parallel", "arbitrary"),
        name="swiglu_up",
    )(a, wg, wu)


def kernel(x, c, w_ada, b_ada, norm_gains, w_in, dn_conv, dn_a_log, dn_dt_bias, dn_norm_w, fox_f_bias,
           w_branch_dn, w_branch_fox, w_out, w_gate, w_up, w_down):
    batch, seq, d = x.shape
    depth = w_ada.shape[0]
    hv = dn_a_log.shape[1]
    hf = fox_f_bias.shape[1]
    v_dim = hv * HEAD_DIM
    conv_dim = dn_conv.shape[1]
    qk_dim = (conv_dim - v_dim) // 2
    hqk = qk_dim // HEAD_DIM
    fox_dim = hf * HEAD_DIM
    assert 2 * hv + hf <= GATE_LANES

    o_z = conv_dim
    o_b = o_z + v_dim
    o_a = o_b + hv
    o_fq = o_a + hv
    o_ff = o_fq + 3 * fox_dim
    o_mg = o_ff + hf
    fox_base = (conv_dim + v_dim) // HEAD_DIM
    merge_base = conv_dim + v_dim + 3 * fox_dim

    mod = _modulation(c, w_ada, b_ada)
    x2 = x.reshape(batch * seq, d)
    pad = GATE_LANES - (2 * hv + hf)
    zpad = jnp.zeros((pad,), F32)

    main_cols = ((0, o_b), (o_fq, 3 * fox_dim), (o_mg, w_in.shape[2] - o_mg))
    w_in_t = jnp.swapaxes(w_in, 1, 2)
    w_dn_all, w_fox_all = w_branch_dn.astype(BF16), w_branch_fox.astype(BF16)
    mod3s = [mod[l].reshape(batch * 6, 1, d) for l in range(depth)]

    h = _norm_mod(x2, norm_gains[0, 0], mod3s[0], 1, 0, seq)
    for l in range(depth):
        mod3 = mod3s[l]
        bias = jnp.concatenate([jnp.zeros((hv,), F32), dn_dt_bias[l], fox_f_bias[l], zpad])
        mult = jnp.concatenate([jnp.ones((hv,), F32), -jnp.exp(dn_a_log[l]), jnp.ones((hf,), F32), zpad])
        pcol = jnp.stack([bias, mult], axis=0)

        proj = _in_proj(h, w_in_t, l, main_cols)
        rows = _gates(h, w_in_t, l, o_b, o_ff, pcol, batch, seq, hv, hf)
        o_dn = _deltanet(proj, dn_conv[l].T, rows, dn_norm_w[l], batch, seq, hqk, hv)
        f_rows = rows[:, 2 * hv:2 * hv + hf, :].reshape(batch, hf, 1, seq)
        o_fox = _fox(proj, f_rows, batch, seq, fox_base, hf)
        ymix = _merge(o_dn, w_dn_all, o_fox, w_fox_all, l, proj, merge_base)
        y = _out_proj(ymix, w_out, l)
        x2, h = _residual_norm(y, x2, norm_gains[l, 1], mod3, 2, seq,
                               nxt=(norm_gains[l, 2], mod3, 4, 3))

        gu = _glu(h, w_gate, w_up, l)
        y = _out_proj(gu, w_down, l)
        nxt = (norm_gains[l + 1, 0], mod3s[l + 1], 1, 0) if l + 1 < depth else None
        x2, h = _residual_norm(y, x2, norm_gains[l, 3], mod3, 5, seq, nxt=nxt)

    return x2.reshape(batch, seq, d)
```

```python
import functools

import jax
import jax.numpy as jnp
from jax import lax
from jax.experimental import pallas as pl
from jax.experimental.pallas import tpu as pltpu

EPS = 1e-6
HEAD_DIM = 128
CONV_K = 4
DN_CHUNK = 128
GATE_LANES = 128
VMEM_LIMIT_BYTES = 48 * 1024 * 1024

F32 = jnp.float32
BF16 = jnp.bfloat16


def _params(*sem):
    return pltpu.CompilerParams(dimension_semantics=sem, vmem_limit_bytes=VMEM_LIMIT_BYTES)


def _tile(dim, pref):
    if dim <= pref:
        return dim
    t = pref - pref % HEAD_DIM
    while dim % t:
        t -= HEAD_DIM
    assert t > 0, (dim, pref)
    return t


def _dot(a, b):
    return jnp.dot(a.astype(BF16), b.astype(BF16), preferred_element_type=F32)


def _dot_nt(a, b):
    return lax.dot_general(a.astype(BF16), b.astype(BF16), (((1,), (1,)), ((), ())),
                           preferred_element_type=F32)


def _dot_tn(a, b):
    return lax.dot_general(a.astype(BF16), b.astype(BF16), (((0,), (0,)), ((), ())),
                           preferred_element_type=F32)


def _sigmoid(x):
    return 0.5 * jnp.tanh(0.5 * x) + 0.5


def _rms(x):
    return x * lax.rsqrt(jnp.mean(x * x, axis=-1, keepdims=True) + EPS)


def _mod_kernel(c_ref, w_ref, b_ref, o_ref):
    c = c_ref[...]
    cond = c * _sigmoid(c)
    o_ref[0] = _dot(cond, w_ref[0]) + b_ref[0]


def _modulation(c, w_ada, b_ada):
    depth, d, n = w_ada.shape
    b = c.shape[0]
    rows = 16
    c_pad = jnp.zeros((rows, d), F32).at[:b].set(c)
    tn = _tile(n, 1024)
    out = pl.pallas_call(
        _mod_kernel,
        grid=(depth, n // tn),
        in_specs=[pl.BlockSpec((rows, d), lambda l, j: (0, 0)),
                  pl.BlockSpec((1, d, tn), lambda l, j: (l, 0, j)),
                  pl.BlockSpec((1, 1, tn), lambda l, j: (l, 0, j))],
        out_specs=pl.BlockSpec((1, rows, tn), lambda l, j: (l, 0, j)),
        out_shape=jax.ShapeDtypeStruct((depth, rows, n), F32),
        compiler_params=_params("parallel", "parallel"),
        name="adaln_mod",
    )(c_pad, w_ada, b_ada.reshape(depth, 1, n))
    return out[:, :b].reshape(depth, b, 6, d)


def _norm_mod_kernel(x_ref, g_ref, sc_ref, sh_ref, o_ref):
    y = _rms(x_ref[...]) * g_ref[...]
    o_ref[...] = (y * (1.0 + sc_ref[0]) + sh_ref[0]).astype(o_ref.dtype)


def _norm_mod(x2, gain, mod3, sc_idx, sh_idx, seq):
    m, d = x2.shape
    tm = _tile(seq, 512)
    per_b = seq // tm
    return pl.pallas_call(
        _norm_mod_kernel,
        grid=(m // tm,),
        in_specs=[pl.BlockSpec((tm, d), lambda i: (i, 0)),
                  pl.BlockSpec((1, d), lambda i: (0, 0)),
                  pl.BlockSpec((1, 1, d), lambda i: ((i // per_b) * 6 + sc_idx, 0, 0)),
                  pl.BlockSpec((1, 1, d), lambda i: ((i // per_b) * 6 + sh_idx, 0, 0))],
        out_specs=pl.BlockSpec((tm, d), lambda i: (i, 0)),
        out_shape=jax.ShapeDtypeStruct((m, d), BF16),
        compiler_params=_params("parallel"),
        name="norm_mod",
    )(x2, gain.reshape(1, d), mod3, mod3)


W_ROWS = 256


def _in_proj_kernel(a_ref, wa_ref, wb_ref, o_ref, w_ref, *, segments):
    j = pl.program_id(0)
    tn = wa_ref.shape[0]

    @pl.when(pl.program_id(1) == 0)
    def _():
        for lo, hi, shift in segments:
            @pl.when((j >= lo) & (j < hi))
            def _():
                for r0 in range(0, tn - shift, W_ROWS):
                    r1 = min(r0 + W_ROWS, tn - shift)
                    w_ref[r0:r1, :] = wa_ref[r0 + shift:r1 + shift, :].astype(BF16)
                if shift:
                    w_ref[tn - shift:, :] = wb_ref[:shift, :].astype(BF16)

    o_ref[...] = _dot_nt(a_ref[...], w_ref[...]).astype(o_ref.dtype)


def _in_proj(a, w_in_t, layer, seg_cols):
    m, k = a.shape
    n_out = sum(w for _, w in seg_cols)
    tn = 1024
    while any(w % tn for _, w in seg_cols):
        tn //= 2
    assert tn >= HEAD_DIM
    tm = _tile(m, 1024)
    segments, out0 = [], 0
    for src0, width in seg_cols:
        shift = src0 - out0
        assert 0 <= shift < HEAD_DIM
        segments.append((out0 // tn, (out0 + width) // tn, shift))
        out0 += width
    kern = functools.partial(_in_proj_kernel, segments=tuple(segments))
    nb = tn // HEAD_DIM
    last_blk = (w_in_t.shape[1] - 1) // HEAD_DIM
    return pl.pallas_call(
        kern,
        grid=(n_out // tn, m // tm),
        in_specs=[pl.BlockSpec((tm, k), lambda j, i: (i, 0)),
                  pl.BlockSpec((None, tn, k), lambda j, i: (layer, j, 0)),
                  pl.BlockSpec((None, HEAD_DIM, k),
                               lambda j, i: (layer, jnp.minimum((j + 1) * nb, last_blk), 0))],
        out_specs=pl.BlockSpec((tm, tn), lambda j, i: (i, j)),
        out_shape=jax.ShapeDtypeStruct((m, n_out), BF16),
        scratch_shapes=[pltpu.VMEM((tn, k), BF16)],
        compiler_params=_params("parallel", "arbitrary"),
        name="in_proj",
    )(a, w_in_t, w_in_t)


def _split3(x):
    hi = x.astype(BF16)
    r1 = x - hi.astype(F32)
    mid = r1.astype(BF16)
    lo = (r1 - mid.astype(F32)).astype(BF16)
    return hi, mid, lo


def _gate_act(x, bias, mult, idx, hv, hf):
    xb = x + bias
    e = jnp.exp(-jnp.abs(xb))
    l1p = jnp.log(1.0 + e)
    sig = jnp.where(xb >= 0, 1.0, e) / (1.0 + e)
    softplus = jnp.maximum(xb, 0.0) + l1p
    logsig = jnp.minimum(xb, 0.0) - l1p
    return jnp.where(idx < hv, sig,
                     jnp.where(idx < 2 * hv, mult * softplus,
                               jnp.where(idx < 2 * hv + hf, logsig, 0.0)))


def _gates_kernel(h_ref, wa_ref, wb_ref, pcol_ref, rows_ref, carry_ref, *, hv, hf, chunk):
    t = pl.program_id(1)
    tm = h_ref.shape[0]

    @pl.when(t == 0)
    def _():
        carry_ref[...] = jnp.zeros_like(carry_ref)

    row_w = lax.broadcasted_iota(jnp.int32, wa_ref.shape, 0)
    w = jnp.where(row_w < 2 * hv, wa_ref[...], jnp.where(row_w < 2 * hv + hf, wb_ref[...], 0.0))
    g_cols = _dot_nt(h_ref[...], w)

    lane = lax.broadcasted_iota(jnp.int32, (tm, GATE_LANES), 1)
    val_c = _gate_act(g_cols, pcol_ref[0:1, :], pcol_ref[1:2, :], lane, hv, hf)
    sub = lax.broadcasted_iota(jnp.int32, (GATE_LANES, tm), 0)
    val_r = val_c.T

    r = lax.broadcasted_iota(jnp.int32, (tm, tm), 0)
    c = lax.broadcasted_iota(jnp.int32, (tm, tm), 1)
    sh = chunk.bit_length() - 1
    same = (r >> sh) == (c >> sh)
    triu_blk = jnp.where((r <= c) & same, 1.0, 0.0).astype(BF16)
    triu_all = jnp.where(r <= c, 1.0, 0.0).astype(BF16)

    pieces_r = _split3(val_r)
    cum_r_blk = sum(jnp.dot(p, triu_blk, preferred_element_type=F32) for p in pieces_r)
    cum_r_all = sum(jnp.dot(p, triu_all, preferred_element_type=F32) for p in pieces_r)
    cum_r_all = cum_r_all + carry_ref[:, 0:1]

    is_decay_r = (sub >= hv) & (sub < 2 * hv)
    is_forget_r = (sub >= 2 * hv) & (sub < 2 * hv + hf)
    rows = jnp.where(is_decay_r, cum_r_blk, jnp.where(is_forget_r, cum_r_all, val_r))
    rows_ref[0] = rows
    carry_ref[...] = jnp.broadcast_to(cum_r_all[:, tm - 1:tm], carry_ref.shape)


def _gates(h2, w_in, layer, col_ba, col_f, pcol, batch, seq, hv, hf):
    m, d = h2.shape
    tm = _tile(seq, 512)
    per_b = seq // tm
    assert col_ba % GATE_LANES == 0 and col_f % GATE_LANES == 2 * hv
    blk_ba, blk_f = col_ba // GATE_LANES, col_f // GATE_LANES
    kern = functools.partial(_gates_kernel, hv=hv, hf=hf, chunk=DN_CHUNK)
    return pl.pallas_call(
        kern,
        grid=(batch, per_b),
        in_specs=[pl.BlockSpec((tm, d), lambda b, t: (b * per_b + t, 0)),
                  pl.BlockSpec((None, GATE_LANES, d), lambda b, t: (layer, blk_ba, 0)),
                  pl.BlockSpec((None, GATE_LANES, d), lambda b, t: (layer, blk_f, 0)),
                  pl.BlockSpec((2, GATE_LANES), lambda b, t: (0, 0))],
        out_specs=pl.BlockSpec((1, GATE_LANES, tm), lambda b, t: (b, 0, t)),
        out_shape=jax.ShapeDtypeStruct((batch, GATE_LANES, seq), F32),
        scratch_shapes=[pltpu.VMEM((GATE_LANES, GATE_LANES), F32)],
        compiler_params=_params("parallel", "arbitrary"),
        name="gates",
    )(h2, w_in, w_in, pcol)


def _causal_conv(u, tail, w):
    out = u * w[CONV_K - 1:CONV_K, :]
    row8 = lax.broadcasted_iota(jnp.int32, tail.shape, 0)
    for s in range(1, CONV_K):
        rolled = pltpu.roll(u, s, 0)
        head = jnp.where(row8 < s, pltpu.roll(tail, s, 0), rolled[0:8])
        shifted = jnp.concatenate([head, rolled[8:]], axis=0)
        out = out + shifted * w[CONV_K - 1 - s:CONV_K - s, :]
    return out


INV_LEVELS = (DN_CHUNK // 8).bit_length() - 1
MASK_DIAG8, MASK_EYE, MASK_STRICT = 0, INV_LEVELS + 1, INV_LEVELS + 2
N_MASKS = INV_LEVELS + 3


def _inverse_masks(n):
    r = lax.broadcasted_iota(jnp.int32, (n, n), 0)
    c = lax.broadcasted_iota(jnp.int32, (n, n), 1)
    masks = [(r >> 3) == (c >> 3)]
    for sh in range(3, 3 + INV_LEVELS):
        masks.append(((r >> (sh + 1)) == (c >> (sh + 1))) & ((r >> sh) == (c >> sh) + 1))
    masks += [r == c, r > c]
    return [jnp.where(m, 1.0, 0.0) for m in masks]


def _unit_lower_inverses(lms, mask_ref, maskb_ref):
    n0s = [-(lm * mask_ref[MASK_DIAG8]) for lm in lms]
    n2s = [_dot(n0, n0) for n0 in n0s]
    n4s = [_dot(n2, n2) for n2 in n2s]
    ps = [mask_ref[MASK_EYE] + n0 for n0 in n0s]
    ps = [p + _dot(p, n2) for p, n2 in zip(ps, n2s)]
    ps = [p + _dot(p, n4) for p, n4 in zip(ps, n4s)]
    lms_b = [lm.astype(BF16) for lm in lms]
    for level in range(1, INV_LEVELS + 1):
        xs = [_dot(lm_b * maskb_ref[level], p) for lm_b, p in zip(lms_b, ps)]
        ps = [p - _dot(p, x) for p, x in zip(ps, xs)]
    return ps


def _deltanet_kernel(q_ref, k_ref, v_ref, z_ref, wq_ref, wk_ref, wv_ref, rows_ref, nw_ref,
                     o_ref, s_ref, qt_ref, kt_ref, vt_ref, mask_ref, maskb_ref, u_ref, lhs1_ref, lhs2_ref,
                     egl_ref,
                     *, hv, rep, groups, unroll, scan_unroll):
    i = pl.program_id(1)
    tb = pl.program_id(2)
    rows_blk = q_ref.shape[0]
    n_chunks = rows_blk // DN_CHUNK
    c_ = DN_CHUNK
    d = HEAD_DIM
    nh = groups * rep

    @pl.when(tb == 0)
    def _():
        s_ref[...] = jnp.zeros_like(s_ref)
        qt_ref[...] = jnp.zeros_like(qt_ref)
        kt_ref[...] = jnp.zeros_like(kt_ref)
        vt_ref[...] = jnp.zeros_like(vt_ref)
        for j, m in enumerate(_inverse_masks(c_)):
            mask_ref[j] = m
            maskb_ref[j] = m.astype(BF16)

    r = lax.broadcasted_iota(jnp.int32, (c_, c_), 0)
    c = lax.broadcasted_iota(jnp.int32, (c_, c_), 1)
    nw = nw_ref[...]

    def tail_of(ref, tail_ref, ci, off):
        prev = pl.multiple_of(jnp.maximum(off - 16, 0), 16)
        inside = ref[pl.ds(prev, 16), :].astype(F32)[8:16]
        return jnp.where(ci == 0, tail_ref[...], inside)

    def prepare_body(j, carry):
        chains = []
        for uu in range(unroll):
            ci = j * unroll + uu
            off = pl.multiple_of(ci * c_, c_)
            sl = pl.ds(off, c_)
            q_raw = q_ref[sl, :].astype(F32)
            k_raw = k_ref[sl, :].astype(F32)
            v_raw = v_ref[sl, :].astype(F32)
            qc = _causal_conv(q_raw, tail_of(q_ref, qt_ref, ci, off), wq_ref[...])
            kc = _causal_conv(k_raw, tail_of(k_ref, kt_ref, ci, off), wk_ref[...])
            vc = _causal_conv(v_raw, tail_of(v_ref, vt_ref, ci, off), wv_ref[...])
            qc = qc * _sigmoid(qc)
            kc = kc * _sigmoid(kc)
            vc = vc * _sigmoid(vc)
            row_b = pl.multiple_of(((i * nh) // 8) * 8, 8)
            row_g = pl.multiple_of(((hv + i * nh) // 8) * 8, 8)
            rows8_b = rows_ref[0, pl.ds(row_b, 8), sl]
            rows8_g = rows_ref[0, pl.ds(row_g, 8), sl]
            sub8 = lax.broadcasted_iota(jnp.int32, (8, c_), 0)
            for g in range(groups):
                qg_ = qc[:, g * d:(g + 1) * d]
                kg_ = kc[:, g * d:(g + 1) * d]
                qn = qg_ * (lax.rsqrt(jnp.sum(qg_ * qg_, axis=-1, keepdims=True) + EPS) * (d ** -0.5))
                kn = kg_ * lax.rsqrt(jnp.sum(kg_ * kg_, axis=-1, keepdims=True) + EPS)
                kk = _dot_nt(kn, kn)
                qk = _dot_nt(qn, kn)
                for hh in range(rep):
                    hl = g * rep + hh
                    head = i * nh + hl
                    beta_r = jnp.sum(jnp.where(sub8 == head % 8, rows8_b, 0.0), axis=0, keepdims=True)
                    gc_r = jnp.sum(jnp.where(sub8 == (hv + head) % 8, rows8_g, 0.0), axis=0, keepdims=True)
                    beta_c = jnp.broadcast_to(beta_r, (c_, c_)).T
                    gc_c = jnp.broadcast_to(gc_r, (c_, c_)).T
                    g_last = gc_r[:, c_ - 1:c_]
                    decay = jnp.exp(jnp.where(r >= c, gc_c - gc_r, -1e30))
                    lm = (beta_c * kk) * (decay * mask_ref[MASK_STRICT])
                    eg = jnp.exp(gc_c)
                    v_h = vc[:, hl * d:(hl + 1) * d]
                    rhs = jnp.concatenate([v_h * beta_c, kn * (beta_c * eg)], axis=1).astype(BF16)
                    kd = kn * jnp.exp(g_last - gc_c)
                    lhs2_ref[hl, ci] = jnp.concatenate([qk * decay, kd.T], axis=0).astype(BF16)
                    egl_ref[hl, ci] = jnp.broadcast_to(jnp.exp(g_last), (8, d))
                    chains.append((hl, ci, sl, lm, rhs, (qn * eg).astype(BF16)))
        tinvs = _unit_lower_inverses([ch[3] for ch in chains], mask_ref, maskb_ref)
        uws = [_dot(tinv, ch[4]) for tinv, ch in zip(tinvs, chains)]
        for uw, (hl, ci, sl, _, _, qg) in zip(uws, chains):
            u_ref[hl, sl, :] = uw[:, :d]
            lhs1_ref[hl, ci] = jnp.concatenate([uw[:, d:].astype(BF16), qg], axis=0)
        return carry

    lax.fori_loop(0, n_chunks // unroll, prepare_body, 0)

    heads = range(nh)

    def scan_chunk(ci):
        off = pl.multiple_of(ci * c_, c_)
        sl = pl.ds(off, c_)
        states = [s_ref[hl] for hl in heads]
        ws_qs = [_dot(lhs1_ref[hl, ci], states[hl]) for hl in heads]
        v_new = [u_ref[hl, sl, :] - ws_qs[hl][:c_] for hl in heads]
        av_kv = [_dot(lhs2_ref[hl, ci], v_new[hl]) for hl in heads]
        for hl in heads:
            s_ref[hl] = states[hl] * egl_ref[hl, ci, 0:1, :] + av_kv[hl][c_:]
        for hl in heads:
            o = ws_qs[hl][c_:] + av_kv[hl][:c_]
            z = z_ref[sl, hl * d:(hl + 1) * d].astype(F32)
            out = _rms(o) * nw * (z * _sigmoid(z))
            o_ref[sl, hl * d:(hl + 1) * d] = out.astype(o_ref.dtype)

    def scan_body(j, carry):
        for uu in range(scan_unroll):
            scan_chunk(j * scan_unroll + uu)
        return carry

    lax.fori_loop(0, n_chunks // scan_unroll, scan_body, 0)

    qt_ref[...] = q_ref[rows_blk - 16:, :].astype(F32)[8:16]
    kt_ref[...] = k_ref[rows_blk - 16:, :].astype(F32)[8:16]
    vt_ref[...] = v_ref[rows_blk - 16:, :].astype(F32)[8:16]


def _deltanet(proj, conv_wt, rows, norm_w, batch, seq, hqk, hv):
    rep = hv // hqk
    assert hv == rep * hqk and DN_CHUNK == HEAD_DIM
    groups = next(g for g in (4, 2, 1) if hqk % g == 0 and 8 % (g * rep) == 0)
    nh = groups * rep
    assert 8 % nh == 0 and hv % nh == 0 and (2 * hqk) % nh == 0
    tb = _tile(seq, 1024)
    per_b = seq // tb
    wqk, wv = groups * HEAD_DIM, nh * HEAD_DIM
    nqk_blk, nv_blk = hqk // groups, hv // nh
    k_blk0 = nqk_blk
    v_blk0 = (2 * hqk) // nh
    z_blk0 = v_blk0 + nv_blk
    row = lambda b, i, t: b * per_b + t
    n_chunks = tb // DN_CHUNK
    unroll = next(u for u in (4, 2, 1) if n_chunks % u == 0 and u * nh <= 16)
    scan_unroll = 2 if n_chunks % 2 == 0 else 1
    kern = functools.partial(_deltanet_kernel, hv=hv, rep=rep, groups=groups, unroll=unroll,
                             scan_unroll=scan_unroll)
    return pl.pallas_call(
        kern,
        grid=(batch, nqk_blk, per_b),
        in_specs=[pl.BlockSpec((tb, wqk), lambda b, i, t: (row(b, i, t), i)),
                  pl.BlockSpec((tb, wqk), lambda b, i, t: (row(b, i, t), k_blk0 + i)),
                  pl.BlockSpec((tb, wv), lambda b, i, t: (row(b, i, t), v_blk0 + i)),
                  pl.BlockSpec((tb, wv), lambda b, i, t: (row(b, i, t), z_blk0 + i)),
                  pl.BlockSpec((CONV_K, wqk), lambda b, i, t: (0, i)),
                  pl.BlockSpec((CONV_K, wqk), lambda b, i, t: (0, k_blk0 + i)),
                  pl.BlockSpec((CONV_K, wv), lambda b, i, t: (0, v_blk0 + i)),
                  pl.BlockSpec((1, GATE_LANES, tb), lambda b, i, t: (b, 0, t)),
                  pl.BlockSpec((1, HEAD_DIM), lambda b, i, t: (0, 0))],
        out_specs=pl.BlockSpec((tb, wv), lambda b, i, t: (row(b, i, t), i)),
        out_shape=jax.ShapeDtypeStruct((batch * seq, hv * HEAD_DIM), BF16),
        scratch_shapes=[pltpu.VMEM((nh, HEAD_DIM, HEAD_DIM), F32),
                        pltpu.VMEM((8, wqk), F32),
                        pltpu.VMEM((8, wqk), F32),
                        pltpu.VMEM((8, wv), F32),
                        pltpu.VMEM((N_MASKS, DN_CHUNK, DN_CHUNK), F32),
                        pltpu.VMEM((N_MASKS, DN_CHUNK, DN_CHUNK), BF16),
                        pltpu.VMEM((nh, tb, HEAD_DIM), F32),
                        pltpu.VMEM((nh, n_chunks, 2 * DN_CHUNK, HEAD_DIM), BF16),
                        pltpu.VMEM((nh, n_chunks, 2 * DN_CHUNK, DN_CHUNK), BF16),
                        pltpu.VMEM((nh, n_chunks, 8, HEAD_DIM), F32)],
        compiler_params=_params("parallel", "parallel", "arbitrary"),
        name="deltanet",
    )(proj, proj, proj, proj, conv_wt, conv_wt, conv_wt, rows, norm_w.reshape(1, HEAD_DIM))


LOG2E = 1.4426950408889634


MXU_COLS = 256


def _reduce_rows(x, op):
    rows, lanes = x.shape
    slabs = 8
    if rows % (8 * slabs) == 0:
        x3 = x.reshape(slabs, rows // slabs, lanes)
        x = x3[0]
        for j in range(1, slabs):
            x = op(x, x3[j])
    final = jnp.max if op is jnp.maximum else jnp.sum
    return final(x, axis=0, keepdims=True)


ONES_ROWS = 16


def _fox_kernel(q_ref, k_ref, v_ref, f_ref, o_ref, vt_ref, fcol_ref, s_ref, m_ref, acc_ref, *, scale):
    qi = pl.program_id(2)
    tq = q_ref.shape[0]
    tk = tq
    seq = k_ref.shape[0]
    qt = min(MXU_COLS, tq)
    d = HEAD_DIM
    nhd = q_ref.shape[1] // d
    hd_lanes = lambda hd: slice(hd * d, (hd + 1) * d)

    @pl.when(qi == 0)
    def _():
        n_blk = seq // d
        group = next(g for g in (4, 2, 1) if n_blk % g == 0)

        def prep(bi, carry):
            for j in range(group):
                off = pl.multiple_of((bi * group + j) * d, d)
                for hd in range(nhd):
                    v_blk = v_ref[pl.ds(off, d), hd_lanes(hd)]
                    vt_ref[hd, 0:d, pl.ds(off, d)] = v_blk.astype(F32).T.astype(BF16)
                    frow = f_ref[0, hd, :, pl.ds(off, d)] * LOG2E
                    fcol_ref[hd, pl.ds(off, d), :] = jnp.broadcast_to(frow, (d, d)).T
            return carry
        lax.fori_loop(0, n_blk // group, prep, 0)
        for hd in range(nhd):
            vt_ref[hd, d:, :] = jnp.ones((ONES_ROWS, seq), BF16)

    m_ref[...] = jnp.full_like(m_ref, -1e30)
    acc_ref[...] = jnp.zeros_like(acc_ref)

    def scores(kj, slot):
        off = pl.multiple_of(kj * tk, tk)
        for hd in range(nhd):
            s_ref[slot, hd] = _dot_nt(k_ref[pl.ds(off, tk), hd_lanes(hd)], q_ref[:, hd_lanes(hd)])

    def consume(kj, slot, masked):
        off = pl.multiple_of(kj * tk, tk)
        for hd in range(nhd):
            fcol = fcol_ref[hd, pl.ds(off, tk), :]
            fcol = jnp.concatenate([fcol] * (qt // d), axis=1)
            vt = vt_ref[hd, :, pl.ds(off, tk)]
            for t in range(tq // qt):
                lanes = slice(t * qt, (t + 1) * qt)
                nk = min((t + 1) * qt, tk) if masked else tk
                s = s_ref[slot, hd, 0:nk, lanes] * (scale * LOG2E) - fcol[0:nk]
                if masked:
                    key = lax.broadcasted_iota(jnp.int32, (nk, qt), 0)
                    qry = lax.broadcasted_iota(jnp.int32, (nk, qt), 1) + t * qt
                    s = jnp.where(key <= qry, s, -1e30)
                m_prev = m_ref[hd, :, lanes]
                m_new = jnp.maximum(m_prev, _reduce_rows(s, jnp.maximum))
                p = jnp.exp2(s - m_new)
                alpha = jnp.exp2(m_prev - m_new)
                m_ref[hd, :, lanes] = m_new
                acc_ref[hd, :, lanes] = alpha * acc_ref[hd, :, lanes] + _dot(vt[:, 0:nk], p)

    scores(0, 0)

    def body(i, carry):
        kj = 2 * i
        scores(kj + 1, 1)
        consume(kj, 0, False)
        scores(kj + 2, 0)
        consume(kj + 1, 1, False)
        return carry

    lax.fori_loop(0, qi // 2, body, 0)

    @pl.when(qi % 2 == 0)
    def _():
        consume(qi, 0, True)

    @pl.when(qi % 2 == 1)
    def _():
        scores(qi, 1)
        consume(qi - 1, 0, False)
        consume(qi, 1, True)

    for hd in range(nhd):
        out = (acc_ref[hd, 0:d, :] / acc_ref[hd, d:d + 1, :]).T
        o_ref[:, hd_lanes(hd)] = out.astype(o_ref.dtype)


def _fox(proj, f_rows, batch, seq, base, hf):
    tq = _tile(seq, 512)
    nq = seq // tq
    nhd = next(n for n in (4, 2, 1) if hf % n == 0 and base % n == 0)
    w = nhd * HEAD_DIM
    b0, nblk = base // nhd, hf // nhd
    kern = functools.partial(_fox_kernel, scale=HEAD_DIM ** -0.5)
    return pl.pallas_call(
        kern,
        grid=(batch, nblk, nq),
        in_specs=[pl.BlockSpec((tq, w), lambda b, h, qi: (b * nq + qi, b0 + h)),
                  pl.BlockSpec((seq, w), lambda b, h, qi: (b, b0 + nblk + h)),
                  pl.BlockSpec((seq, w), lambda b, h, qi: (b, b0 + 2 * nblk + h)),
                  pl.BlockSpec((1, nhd, 1, seq), lambda b, h, qi: (b, h, 0, 0))],
        out_specs=pl.BlockSpec((tq, w), lambda b, h, qi: (b * nq + qi, h)),
        out_shape=jax.ShapeDtypeStruct((batch * seq, hf * HEAD_DIM), BF16),
        scratch_shapes=[pltpu.VMEM((nhd, HEAD_DIM + ONES_ROWS, seq), BF16),
                        pltpu.VMEM((nhd, seq, HEAD_DIM), F32),
                        pltpu.VMEM((2, nhd, tq, tq), F32),
                        pltpu.VMEM((nhd, 1, tq), F32),
                        pltpu.VMEM((nhd, HEAD_DIM + ONES_ROWS, tq), F32)],
        compiler_params=_params("parallel", "parallel", "arbitrary"),
        name="fox_attention",
    )(proj, proj, proj, f_rows)


def _merge_kernel(a1_ref, w1_ref, a2_ref, w2_ref, m1_ref, m2_ref, o_ref):
    y1 = _dot(a1_ref[...], w1_ref[...])
    y2 = _dot(a2_ref[...], w2_ref[...])
    g1 = _sigmoid(m1_ref[...].astype(F32))
    g2 = _sigmoid(m2_ref[...].astype(F32))
    o_ref[...] = (g1 * y1 + g2 * y2).astype(o_ref.dtype)


def _merge(o_dn, w_dn, o_fox, w_fox, layer, proj, merge_base_cols):
    m, k1 = o_dn.shape
    k2 = o_fox.shape[1]
    d = w_dn.shape[2]
    tm, tn = _tile(m, 512), _tile(d, 512)
    assert merge_base_cols % tn == 0
    mb = merge_base_cols // tn
    return pl.pallas_call(
        _merge_kernel,
        grid=(m // tm, d // tn),
        in_specs=[pl.BlockSpec((tm, k1), lambda i, j: (i, 0)),
                  pl.BlockSpec((None, k1, tn), lambda i, j: (layer, 0, j)),
                  pl.BlockSpec((tm, k2), lambda i, j: (i, 0)),
                  pl.BlockSpec((None, k2, tn), lambda i, j: (layer, 0, j)),
                  pl.BlockSpec((tm, tn), lambda i, j: (i, mb + j)),
                  pl.BlockSpec((tm, tn), lambda i, j: (i, mb + d // tn + j))],
        out_specs=pl.BlockSpec((tm, tn), lambda i, j: (i, j)),
        out_shape=jax.ShapeDtypeStruct((m, d), BF16),
        compiler_params=_params("parallel", "parallel"),
        name="branch_merge",
    )(o_dn, w_dn, o_fox, w_fox, proj, proj)


def _cast_weight_once(w_ref, wb_ref):
    @pl.when(pl.program_id(1) == 0)
    def _():
        k = w_ref.shape[0]
        for r0 in range(0, k, W_ROWS):
            r1 = min(r0 + W_ROWS, k)
            wb_ref[r0:r1, :] = w_ref[r0:r1, :].astype(BF16)


def _mm_f32_kernel(a_ref, w_ref, o_ref, wb_ref):
    _cast_weight_once(w_ref, wb_ref)
    o_ref[...] = _dot(a_ref[...], wb_ref[...])


def _out_proj(a, w, layer):
    m, k = a.shape
    d = w.shape[2]
    pref = 512 if k > 4096 else 1024
    tm, tn = _tile(m, pref), _tile(d, pref)
    return pl.pallas_call(
        _mm_f32_kernel,
        grid=(d // tn, m // tm),
        in_specs=[pl.BlockSpec((tm, k), lambda j, i: (i, 0)),
                  pl.BlockSpec((None, k, tn), lambda j, i: (layer, 0, j))],
        out_specs=pl.BlockSpec((tm, tn), lambda j, i: (i, j)),
        out_shape=jax.ShapeDtypeStruct((m, d), F32),
        scratch_shapes=[pltpu.VMEM((k, tn), BF16)],
        compiler_params=_params("parallel", "arbitrary"),
        name="out_proj",
    )(a, w)


def _res_norm_kernel(y_ref, x_ref, g_ref, gate_ref, *rest, emit_h):
    x_new = x_ref[...] + gate_ref[0] * (_rms(y_ref[...]) * g_ref[...])
    if emit_h:
        g2_ref, sc_ref, sh_ref, o_ref, h_ref = rest
        o_ref[...] = x_new
        h_ref[...] = ((_rms(x_new) * g2_ref[...]) * (1.0 + sc_ref[0]) + sh_ref[0]).astype(h_ref.dtype)
    else:
        (o_ref,) = rest
        o_ref[...] = x_new


def _residual_norm(y, x2, gain, mod3, gate_idx, seq, nxt=None):
    m, d = x2.shape
    tm = _tile(seq, 256)
    per_b = seq // tm
    row = pl.BlockSpec((tm, d), lambda i: (i, 0))
    vec = pl.BlockSpec((1, d), lambda i: (0, 0))
    modv = lambda idx: pl.BlockSpec((1, 1, d), lambda i: ((i // per_b) * 6 + idx, 0, 0))
    in_specs = [row, row, vec, modv(gate_idx)]
    args = [y, x2, gain.reshape(1, d), mod3]
    out_specs, out_shape = [row], [jax.ShapeDtypeStruct((m, d), F32)]
    if nxt is not None:
        gain2, mod3n, sc_idx, sh_idx = nxt
        in_specs += [vec, modv(sc_idx), modv(sh_idx)]
        args += [gain2.reshape(1, d), mod3n, mod3n]
        out_specs.append(row)
        out_shape.append(jax.ShapeDtypeStruct((m, d), BF16))
    out = pl.pallas_call(
        functools.partial(_res_norm_kernel, emit_h=nxt is not None),
        grid=(m // tm,),
        in_specs=in_specs,
        out_specs=out_specs,
        out_shape=out_shape,
        compiler_params=_params("parallel"),
        name="residual_norm",
    )(*args)
    return (out[0], out[1]) if nxt is not None else (out[0], None)


def _glu_kernel(a_ref, wg_ref, wu_ref, o_ref, wgb_ref, wub_ref):
    _cast_weight_once(wg_ref, wgb_ref)
    _cast_weight_once(wu_ref, wub_ref)
    a = a_ref[...]
    g = _dot(a, wgb_ref[...])
    u = _dot(a, wub_ref[...])
    o_ref[...] = (g * _sigmoid(g) * u).astype(o_ref.dtype)


def _glu(a, wg, wu, layer):
    m, k = a.shape
    n = wg.shape[2]
    tm, tn = _tile(m, 1024), _tile(n, 512)
    return pl.pallas_call(
        _glu_kernel,
        grid=(n // tn, m // tm),
        in_specs=[pl.BlockSpec((tm, k), lambda j, i: (i, 0)),
                  pl.BlockSpec((None, k, tn), lambda j, i: (layer, 0, j)),
                  pl.BlockSpec((None, k, tn), lambda j, i: (layer, 0, j))],
        out_specs=pl.BlockSpec((tm, tn), lambda j, i: (i, j)),
        out_shape=jax.ShapeDtypeStruct((m, n), BF16),
        scratch_shapes=[pltpu.VMEM((k, tn), BF16), pltpu.VMEM((k, tn), BF16)],
        compiler_params=_params("parallel", "arbitrary"),
        name="swiglu_up",
    )(a, wg, wu)


def kernel(x, c, w_ada, b_ada, norm_gains, w_in, dn_conv, dn_a_log, dn_dt_bias, dn_norm_w, fox_f_bias,
           w_branch_dn, w_branch_fox, w_out, w_gate, w_up, w_down):
    batch, seq, d = x.shape
    depth = w_ada.shape[0]
    hv = dn_a_log.shape[1]
    hf = fox_f_bias.shape[1]
    v_dim = hv * HEAD_DIM
    conv_dim = dn_conv.shape[1]
    qk_dim = (conv_dim - v_dim) // 2
    hqk = qk_dim // HEAD_DIM
    fox_dim = hf * HEAD_DIM
    assert 2 * hv + hf <= GATE_LANES

    o_z = conv_dim
    o_b = o_z + v_dim
    o_a = o_b + hv
    o_fq = o_a + hv
    o_ff = o_fq + 3 * fox_dim
    o_mg = o_ff + hf
    fox_base = (conv_dim + v_dim) // HEAD_DIM
    merge_base = conv_dim + v_dim + 3 * fox_dim

    mod = _modulation(c, w_ada, b_ada)
    x2 = x.reshape(batch * seq, d)
    pad = GATE_LANES - (2 * hv + hf)
    zpad = jnp.zeros((pad,), F32)

    main_cols = ((0, o_b), (o_fq, 3 * fox_dim), (o_mg, w_in.shape[2] - o_mg))
    w_in_t = jnp.swapaxes(w_in, 1, 2)
    w_dn_all, w_fox_all = w_branch_dn.astype(BF16), w_branch_fox.astype(BF16)
    mod3s = [mod[l].reshape(batch * 6, 1, d) for l in range(depth)]

    h = _norm_mod(x2, norm_gains[0, 0], mod3s[0], 1, 0, seq)
    for l in range(depth):
        mod3 = mod3s[l]
        bias = jnp.concatenate([jnp.zeros((hv,), F32), dn_dt_bias[l], fox_f_bias[l], zpad])
        mult = jnp.concatenate([jnp.ones((hv,), F32), -jnp.exp(dn_a_log[l]), jnp.ones((hf,), F32), zpad])
        pcol = jnp.stack([bias, mult], axis=0)

        proj = _in_proj(h, w_in_t, l, main_cols)
        rows = _gates(h, w_in_t, l, o_b, o_ff, pcol, batch, seq, hv, hf)
        o_dn = _deltanet(proj, dn_conv[l].T, rows, dn_norm_w[l], batch, seq, hqk, hv)
        f_rows = rows[:, 2 * hv:2 * hv + hf, :].reshape(batch, hf, 1, seq)
        o_fox = _fox(proj, f_rows, batch, seq, fox_base, hf)
        ymix = _merge(o_dn, w_dn_all, o_fox, w_fox_all, l, proj, merge_base)
        y = _out_proj(ymix, w_out, l)
        x2, h = _residual_norm(y, x2, norm_gains[l, 1], mod3, 2, seq,
                               nxt=(norm_gains[l, 2], mod3, 4, 3))

        gu = _glu(h, w_gate, w_up, l)
        y = _out_proj(gu, w_down, l)
        nxt = (norm_gains[l + 1, 0], mod3s[l + 1], 1, 0) if l + 1 < depth else None
        x2, h = _residual_norm(y, x2, norm_gains[l, 3], mod3, 5, seq, nxt=nxt)

    return x2.reshape(batch, seq, d)
```

```python
import functools

import jax
import jax.numpy as jnp
from jax import lax
from jax.experimental import pallas as pl
from jax.experimental.pallas import tpu as pltpu

EPS = 1e-6
HEAD_DIM = 128
CONV_K = 4
DN_CHUNK = 128
GATE_LANES = 128
VMEM_LIMIT_BYTES = 48 * 1024 * 1024

F32 = jnp.float32
BF16 = jnp.bfloat16


def _params(*sem):
    return pltpu.CompilerParams(dimension_semantics=sem, vmem_limit_bytes=VMEM_LIMIT_BYTES)


def _tile(dim, pref):
    if dim <= pref:
        return dim
    t = pref - pref % HEAD_DIM
    while dim % t:
        t -= HEAD_DIM
    assert t > 0, (dim, pref)
    return t


def _dot(a, b):
    return jnp.dot(a.astype(BF16), b.astype(BF16), preferred_element_type=F32)


def _dot_nt(a, b):
    return lax.dot_general(a.astype(BF16), b.astype(BF16), (((1,), (1,)), ((), ())),
                           preferred_element_type=F32)


def _sigmoid(x):
    return 0.5 * jnp.tanh(0.5 * x) + 0.5


def _rms(x):
    return x * lax.rsqrt(jnp.mean(x * x, axis=-1, keepdims=True) + EPS)


def _mod_kernel(c_ref, w_ref, b_ref, o_ref):
    c = c_ref[...]
    cond = c * _sigmoid(c)
    o_ref[0] = _dot(cond, w_ref[0]) + b_ref[0]


def _modulation(c, w_ada, b_ada):
    depth, d, n = w_ada.shape
    b = c.shape[0]
    rows = 16
    c_pad = jnp.zeros((rows, d), F32).at[:b].set(c)
    tn = _tile(n, 1024)
    out = pl.pallas_call(
        _mod_kernel,
        grid=(depth, n // tn),
        in_specs=[pl.BlockSpec((rows, d), lambda l, j: (0, 0)),
                  pl.BlockSpec((1, d, tn), lambda l, j: (l, 0, j)),
                  pl.BlockSpec((1, 1, tn), lambda l, j: (l, 0, j))],
        out_specs=pl.BlockSpec((1, rows, tn), lambda l, j: (l, 0, j)),
        out_shape=jax.ShapeDtypeStruct((depth, rows, n), F32),
        compiler_params=_params("parallel", "parallel"),
        name="adaln_mod",
    )(c_pad, w_ada, b_ada.reshape(depth, 1, n))
    return out[:, :b].reshape(depth, b, 6, d)


def _norm_mod_kernel(x_ref, g_ref, sc_ref, sh_ref, o_ref):
    y = _rms(x_ref[...]) * g_ref[...]
    o_ref[...] = (y * (1.0 + sc_ref[0]) + sh_ref[0]).astype(o_ref.dtype)


def _norm_mod(x2, gain, mod3, sc_idx, sh_idx, seq):
    m, d = x2.shape
    tm = _tile(seq, 512)
    per_b = seq // tm
    return pl.pallas_call(
        _norm_mod_kernel,
        grid=(m // tm,),
        in_specs=[pl.BlockSpec((tm, d), lambda i: (i, 0)),
                  pl.BlockSpec((1, d), lambda i: (0, 0)),
                  pl.BlockSpec((1, 1, d), lambda i: ((i // per_b) * 6 + sc_idx, 0, 0)),
                  pl.BlockSpec((1, 1, d), lambda i: ((i // per_b) * 6 + sh_idx, 0, 0))],
        out_specs=pl.BlockSpec((tm, d), lambda i: (i, 0)),
        out_shape=jax.ShapeDtypeStruct((m, d), BF16),
        compiler_params=_params("parallel"),
        name="norm_mod",
    )(x2, gain.reshape(1, d), mod3, mod3)


W_ROWS = 256


def _in_proj_kernel(a_ref, wa_ref, wb_ref, o_ref, w_ref, *, segments):
    j = pl.program_id(0)
    tn = wa_ref.shape[0]

    @pl.when(pl.program_id(1) == 0)
    def _():
        for lo, hi, shift in segments:
            @pl.when((j >= lo) & (j < hi))
            def _():
                for r0 in range(0, tn - shift, W_ROWS):
                    r1 = min(r0 + W_ROWS, tn - shift)
                    w_ref[r0:r1, :] = wa_ref[r0 + shift:r1 + shift, :].astype(BF16)
                if shift:
                    w_ref[tn - shift:, :] = wb_ref[:shift, :].astype(BF16)

    o_ref[...] = _dot_nt(a_ref[...], w_ref[...]).astype(o_ref.dtype)


def _in_proj(a, w_in_t, layer, seg_cols):
    m, k = a.shape
    n_out = sum(w for _, w in seg_cols)
    tn = 1024
    while any(w % tn for _, w in seg_cols):
        tn //= 2
    assert tn >= HEAD_DIM
    tm = _tile(m, 1024)
    segments, out0 = [], 0
    for src0, width in seg_cols:
        shift = src0 - out0
        assert 0 <= shift < HEAD_DIM
        segments.append((out0 // tn, (out0 + width) // tn, shift))
        out0 += width
    kern = functools.partial(_in_proj_kernel, segments=tuple(segments))
    nb = tn // HEAD_DIM
    last_blk = (w_in_t.shape[1] - 1) // HEAD_DIM
    return pl.pallas_call(
        kern,
        grid=(n_out // tn, m // tm),
        in_specs=[pl.BlockSpec((tm, k), lambda j, i: (i, 0)),
                  pl.BlockSpec((None, tn, k), lambda j, i: (layer, j, 0)),
                  pl.BlockSpec((None, HEAD_DIM, k),
                               lambda j, i: (layer, jnp.minimum((j + 1) * nb, last_blk), 0))],
        out_specs=pl.BlockSpec((tm, tn), lambda j, i: (i, j)),
        out_shape=jax.ShapeDtypeStruct((m, n_out), BF16),
        scratch_shapes=[pltpu.VMEM((tn, k), BF16)],
        compiler_params=_params("parallel", "arbitrary"),
        name="in_proj",
    )(a, w_in_t, w_in_t)


def _split3(x):
    hi = x.astype(BF16)
    r1 = x - hi.astype(F32)
    mid = r1.astype(BF16)
    lo = (r1 - mid.astype(F32)).astype(BF16)
    return hi, mid, lo


def _gate_act(x, bias, mult, idx, hv, hf):
    xb = x + bias
    e = jnp.exp(-jnp.abs(xb))
    l1p = jnp.log(1.0 + e)
    sig = jnp.where(xb >= 0, 1.0, e) / (1.0 + e)
    softplus = jnp.maximum(xb, 0.0) + l1p
    logsig = jnp.minimum(xb, 0.0) - l1p
    return jnp.where(idx < hv, sig,
                     jnp.where(idx < 2 * hv, mult * softplus,
                               jnp.where(idx < 2 * hv + hf, logsig, 0.0)))


def _gates_kernel(h_ref, wa_ref, wb_ref, pcol_ref, rows_ref, carry_ref, *, hv, hf, chunk):
    t = pl.program_id(1)
    tm = h_ref.shape[0]

    @pl.when(t == 0)
    def _():
        carry_ref[...] = jnp.zeros_like(carry_ref)

    row_w = lax.broadcasted_iota(jnp.int32, wa_ref.shape, 0)
    w = jnp.where(row_w < 2 * hv, wa_ref[...], jnp.where(row_w < 2 * hv + hf, wb_ref[...], 0.0))
    g_cols = _dot_nt(h_ref[...], w)

    lane = lax.broadcasted_iota(jnp.int32, (tm, GATE_LANES), 1)
    val_c = _gate_act(g_cols, pcol_ref[0:1, :], pcol_ref[1:2, :], lane, hv, hf)
    sub = lax.broadcasted_iota(jnp.int32, (GATE_LANES, tm), 0)
    val_r = val_c.T

    r = lax.broadcasted_iota(jnp.int32, (tm, tm), 0)
    c = lax.broadcasted_iota(jnp.int32, (tm, tm), 1)
    sh = chunk.bit_length() - 1
    same = (r >> sh) == (c >> sh)
    triu_blk = jnp.where((r <= c) & same, 1.0, 0.0).astype(BF16)
    triu_all = jnp.where(r <= c, 1.0, 0.0).astype(BF16)

    pieces_r = _split3(val_r)
    cum_r_blk = sum(jnp.dot(p, triu_blk, preferred_element_type=F32) for p in pieces_r)
    cum_r_all = sum(jnp.dot(p, triu_all, preferred_element_type=F32) for p in pieces_r)
    cum_r_all = cum_r_all + carry_ref[:, 0:1]

    is_decay_r = (sub >= hv) & (sub < 2 * hv)
    is_forget_r = (sub >= 2 * hv) & (sub < 2 * hv + hf)
    rows = jnp.where(is_decay_r, cum_r_blk, jnp.where(is_forget_r, cum_r_all, val_r))
    rows_ref[0] = rows
    carry_ref[...] = jnp.broadcast_to(cum_r_all[:, tm - 1:tm], carry_ref.shape)


def _gates(h2, w_in, layer, col_ba, col_f, pcol, batch, seq, hv, hf):
    m, d = h2.shape
    tm = _tile(seq, 512)
    per_b = seq // tm
    assert col_ba % GATE_LANES == 0 and col_f % GATE_LANES == 2 * hv
    blk_ba, blk_f = col_ba // GATE_LANES, col_f // GATE_LANES
    kern = functools.partial(_gates_kernel, hv=hv, hf=hf, chunk=DN_CHUNK)
    return pl.pallas_call(
        kern,
        grid=(batch, per_b),
        in_specs=[pl.BlockSpec((tm, d), lambda b, t: (b * per_b + t, 0)),
                  pl.BlockSpec((None, GATE_LANES, d), lambda b, t: (layer, blk_ba, 0)),
                  pl.BlockSpec((None, GATE_LANES, d), lambda b, t: (layer, blk_f, 0)),
                  pl.BlockSpec((2, GATE_LANES), lambda b, t: (0, 0))],
        out_specs=pl.BlockSpec((1, GATE_LANES, tm), lambda b, t: (b, 0, t)),
        out_shape=jax.ShapeDtypeStruct((batch, GATE_LANES, seq), F32),
        scratch_shapes=[pltpu.VMEM((GATE_LANES, GATE_LANES), F32)],
        compiler_params=_params("parallel", "arbitrary"),
        name="gates",
    )(h2, w_in, w_in, pcol)


def _causal_conv(u, tail, w):
    out = u * w[CONV_K - 1:CONV_K, :]
    row8 = lax.broadcasted_iota(jnp.int32, tail.shape, 0)
    for s in range(1, CONV_K):
        rolled = pltpu.roll(u, s, 0)
        head = jnp.where(row8 < s, pltpu.roll(tail, s, 0), rolled[0:8])
        shifted = jnp.concatenate([head, rolled[8:]], axis=0)
        out = out + shifted * w[CONV_K - 1 - s:CONV_K - s, :]
    return out


INV_LEVELS = (DN_CHUNK // 8).bit_length() - 1
MASK_DIAG8, MASK_EYE, MASK_STRICT = 0, INV_LEVELS + 1, INV_LEVELS + 2
N_MASKS = INV_LEVELS + 3


def _inverse_masks(n):
    r = lax.broadcasted_iota(jnp.int32, (n, n), 0)
    c = lax.broadcasted_iota(jnp.int32, (n, n), 1)
    masks = [(r >> 3) == (c >> 3)]
    for sh in range(3, 3 + INV_LEVELS):
        masks.append(((r >> (sh + 1)) == (c >> (sh + 1))) & ((r >> sh) == (c >> sh) + 1))
    masks += [r == c, r > c]
    return [jnp.where(m, 1.0, 0.0) for m in masks]


def _unit_lower_inverses(lms, mask_ref, maskb_ref):
    n0s = [-(lm * mask_ref[MASK_DIAG8]) for lm in lms]
    n2s = [_dot(n0, n0) for n0 in n0s]
    n4s = [_dot(n2, n2) for n2 in n2s]
    ps = [mask_ref[MASK_EYE] + n0 for n0 in n0s]
    ps = [p + _dot(p, n2) for p, n2 in zip(ps, n2s)]
    ps = [p + _dot(p, n4) for p, n4 in zip(ps, n4s)]
    lms_b = [lm.astype(BF16) for lm in lms]
    for level in range(1, INV_LEVELS + 1):
        xs = [_dot(lm_b * maskb_ref[level], p) for lm_b, p in zip(lms_b, ps)]
        ps = [p - _dot(p, x) for p, x in zip(ps, xs)]
    return ps


def _deltanet_kernel(q_ref, k_ref, v_ref, z_ref, wq_ref, wk_ref, wv_ref, rows_ref, nw_ref,
                     o_ref, s_ref, qt_ref, kt_ref, vt_ref, mask_ref, maskb_ref, u_ref, lhs1_ref, lhs2_ref,
                     egl_ref,
                     *, hv, rep, groups, unroll, scan_unroll):
    i = pl.program_id(1)
    tb = pl.program_id(2)
    rows_blk = q_ref.shape[0]
    n_chunks = rows_blk // DN_CHUNK
    c_ = DN_CHUNK
    d = HEAD_DIM
    nh = groups * rep

    @pl.when(tb == 0)
    def _():
        s_ref[...] = jnp.zeros_like(s_ref)
        qt_ref[...] = jnp.zeros_like(qt_ref)
        kt_ref[...] = jnp.zeros_like(kt_ref)
        vt_ref[...] = jnp.zeros_like(vt_ref)
        for j, m in enumerate(_inverse_masks(c_)):
            mask_ref[j] = m
            maskb_ref[j] = m.astype(BF16)

    r = lax.broadcasted_iota(jnp.int32, (c_, c_), 0)
    c = lax.broadcasted_iota(jnp.int32, (c_, c_), 1)
    nw = nw_ref[...]

    def tail_of(ref, tail_ref, ci, off):
        prev = pl.multiple_of(jnp.maximum(off - 16, 0), 16)
        inside = ref[pl.ds(prev, 16), :].astype(F32)[8:16]
        return jnp.where(ci == 0, tail_ref[...], inside)

    def prepare_body(j, carry):
        chains = []
        for uu in range(unroll):
            ci = j * unroll + uu
            off = pl.multiple_of(ci * c_, c_)
            sl = pl.ds(off, c_)
            q_raw = q_ref[sl, :].astype(F32)
            k_raw = k_ref[sl, :].astype(F32)
            v_raw = v_ref[sl, :].astype(F32)
            qc = _causal_conv(q_raw, tail_of(q_ref, qt_ref, ci, off), wq_ref[...])
            kc = _causal_conv(k_raw, tail_of(k_ref, kt_ref, ci, off), wk_ref[...])
            vc = _causal_conv(v_raw, tail_of(v_ref, vt_ref, ci, off), wv_ref[...])
            qc = qc * _sigmoid(qc)
            kc = kc * _sigmoid(kc)
            vc = vc * _sigmoid(vc)
            row_b = pl.multiple_of(((i * nh) // 8) * 8, 8)
            row_g = pl.multiple_of(((hv + i * nh) // 8) * 8, 8)
            rows8_b = rows_ref[0, pl.ds(row_b, 8), sl]
            rows8_g = rows_ref[0, pl.ds(row_g, 8), sl]
            sub8 = lax.broadcasted_iota(jnp.int32, (8, c_), 0)
            for g in range(groups):
                qg_ = qc[:, g * d:(g + 1) * d]
                kg_ = kc[:, g * d:(g + 1) * d]
                qn = qg_ * (lax.rsqrt(jnp.sum(qg_ * qg_, axis=-1, keepdims=True) + EPS) * (d ** -0.5))
                kn = kg_ * lax.rsqrt(jnp.sum(kg_ * kg_, axis=-1, keepdims=True) + EPS)
                kk = _dot_nt(kn, kn)
                qk = _dot_nt(qn, kn)
                for hh in range(rep):
                    hl = g * rep + hh
                    head = i * nh + hl
                    beta_r = jnp.sum(jnp.where(sub8 == head % 8, rows8_b, 0.0), axis=0, keepdims=True)
                    gc_r = jnp.sum(jnp.where(sub8 == (hv + head) % 8, rows8_g, 0.0), axis=0, keepdims=True)
                    beta_c = jnp.broadcast_to(beta_r, (c_, c_)).T
                    gc_c = jnp.broadcast_to(gc_r, (c_, c_)).T
                    g_last = gc_r[:, c_ - 1:c_]
                    decay = jnp.exp(jnp.where(r >= c, gc_c - gc_r, -1e30))
                    lm = (beta_c * kk) * (decay * mask_ref[MASK_STRICT])
                    eg = jnp.exp(gc_c)
                    v_h = vc[:, hl * d:(hl + 1) * d]
                    rhs = jnp.concatenate([v_h * beta_c, kn * (beta_c * eg)], axis=1).astype(BF16)
                    kd = kn * jnp.exp(g_last - gc_c)
                    lhs2_ref[hl, ci] = jnp.concatenate([qk * decay, kd.T], axis=0).astype(BF16)
                    egl_ref[hl, ci] = jnp.broadcast_to(jnp.exp(g_last), (8, d))
                    chains.append((hl, ci, sl, lm, rhs, (qn * eg).astype(BF16)))
        tinvs = _unit_lower_inverses([ch[3] for ch in chains], mask_ref, maskb_ref)
        uws = [_dot(tinv, ch[4]) for tinv, ch in zip(tinvs, chains)]
        for uw, (hl, ci, sl, _, _, qg) in zip(uws, chains):
            u_ref[hl, sl, :] = uw[:, :d]
            lhs1_ref[hl, ci] = jnp.concatenate([uw[:, d:].astype(BF16), qg], axis=0)
        return carry

    lax.fori_loop(0, n_chunks // unroll, prepare_body, 0)

    heads = range(nh)

    def scan_chunk(ci):
        off = pl.multiple_of(ci * c_, c_)
        sl = pl.ds(off, c_)
        states = [s_ref[hl] for hl in heads]
        ws_qs = [_dot(lhs1_ref[hl, ci], states[hl]) for hl in heads]
        v_new = [u_ref[hl, sl, :] - ws_qs[hl][:c_] for hl in heads]
        av_kv = [_dot(lhs2_ref[hl, ci], v_new[hl]) for hl in heads]
        for hl in heads:
            s_ref[hl] = states[hl] * egl_ref[hl, ci, 0:1, :] + av_kv[hl][c_:]
        for hl in heads:
            o = ws_qs[hl][c_:] + av_kv[hl][:c_]
            z = z_ref[sl, hl * d:(hl + 1) * d].astype(F32)
            out = _rms(o) * nw * (z * _sigmoid(z))
            o_ref[sl, hl * d:(hl + 1) * d] = out.astype(o_ref.dtype)

    def scan_body(j, carry):
        for uu in range(scan_unroll):
            scan_chunk(j * scan_unroll + uu)
        return carry

    lax.fori_loop(0, n_chunks // scan_unroll, scan_body, 0)

    qt_ref[...] = q_ref[rows_blk - 16:, :].astype(F32)[8:16]
    kt_ref[...] = k_ref[rows_blk - 16:, :].astype(F32)[8:16]
    vt_ref[...] = v_ref[rows_blk - 16:, :].astype(F32)[8:16]


def _deltanet(proj, conv_wt, rows, norm_w, batch, seq, hqk, hv):
    rep = hv // hqk
    assert hv == rep * hqk and DN_CHUNK == HEAD_DIM
    groups = next(g for g in (4, 2, 1) if hqk % g == 0 and 8 % (g * rep) == 0)
    nh = groups * rep
    assert 8 % nh == 0 and hv % nh == 0 and (2 * hqk) % nh == 0
    tb = _tile(seq, 1024)
    per_b = seq // tb
    wqk, wv = groups * HEAD_DIM, nh * HEAD_DIM
    nqk_blk, nv_blk = hqk // groups, hv // nh
    k_blk0 = nqk_blk
    v_blk0 = (2 * hqk) // nh
    z_blk0 = v_blk0 + nv_blk
    row = lambda b, i, t: b * per_b + t
    n_chunks = tb // DN_CHUNK
    unroll = next(u for u in (4, 2, 1) if n_chunks % u == 0 and u * nh <= 16)
    scan_unroll = next(u for u in (4, 2, 1) if n_chunks % u == 0)
    kern = functools.partial(_deltanet_kernel, hv=hv, rep=rep, groups=groups, unroll=unroll,
                             scan_unroll=scan_unroll)
    return pl.pallas_call(
        kern,
        grid=(batch, nqk_blk, per_b),
        in_specs=[pl.BlockSpec((tb, wqk), lambda b, i, t: (row(b, i, t), i)),
                  pl.BlockSpec((tb, wqk), lambda b, i, t: (row(b, i, t), k_blk0 + i)),
                  pl.BlockSpec((tb, wv), lambda b, i, t: (row(b, i, t), v_blk0 + i)),
                  pl.BlockSpec((tb, wv), lambda b, i, t: (row(b, i, t), z_blk0 + i)),
                  pl.BlockSpec((CONV_K, wqk), lambda b, i, t: (0, i)),
                  pl.BlockSpec((CONV_K, wqk), lambda b, i, t: (0, k_blk0 + i)),
                  pl.BlockSpec((CONV_K, wv), lambda b, i, t: (0, v_blk0 + i)),
                  pl.BlockSpec((1, GATE_LANES, tb), lambda b, i, t: (b, 0, t)),
                  pl.BlockSpec((1, HEAD_DIM), lambda b, i, t: (0, 0))],
        out_specs=pl.BlockSpec((tb, wv), lambda b, i, t: (row(b, i, t), i)),
        out_shape=jax.ShapeDtypeStruct((batch * seq, hv * HEAD_DIM), BF16),
        scratch_shapes=[pltpu.VMEM((nh, HEAD_DIM, HEAD_DIM), F32),
                        pltpu.VMEM((8, wqk), F32),
                        pltpu.VMEM((8, wqk), F32),
                        pltpu.VMEM((8, wv), F32),
                        pltpu.VMEM((N_MASKS, DN_CHUNK, DN_CHUNK), F32),
                        pltpu.VMEM((N_MASKS, DN_CHUNK, DN_CHUNK), BF16),
                        pltpu.VMEM((nh, tb, HEAD_DIM), F32),
                        pltpu.VMEM((nh, n_chunks, 2 * DN_CHUNK, HEAD_DIM), BF16),
                        pltpu.VMEM((nh, n_chunks, 2 * DN_CHUNK, DN_CHUNK), BF16),
                        pltpu.VMEM((nh, n_chunks, 8, HEAD_DIM), F32)],
        compiler_params=_params("parallel", "parallel", "arbitrary"),
        name="deltanet",
    )(proj, proj, proj, proj, conv_wt, conv_wt, conv_wt, rows, norm_w.reshape(1, HEAD_DIM))


LOG2E = 1.4426950408889634
MXU_COLS = 256
ONES_ROWS = 16


def _reduce_rows(x, op):
    rows, lanes = x.shape
    slabs = 8
    if rows % (8 * slabs) == 0:
        x3 = x.reshape(slabs, rows // slabs, lanes)
        x = x3[0]
        for j in range(1, slabs):
            x = op(x, x3[j])
    final = jnp.max if op is jnp.maximum else jnp.sum
    return final(x, axis=0, keepdims=True)


def _fox_kernel(q_ref, k_ref, v_ref, f_ref, o_ref, vt_ref, fcol_ref, s_ref, m_ref, acc_ref, *, scale):
    qi = pl.program_id(2)
    tq = q_ref.shape[0]
    tk = tq
    seq = k_ref.shape[0]
    qt = min(MXU_COLS, tq)
    d = HEAD_DIM
    nhd = q_ref.shape[1] // d
    hd_lanes = lambda hd: slice(hd * d, (hd + 1) * d)

    @pl.when(qi == 0)
    def _():
        n_blk = seq // d
        group = next(g for g in (4, 2, 1) if n_blk % g == 0)

        def prep(bi, carry):
            for j in range(group):
                off = pl.multiple_of((bi * group + j) * d, d)
                for hd in range(nhd):
                    v_blk = v_ref[pl.ds(off, d), hd_lanes(hd)]
                    vt_ref[hd, 0:d, pl.ds(off, d)] = v_blk.astype(F32).T.astype(BF16)
                    frow = f_ref[0, hd, :, pl.ds(off, d)] * LOG2E
                    fcol_ref[hd, pl.ds(off, d), :] = jnp.broadcast_to(frow, (d, d)).T
            return carry
        lax.fori_loop(0, n_blk // group, prep, 0)
        for hd in range(nhd):
            vt_ref[hd, d:, :] = jnp.ones((ONES_ROWS, seq), BF16)

    m_ref[...] = jnp.full_like(m_ref, -1e30)
    acc_ref[...] = jnp.zeros_like(acc_ref)

    def scores(kj, slot):
        off = pl.multiple_of(kj * tk, tk)
        for hd in range(nhd):
            s_ref[slot, hd] = _dot_nt(k_ref[pl.ds(off, tk), hd_lanes(hd)], q_ref[:, hd_lanes(hd)])

    def consume(kj, slot, masked):
        off = pl.multiple_of(kj * tk, tk)
        for hd in range(nhd):
            fcol = fcol_ref[hd, pl.ds(off, tk), :]
            fcol = jnp.concatenate([fcol] * (qt // d), axis=1)
            vt = vt_ref[hd, :, pl.ds(off, tk)]
            for t in range(tq // qt):
                lanes = slice(t * qt, (t + 1) * qt)
                nk = min((t + 1) * qt, tk) if masked else tk
                s = s_ref[slot, hd, 0:nk, lanes] * (scale * LOG2E) - fcol[0:nk]
                if masked:
                    key = lax.broadcasted_iota(jnp.int32, (nk, qt), 0)
                    qry = lax.broadcasted_iota(jnp.int32, (nk, qt), 1) + t * qt
                    s = jnp.where(key <= qry, s, -1e30)
                m_prev = m_ref[hd, :, lanes]
                m_new = jnp.maximum(m_prev, _reduce_rows(s, jnp.maximum))
                p = jnp.exp2(s - m_new)
                alpha = jnp.exp2(m_prev - m_new)
                m_ref[hd, :, lanes] = m_new
                acc_ref[hd, :, lanes] = alpha * acc_ref[hd, :, lanes] + _dot(vt[:, 0:nk], p)

    scores(0, 0)

    def body(i, carry):
        kj = 2 * i
        scores(kj + 1, 1)
        consume(kj, 0, False)
        scores(kj + 2, 0)
        consume(kj + 1, 1, False)
        return carry

    lax.fori_loop(0, qi // 2, body, 0)

    @pl.when(qi % 2 == 0)
    def _():
        consume(qi, 0, True)

    @pl.when(qi % 2 == 1)
    def _():
        scores(qi, 1)
        consume(qi - 1, 0, False)
        consume(qi, 1, True)

    for hd in range(nhd):
        out = (acc_ref[hd, 0:d, :] / acc_ref[hd, d:d + 1, :]).T
        o_ref[:, hd_lanes(hd)] = out.astype(o_ref.dtype)


def _fox(proj, f_rows, batch, seq, base, hf):
    tq = _tile(seq, 512)
    nq = seq // tq
    nhd = next(n for n in (4, 2, 1) if hf % n == 0 and base % n == 0)
    w = nhd * HEAD_DIM
    b0, nblk = base // nhd, hf // nhd
    kern = functools.partial(_fox_kernel, scale=HEAD_DIM ** -0.5)
    return pl.pallas_call(
        kern,
        grid=(batch, nblk, nq),
        in_specs=[pl.BlockSpec((tq, w), lambda b, h, qi: (b * nq + qi, b0 + h)),
                  pl.BlockSpec((seq, w), lambda b, h, qi: (b, b0 + nblk + h)),
                  pl.BlockSpec((seq, w), lambda b, h, qi: (b, b0 + 2 * nblk + h)),
                  pl.BlockSpec((1, nhd, 1, seq), lambda b, h, qi: (b, h, 0, 0))],
        out_specs=pl.BlockSpec((tq, w), lambda b, h, qi: (b * nq + qi, h)),
        out_shape=jax.ShapeDtypeStruct((batch * seq, hf * HEAD_DIM), BF16),
        scratch_shapes=[pltpu.VMEM((nhd, HEAD_DIM + ONES_ROWS, seq), BF16),
                        pltpu.VMEM((nhd, seq, HEAD_DIM), F32),
                        pltpu.VMEM((2, nhd, tq, tq), F32),
                        pltpu.VMEM((nhd, 1, tq), F32),
                        pltpu.VMEM((nhd, HEAD_DIM + ONES_ROWS, tq), F32)],
        compiler_params=_params("parallel", "parallel", "arbitrary"),
        name="fox_attention",
    )(proj, proj, proj, f_rows)


def _merge_kernel(a1_ref, w1_ref, a2_ref, w2_ref, m1_ref, m2_ref, o_ref):
    y1 = _dot(a1_ref[...], w1_ref[...])
    y2 = _dot(a2_ref[...], w2_ref[...])
    g1 = _sigmoid(m1_ref[...].astype(F32))
    g2 = _sigmoid(m2_ref[...].astype(F32))
    o_ref[...] = (g1 * y1 + g2 * y2).astype(o_ref.dtype)


def _merge(o_dn, w_dn, o_fox, w_fox, layer, proj, merge_base_cols):
    m, k1 = o_dn.shape
    k2 = o_fox.shape[1]
    d = w_dn.shape[2]
    tm, tn = _tile(m, 512), _tile(d, 512)
    assert merge_base_cols % tn == 0
    mb = merge_base_cols // tn
    return pl.pallas_call(
        _merge_kernel,
        grid=(m // tm, d // tn),
        in_specs=[pl.BlockSpec((tm, k1), lambda i, j: (i, 0)),
                  pl.BlockSpec((None, k1, tn), lambda i, j: (layer, 0, j)),
                  pl.BlockSpec((tm, k2), lambda i, j: (i, 0)),
                  pl.BlockSpec((None, k2, tn), lambda i, j: (layer, 0, j)),
                  pl.BlockSpec((tm, tn), lambda i, j: (i, mb + j)),
                  pl.BlockSpec((tm, tn), lambda i, j: (i, mb + d // tn + j))],
        out_specs=pl.BlockSpec((tm, tn), lambda i, j: (i, j)),
        out_shape=jax.ShapeDtypeStruct((m, d), BF16),
        compiler_params=_params("parallel", "parallel"),
        name="branch_merge",
    )(o_dn, w_dn, o_fox, w_fox, proj, proj)


def _cast_weight_once(w_ref, wb_ref):
    @pl.when(pl.program_id(1) == 0)
    def _():
        k = w_ref.shape[0]
        for r0 in range(0, k, W_ROWS):
            r1 = min(r0 + W_ROWS, k)
            wb_ref[r0:r1, :] = w_ref[r0:r1, :].astype(BF16)


def _mm_f32_kernel(a_ref, w_ref, o_ref, wb_ref):
    _cast_weight_once(w_ref, wb_ref)
    o_ref[...] = _dot(a_ref[...], wb_ref[...])


def _out_proj(a, w, layer):
    m, k = a.shape
    d = w.shape[2]
    pref = 512 if k > 4096 else 1024
    tm, tn = _tile(m, pref), _tile(d, pref)
    return pl.pallas_call(
        _mm_f32_kernel,
        grid=(d // tn, m // tm),
        in_specs=[pl.BlockSpec((tm, k), lambda j, i: (i, 0)),
                  pl.BlockSpec((None, k, tn), lambda j, i: (layer, 0, j))],
        out_specs=pl.BlockSpec((tm, tn), lambda j, i: (i, j)),
        out_shape=jax.ShapeDtypeStruct((m, d), F32),
        scratch_shapes=[pltpu.VMEM((k, tn), BF16)],
        compiler_params=_params("parallel", "arbitrary"),
        name="out_proj",
    )(a, w)


def _res_norm_kernel(y_ref, x_ref, g_ref, gate_ref, *rest, emit_h):
    x_new = x_ref[...] + gate_ref[0] * (_rms(y_ref[...]) * g_ref[...])
    if emit_h:
        g2_ref, sc_ref, sh_ref, o_ref, h_ref = rest
        o_ref[...] = x_new
        h_ref[...] = ((_rms(x_new) * g2_ref[...]) * (1.0 + sc_ref[0]) + sh_ref[0]).astype(h_ref.dtype)
    else:
        (o_ref,) = rest
        o_ref[...] = x_new


def _residual_norm(y, x2, gain, mod3, gate_idx, seq, nxt=None):
    m, d = x2.shape
    tm = _tile(seq, 256)
    per_b = seq // tm
    row = pl.BlockSpec((tm, d), lambda i: (i, 0))
    vec = pl.BlockSpec((1, d), lambda i: (0, 0))
    modv = lambda idx: pl.BlockSpec((1, 1, d), lambda i: ((i // per_b) * 6 + idx, 0, 0))
    in_specs = [row, row, vec, modv(gate_idx)]
    args = [y, x2, gain.reshape(1, d), mod3]
    out_specs, out_shape = [row], [jax.ShapeDtypeStruct((m, d), F32)]
    if nxt is not None:
        gain2, mod3n, sc_idx, sh_idx = nxt
        in_specs += [vec, modv(sc_idx), modv(sh_idx)]
        args += [gain2.reshape(1, d), mod3n, mod3n]
        out_specs.append(row)
        out_shape.append(jax.ShapeDtypeStruct((m, d), BF16))
    out = pl.pallas_call(
        functools.partial(_res_norm_kernel, emit_h=nxt is not None),
        grid=(m // tm,),
        in_specs=in_specs,
        out_specs=out_specs,
        out_shape=out_shape,
        compiler_params=_params("parallel"),
        name="residual_norm",
    )(*args)
    return (out[0], out[1]) if nxt is not None else (out[0], None)


def _glu_kernel(a_ref, wg_ref, wu_ref, o_ref, wgb_ref, wub_ref):
    _cast_weight_once(wg_ref, wgb_ref)
    _cast_weight_once(wu_ref, wub_ref)
    a = a_ref[...]
    g = _dot(a, wgb_ref[...])
    u = _dot(a, wub_ref[...])
    o_ref[...] = (g * _sigmoid(g) * u).astype(o_ref.dtype)


def _glu(a, wg, wu, layer):
    m, k = a.shape
    n = wg.shape[2]
    tm, tn = _tile(m, 1024), _tile(n, 512)
    return pl.pallas_call(
        _glu_kernel,
        grid=(n // tn, m // tm),
        in_specs=[pl.BlockSpec((tm, k), lambda j, i: (i, 0)),
                  pl.BlockSpec((None, k, tn), lambda j, i: (layer, 0, j)),
                  pl.BlockSpec((None, k, tn), lambda j, i: (layer, 0, j))],
        out_specs=pl.BlockSpec((tm, tn), lambda j, i: (i, j)),
        out_shape=jax.ShapeDtypeStruct((m, n), BF16),
        scratch_shapes=[pltpu.VMEM((k, tn), BF16), pltpu.VMEM((k, tn), BF16)],
        compiler_params=_params("parallel", "arbitrary"),
        name="swiglu_up",
    )(a, wg, wu)


def kernel(x, c, w_ada, b_ada, norm_gains, w_in, dn_conv, dn_a_log, dn_dt_bias, dn_norm_w, fox_f_bias,
           w_branch_dn, w_branch_fox, w_out, w_gate, w_up, w_down):
    batch, seq, d = x.shape
    depth = w_ada.shape[0]
    hv = dn_a_log.shape[1]
    hf = fox_f_bias.shape[1]
    v_dim = hv * HEAD_DIM
    conv_dim = dn_conv.shape[1]
    qk_dim = (conv_dim - v_dim) // 2
    hqk = qk_dim // HEAD_DIM
    fox_dim = hf * HEAD_DIM
    assert 2 * hv + hf <= GATE_LANES

    o_z = conv_dim
    o_b = o_z + v_dim
    o_a = o_b + hv
    o_fq = o_a + hv
    o_ff = o_fq + 3 * fox_dim
    o_mg = o_ff + hf
    fox_base = (conv_dim + v_dim) // HEAD_DIM
    merge_base = conv_dim + v_dim + 3 * fox_dim

    mod = _modulation(c, w_ada, b_ada)
    x2 = x.reshape(batch * seq, d)
    pad = GATE_LANES - (2 * hv + hf)
    zpad = jnp.zeros((pad,), F32)

    main_cols = ((0, o_b), (o_fq, 3 * fox_dim), (o_mg, w_in.shape[2] - o_mg))
    w_in_t = jnp.swapaxes(w_in, 1, 2)
    w_dn_all, w_fox_all = w_branch_dn.astype(BF16), w_branch_fox.astype(BF16)
    mod3s = [mod[l].reshape(batch * 6, 1, d) for l in range(depth)]

    h = _norm_mod(x2, norm_gains[0, 0], mod3s[0], 1, 0, seq)
    for l in range(depth):
        mod3 = mod3s[l]
        bias = jnp.concatenate([jnp.zeros((hv,), F32), dn_dt_bias[l], fox_f_bias[l], zpad])
        mult = jnp.concatenate([jnp.ones((hv,), F32), -jnp.exp(dn_a_log[l]), jnp.ones((hf,), F32), zpad])
        pcol = jnp.stack([bias, mult], axis=0)

        proj = _in_proj(h, w_in_t, l, main_cols)
        rows = _gates(h, w_in_t, l, o_b, o_ff, pcol, batch, seq, hv, hf)
        o_dn = _deltanet(proj, dn_conv[l].T, rows, dn_norm_w[l], batch, seq, hqk, hv)
        f_rows = rows[:, 2 * hv:2 * hv + hf, :].reshape(batch, hf, 1, seq)
        o_fox = _fox(proj, f_rows, batch, seq, fox_base, hf)
        ymix = _merge(o_dn, w_dn_all, o_fox, w_fox_all, l, proj, merge_base)
        y = _out_proj(ymix, w_out, l)
        x2, h = _residual_norm(y, x2, norm_gains[l, 1], mod3, 2, seq,
                               nxt=(norm_gains[l, 2], mod3, 4, 3))

        gu = _glu(h, w_gate, w_up, l)
        y = _out_proj(gu, w_down, l)
        nxt = (norm_gains[l + 1, 0], mod3s[l + 1], 1, 0) if l + 1 < depth else None
        x2, h = _residual_norm(y, x2, norm_gains[l, 3], mod3, 5, seq, nxt=nxt)

    return x2.reshape(batch, seq, d)
```

```python
import functools

import jax
import jax.numpy as jnp
from jax import lax
from jax.experimental import pallas as pl
from jax.experimental.pallas import tpu as pltpu

EPS = 1e-6
HEAD_DIM = 128
CONV_K = 4
DN_CHUNK = 128
GATE_LANES = 128
VMEM_LIMIT_BYTES = 48 * 1024 * 1024

F32 = jnp.float32
BF16 = jnp.bfloat16


def _params(*sem):
    return pltpu.CompilerParams(dimension_semantics=sem, vmem_limit_bytes=VMEM_LIMIT_BYTES)


def _tile(dim, pref):
    if dim <= pref:
        return dim
    t = pref - pref % HEAD_DIM
    while dim % t:
        t -= HEAD_DIM
    assert t > 0, (dim, pref)
    return t


def _dot(a, b):
    return jnp.dot(a.astype(BF16), b.astype(BF16), preferred_element_type=F32)


def _dot_nt(a, b):
    return lax.dot_general(a.astype(BF16), b.astype(BF16), (((1,), (1,)), ((), ())),
                           preferred_element_type=F32)


def _sigmoid(x):
    return 0.5 * jnp.tanh(0.5 * x) + 0.5


def _rms(x):
    return x * lax.rsqrt(jnp.mean(x * x, axis=-1, keepdims=True) + EPS)


def _mod_kernel(c_ref, w_ref, b_ref, o_ref):
    c = c_ref[...]
    cond = c * _sigmoid(c)
    o_ref[0] = _dot(cond, w_ref[0]) + b_ref[0]


def _modulation(c, w_ada, b_ada):
    depth, d, n = w_ada.shape
    b = c.shape[0]
    rows = 16
    c_pad = jnp.zeros((rows, d), F32).at[:b].set(c)
    tn = _tile(n, 1024)
    out = pl.pallas_call(
        _mod_kernel,
        grid=(depth, n // tn),
        in_specs=[pl.BlockSpec((rows, d), lambda l, j: (0, 0)),
                  pl.BlockSpec((1, d, tn), lambda l, j: (l, 0, j)),
                  pl.BlockSpec((1, 1, tn), lambda l, j: (l, 0, j))],
        out_specs=pl.BlockSpec((1, rows, tn), lambda l, j: (l, 0, j)),
        out_shape=jax.ShapeDtypeStruct((depth, rows, n), F32),
        compiler_params=_params("parallel", "parallel"),
        name="adaln_mod",
    )(c_pad, w_ada, b_ada.reshape(depth, 1, n))
    return out[:, :b].reshape(depth, b, 6, d)


def _norm_mod_kernel(x_ref, g_ref, sc_ref, sh_ref, o_ref):
    y = _rms(x_ref[...]) * g_ref[...]
    o_ref[...] = (y * (1.0 + sc_ref[0]) + sh_ref[0]).astype(o_ref.dtype)


def _norm_mod(x2, gain, mod3, sc_idx, sh_idx, seq):
    m, d = x2.shape
    tm = _tile(seq, 512)
    per_b = seq // tm
    return pl.pallas_call(
        _norm_mod_kernel,
        grid=(m // tm,),
        in_specs=[pl.BlockSpec((tm, d), lambda i: (i, 0)),
                  pl.BlockSpec((1, d), lambda i: (0, 0)),
                  pl.BlockSpec((1, 1, d), lambda i: ((i // per_b) * 6 + sc_idx, 0, 0)),
                  pl.BlockSpec((1, 1, d), lambda i: ((i // per_b) * 6 + sh_idx, 0, 0))],
        out_specs=pl.BlockSpec((tm, d), lambda i: (i, 0)),
        out_shape=jax.ShapeDtypeStruct((m, d), BF16),
        compiler_params=_params("parallel"),
        name="norm_mod",
    )(x2, gain.reshape(1, d), mod3, mod3)


W_ROWS = 256


def _in_proj_kernel(a_ref, wa_ref, wb_ref, o_ref, w_ref, *, segments):
    j = pl.program_id(0)
    tn = wa_ref.shape[0]

    @pl.when(pl.program_id(1) == 0)
    def _():
        for lo, hi, shift in segments:
            @pl.when((j >= lo) & (j < hi))
            def _():
                for r0 in range(0, tn - shift, W_ROWS):
                    r1 = min(r0 + W_ROWS, tn - shift)
                    w_ref[r0:r1, :] = wa_ref[r0 + shift:r1 + shift, :].astype(BF16)
                if shift:
                    w_ref[tn - shift:, :] = wb_ref[:shift, :].astype(BF16)

    o_ref[...] = _dot_nt(a_ref[...], w_ref[...]).astype(o_ref.dtype)


def _in_proj(a, w_in_t, layer, seg_cols):
    m, k = a.shape
    n_out = sum(w for _, w in seg_cols)
    tn = 1024
    while any(w % tn for _, w in seg_cols):
        tn //= 2
    assert tn >= HEAD_DIM
    tm = _tile(m, 1024)
    segments, out0 = [], 0
    for src0, width in seg_cols:
        shift = src0 - out0
        assert 0 <= shift < HEAD_DIM
        segments.append((out0 // tn, (out0 + width) // tn, shift))
        out0 += width
    kern = functools.partial(_in_proj_kernel, segments=tuple(segments))
    nb = tn // HEAD_DIM
    last_blk = (w_in_t.shape[1] - 1) // HEAD_DIM
    return pl.pallas_call(
        kern,
        grid=(n_out // tn, m // tm),
        in_specs=[pl.BlockSpec((tm, k), lambda j, i: (i, 0)),
                  pl.BlockSpec((None, tn, k), lambda j, i: (layer, j, 0)),
                  pl.BlockSpec((None, HEAD_DIM, k),
                               lambda j, i: (layer, jnp.minimum((j + 1) * nb, last_blk), 0))],
        out_specs=pl.BlockSpec((tm, tn), lambda j, i: (i, j)),
        out_shape=jax.ShapeDtypeStruct((m, n_out), BF16),
        scratch_shapes=[pltpu.VMEM((tn, k), BF16)],
        compiler_params=_params("parallel", "arbitrary"),
        name="in_proj",
    )(a, w_in_t, w_in_t)


def _split3(x):
    hi = x.astype(BF16)
    r1 = x - hi.astype(F32)
    mid = r1.astype(BF16)
    lo = (r1 - mid.astype(F32)).astype(BF16)
    return hi, mid, lo


def _gate_act(x, bias, mult, idx, hv, hf):
    xb = x + bias
    e = jnp.exp(-jnp.abs(xb))
    l1p = jnp.log(1.0 + e)
    sig = jnp.where(xb >= 0, 1.0, e) / (1.0 + e)
    softplus = jnp.maximum(xb, 0.0) + l1p
    logsig = jnp.minimum(xb, 0.0) - l1p
    return jnp.where(idx < hv, sig,
                     jnp.where(idx < 2 * hv, mult * softplus,
                               jnp.where(idx < 2 * hv + hf, logsig, 0.0)))


def _gates_kernel(h_ref, wa_ref, wb_ref, pcol_ref, rows_ref, carry_ref, *, hv, hf, chunk):
    t = pl.program_id(1)
    tm = h_ref.shape[0]

    @pl.when(t == 0)
    def _():
        carry_ref[...] = jnp.zeros_like(carry_ref)

    row_w = lax.broadcasted_iota(jnp.int32, wa_ref.shape, 0)
    w = jnp.where(row_w < 2 * hv, wa_ref[...], jnp.where(row_w < 2 * hv + hf, wb_ref[...], 0.0))
    g_cols = _dot_nt(h_ref[...], w)

    lane = lax.broadcasted_iota(jnp.int32, (tm, GATE_LANES), 1)
    val_c = _gate_act(g_cols, pcol_ref[0:1, :], pcol_ref[1:2, :], lane, hv, hf)
    sub = lax.broadcasted_iota(jnp.int32, (GATE_LANES, tm), 0)
    val_r = val_c.T

    r = lax.broadcasted_iota(jnp.int32, (tm, tm), 0)
    c = lax.broadcasted_iota(jnp.int32, (tm, tm), 1)
    sh = chunk.bit_length() - 1
    same = (r >> sh) == (c >> sh)
    triu_blk = jnp.where((r <= c) & same, 1.0, 0.0).astype(BF16)
    triu_all = jnp.where(r <= c, 1.0, 0.0).astype(BF16)

    pieces_r = _split3(val_r)
    cum_r_blk = sum(jnp.dot(p, triu_blk, preferred_element_type=F32) for p in pieces_r)
    cum_r_all = sum(jnp.dot(p, triu_all, preferred_element_type=F32) for p in pieces_r)
    cum_r_all = cum_r_all + carry_ref[:, 0:1]

    is_decay_r = (sub >= hv) & (sub < 2 * hv)
    is_forget_r = (sub >= 2 * hv) & (sub < 2 * hv + hf)
    rows = jnp.where(is_decay_r, cum_r_blk, jnp.where(is_forget_r, cum_r_all, val_r))
    rows_ref[0] = rows
    carry_ref[...] = jnp.broadcast_to(cum_r_all[:, tm - 1:tm], carry_ref.shape)


def _gates(h2, w_in, layer, col_ba, col_f, pcol, batch, seq, hv, hf):
    m, d = h2.shape
    tm = _tile(seq, 512)
    per_b = seq // tm
    assert col_ba % GATE_LANES == 0 and col_f % GATE_LANES == 2 * hv
    blk_ba, blk_f = col_ba // GATE_LANES, col_f // GATE_LANES
    kern = functools.partial(_gates_kernel, hv=hv, hf=hf, chunk=DN_CHUNK)
    return pl.pallas_call(
        kern,
        grid=(batch, per_b),
        in_specs=[pl.BlockSpec((tm, d), lambda b, t: (b * per_b + t, 0)),
                  pl.BlockSpec((None, GATE_LANES, d), lambda b, t: (layer, blk_ba, 0)),
                  pl.BlockSpec((None, GATE_LANES, d), lambda b, t: (layer, blk_f, 0)),
                  pl.BlockSpec((2, GATE_LANES), lambda b, t: (0, 0))],
        out_specs=pl.BlockSpec((1, GATE_LANES, tm), lambda b, t: (b, 0, t)),
        out_shape=jax.ShapeDtypeStruct((batch, GATE_LANES, seq), F32),
        scratch_shapes=[pltpu.VMEM((GATE_LANES, GATE_LANES), F32)],
        compiler_params=_params("parallel", "arbitrary"),
        name="gates",
    )(h2, w_in, w_in, pcol)


def _causal_conv(u, tail, w):
    out = u * w[CONV_K - 1:CONV_K, :]
    row8 = lax.broadcasted_iota(jnp.int32, tail.shape, 0)
    for s in range(1, CONV_K):
        rolled = pltpu.roll(u, s, 0)
        head = jnp.where(row8 < s, pltpu.roll(tail, s, 0), rolled[0:8])
        shifted = jnp.concatenate([head, rolled[8:]], axis=0)
        out = out + shifted * w[CONV_K - 1 - s:CONV_K - s, :]
    return out


INV_LEVELS = (DN_CHUNK // 8).bit_length() - 1
MASK_DIAG8, MASK_EYE, MASK_STRICT = 0, INV_LEVELS + 1, INV_LEVELS + 2
N_MASKS = INV_LEVELS + 3


def _inverse_masks(n):
    r = lax.broadcasted_iota(jnp.int32, (n, n), 0)
    c = lax.broadcasted_iota(jnp.int32, (n, n), 1)
    masks = [(r >> 3) == (c >> 3)]
    for sh in range(3, 3 + INV_LEVELS):
        masks.append(((r >> (sh + 1)) == (c >> (sh + 1))) & ((r >> sh) == (c >> sh) + 1))
    masks += [r == c, r > c]
    return [jnp.where(m, 1.0, 0.0) for m in masks]


def _unit_lower_inverses(lms, mask_ref, maskb_ref):
    n0s = [-(lm * mask_ref[MASK_DIAG8]) for lm in lms]
    n2s = [_dot(n0, n0) for n0 in n0s]
    n4s = [_dot(n2, n2) for n2 in n2s]
    ps = [mask_ref[MASK_EYE] + n0 for n0 in n0s]
    ps = [p + _dot(p, n2) for p, n2 in zip(ps, n2s)]
    ps = [p + _dot(p, n4) for p, n4 in zip(ps, n4s)]
    lms_b = [lm.astype(BF16) for lm in lms]
    for level in range(1, INV_LEVELS + 1):
        xs = [_dot(lm_b * maskb_ref[level], p) for lm_b, p in zip(lms_b, ps)]
        ps = [p - _dot(p, x) for p, x in zip(ps, xs)]
    return ps


def _deltanet_kernel(q_ref, k_ref, v_ref, z_ref, wq_ref, wk_ref, wv_ref, rows_ref, nw_ref,
                     o_ref, s_ref, qt_ref, kt_ref, vt_ref, mask_ref, maskb_ref, u_ref, lhs1_ref, lhs2_ref,
                     egl_ref,
                     *, hv, rep, groups, unroll, scan_unroll):
    i = pl.program_id(1)
    tb = pl.program_id(2)
    rows_blk = q_ref.shape[0]
    n_chunks = rows_blk // DN_CHUNK
    c_ = DN_CHUNK
    d = HEAD_DIM
    nh = groups * rep

    @pl.when(tb == 0)
    def _():
        s_ref[...] = jnp.zeros_like(s_ref)
        qt_ref[...] = jnp.zeros_like(qt_ref)
        kt_ref[...] = jnp.zeros_like(kt_ref)
        vt_ref[...] = jnp.zeros_like(vt_ref)
        for j, m in enumerate(_inverse_masks(c_)):
            mask_ref[j] = m
            maskb_ref[j] = m.astype(BF16)

    r = lax.broadcasted_iota(jnp.int32, (c_, c_), 0)
    c = lax.broadcasted_iota(jnp.int32, (c_, c_), 1)
    nw = nw_ref[...]

    def tail_of(ref, tail_ref, ci, off):
        prev = pl.multiple_of(jnp.maximum(off - 16, 0), 16)
        inside = ref[pl.ds(prev, 16), :].astype(F32)[8:16]
        return jnp.where(ci == 0, tail_ref[...], inside)

    def prepare_body(j, carry):
        chains = []
        for uu in range(unroll):
            ci = j * unroll + uu
            off = pl.multiple_of(ci * c_, c_)
            sl = pl.ds(off, c_)
            q_raw = q_ref[sl, :].astype(F32)
            k_raw = k_ref[sl, :].astype(F32)
            v_raw = v_ref[sl, :].astype(F32)
            qc = _causal_conv(q_raw, tail_of(q_ref, qt_ref, ci, off), wq_ref[...])
            kc = _causal_conv(k_raw, tail_of(k_ref, kt_ref, ci, off), wk_ref[...])
            vc = _causal_conv(v_raw, tail_of(v_ref, vt_ref, ci, off), wv_ref[...])
            qc = qc * _sigmoid(qc)
            kc = kc * _sigmoid(kc)
            vc = vc * _sigmoid(vc)
            row_b = pl.multiple_of(((i * nh) // 8) * 8, 8)
            row_g = pl.multiple_of(((hv + i * nh) // 8) * 8, 8)
            rows8_b = rows_ref[0, pl.ds(row_b, 8), sl]
            rows8_g = rows_ref[0, pl.ds(row_g, 8), sl]
            sub8 = lax.broadcasted_iota(jnp.int32, (8, c_), 0)
            for g in range(groups):
                qg_ = qc[:, g * d:(g + 1) * d]
                kg_ = kc[:, g * d:(g + 1) * d]
                qn = qg_ * (lax.rsqrt(jnp.sum(qg_ * qg_, axis=-1, keepdims=True) + EPS) * (d ** -0.5))
                kn = kg_ * lax.rsqrt(jnp.sum(kg_ * kg_, axis=-1, keepdims=True) + EPS)
                kk = _dot_nt(kn, kn)
                qk = _dot_nt(qn, kn)
                for hh in range(rep):
                    hl = g * rep + hh
                    head = i * nh + hl
                    beta_r = jnp.sum(jnp.where(sub8 == head % 8, rows8_b, 0.0), axis=0, keepdims=True)
                    gc_r = jnp.sum(jnp.where(sub8 == (hv + head) % 8, rows8_g, 0.0), axis=0, keepdims=True)
                    beta_c = jnp.broadcast_to(beta_r, (c_, c_)).T
                    gc_c = jnp.broadcast_to(gc_r, (c_, c_)).T
                    g_last = gc_r[:, c_ - 1:c_]
                    decay = jnp.exp(jnp.where(r >= c, gc_c - gc_r, -1e30))
                    lm = (beta_c * kk) * (decay * mask_ref[MASK_STRICT])
                    eg = jnp.exp(gc_c)
                    v_h = vc[:, hl * d:(hl + 1) * d]
                    rhs = jnp.concatenate([v_h * beta_c, kn * (beta_c * eg)], axis=1).astype(BF16)
                    kd = kn * jnp.exp(g_last - gc_c)
                    lhs2_ref[hl, ci] = jnp.concatenate([qk * decay, kd.T], axis=0).astype(BF16)
                    egl_ref[hl, ci] = jnp.broadcast_to(jnp.exp(g_last), (8, d))
                    chains.append((hl, ci, sl, lm, rhs, (qn * eg).astype(BF16)))
        tinvs = _unit_lower_inverses([ch[3] for ch in chains], mask_ref, maskb_ref)
        uws = [_dot(tinv, ch[4]) for tinv, ch in zip(tinvs, chains)]
        for uw, (hl, ci, sl, _, _, qg) in zip(uws, chains):
            u_ref[hl, sl, :] = uw[:, :d]
            lhs1_ref[hl, ci] = jnp.concatenate([uw[:, d:].astype(BF16), qg], axis=0)
        return carry

    lax.fori_loop(0, n_chunks // unroll, prepare_body, 0)

    heads = range(nh)

    def scan_chunk(ci):
        off = pl.multiple_of(ci * c_, c_)
        sl = pl.ds(off, c_)
        states = [s_ref[hl] for hl in heads]
        ws_qs = [_dot(lhs1_ref[hl, ci], states[hl]) for hl in heads]
        v_new = [u_ref[hl, sl, :] - ws_qs[hl][:c_] for hl in heads]
        av_kv = [_dot(lhs2_ref[hl, ci], v_new[hl]) for hl in heads]
        for hl in heads:
            s_ref[hl] = states[hl] * egl_ref[hl, ci, 0:1, :] + av_kv[hl][c_:]
        for hl in heads:
            o = ws_qs[hl][c_:] + av_kv[hl][:c_]
            z = z_ref[sl, hl * d:(hl + 1) * d].astype(F32)
            out = _rms(o) * nw * (z * _sigmoid(z))
            o_ref[sl, hl * d:(hl + 1) * d] = out.astype(o_ref.dtype)

    def scan_body(j, carry):
        for uu in range(scan_unroll):
            scan_chunk(j * scan_unroll + uu)
        return carry

    lax.fori_loop(0, n_chunks // scan_unroll, scan_body, 0)

    qt_ref[...] = q_ref[rows_blk - 16:, :].astype(F32)[8:16]
    kt_ref[...] = k_ref[rows_blk - 16:, :].astype(F32)[8:16]
    vt_ref[...] = v_ref[rows_blk - 16:, :].astype(F32)[8:16]


def _deltanet(proj, conv_wt, rows, norm_w, batch, seq, hqk, hv):
    rep = hv // hqk
    assert hv == rep * hqk and DN_CHUNK == HEAD_DIM
    groups = next(g for g in (4, 2, 1) if hqk % g == 0 and 8 % (g * rep) == 0)
    nh = groups * rep
    assert 8 % nh == 0 and hv % nh == 0 and (2 * hqk) % nh == 0
    tb = _tile(seq, 1024)
    per_b = seq // tb
    wqk, wv = groups * HEAD_DIM, nh * HEAD_DIM
    nqk_blk, nv_blk = hqk // groups, hv // nh
    k_blk0 = nqk_blk
    v_blk0 = (2 * hqk) // nh
    z_blk0 = v_blk0 + nv_blk
    row = lambda b, i, t: b * per_b + t
    n_chunks = tb // DN_CHUNK
    unroll = next(u for u in (4, 2, 1) if n_chunks % u == 0 and u * nh <= 16)
    scan_unroll = next(u for u in (4, 2, 1) if n_chunks % u == 0)
    kern = functools.partial(_deltanet_kernel, hv=hv, rep=rep, groups=groups, unroll=unroll,
                             scan_unroll=scan_unroll)
    return pl.pallas_call(
        kern,
        grid=(batch, nqk_blk, per_b),
        in_specs=[pl.BlockSpec((tb, wqk), lambda b, i, t: (row(b, i, t), i)),
                  pl.BlockSpec((tb, wqk), lambda b, i, t: (row(b, i, t), k_blk0 + i)),
                  pl.BlockSpec((tb, wv), lambda b, i, t: (row(b, i, t), v_blk0 + i)),
                  pl.BlockSpec((tb, wv), lambda b, i, t: (row(b, i, t), z_blk0 + i)),
                  pl.BlockSpec((CONV_K, wqk), lambda b, i, t: (0, i)),
                  pl.BlockSpec((CONV_K, wqk), lambda b, i, t: (0, k_blk0 + i)),
                  pl.BlockSpec((CONV_K, wv), lambda b, i, t: (0, v_blk0 + i)),
                  pl.BlockSpec((1, GATE_LANES, tb), lambda b, i, t: (b, 0, t)),
                  pl.BlockSpec((1, HEAD_DIM), lambda b, i, t: (0, 0))],
        out_specs=pl.BlockSpec((tb, wv), lambda b, i, t: (row(b, i, t), i)),
        out_shape=jax.ShapeDtypeStruct((batch * seq, hv * HEAD_DIM), BF16),
        scratch_shapes=[pltpu.VMEM((nh, HEAD_DIM, HEAD_DIM), F32),
                        pltpu.VMEM((8, wqk), F32),
                        pltpu.VMEM((8, wqk), F32),
                        pltpu.VMEM((8, wv), F32),
                        pltpu.VMEM((N_MASKS, DN_CHUNK, DN_CHUNK), F32),
                        pltpu.VMEM((N_MASKS, DN_CHUNK, DN_CHUNK), BF16),
                        pltpu.VMEM((nh, tb, HEAD_DIM), F32),
                        pltpu.VMEM((nh, n_chunks, 2 * DN_CHUNK, HEAD_DIM), BF16),
                        pltpu.VMEM((nh, n_chunks, 2 * DN_CHUNK, DN_CHUNK), BF16),
                        pltpu.VMEM((nh, n_chunks, 8, HEAD_DIM), F32)],
        compiler_params=_params("parallel", "parallel", "arbitrary"),
        name="deltanet",
    )(proj, proj, proj, proj, conv_wt, conv_wt, conv_wt, rows, norm_w.reshape(1, HEAD_DIM))


LOG2E = 1.4426950408889634
MXU_COLS = 256
ONES_ROWS = 16


def _reduce_rows(x, op):
    rows, lanes = x.shape
    slabs = 8
    if rows % (8 * slabs) == 0:
        x3 = x.reshape(slabs, rows // slabs, lanes)
        x = x3[0]
        for j in range(1, slabs):
            x = op(x, x3[j])
    final = jnp.max if op is jnp.maximum else jnp.sum
    return final(x, axis=0, keepdims=True)


def _fox_kernel(q_ref, k_ref, v_ref, f_ref, o_ref, vt_ref, fcol_ref, s_ref, m_ref, acc_ref, *, scale):
    qi = pl.program_id(2)
    tq = q_ref.shape[0]
    tk = tq
    seq = k_ref.shape[0]
    qt = min(MXU_COLS, tq)
    d = HEAD_DIM
    nhd = q_ref.shape[1] // d
    hd_lanes = lambda hd: slice(hd * d, (hd + 1) * d)

    @pl.when(qi == 0)
    def _():
        n_blk = seq // d
        group = next(g for g in (4, 2, 1) if n_blk % g == 0)

        def prep(bi, carry):
            for j in range(group):
                off = pl.multiple_of((bi * group + j) * d, d)
                for hd in range(nhd):
                    v_blk = v_ref[pl.ds(off, d), hd_lanes(hd)]
                    vt_ref[hd, 0:d, pl.ds(off, d)] = v_blk.astype(F32).T.astype(BF16)
                    frow = f_ref[0, hd, :, pl.ds(off, d)] * LOG2E
                    fcol_ref[hd, pl.ds(off, d), :] = jnp.broadcast_to(frow, (d, d)).T
            return carry
        lax.fori_loop(0, n_blk // group, prep, 0)
        for hd in range(nhd):
            vt_ref[hd, d:, :] = jnp.ones((ONES_ROWS, seq), BF16)

    m_ref[...] = jnp.full_like(m_ref, -1e30)
    acc_ref[...] = jnp.zeros_like(acc_ref)

    all_heads = tuple(range(nhd))

    def scores(kj, slot, heads=all_heads):
        off = pl.multiple_of(kj * tk, tk)
        for hd in heads:
            s_ref[slot, hd] = _dot_nt(k_ref[pl.ds(off, tk), hd_lanes(hd)], q_ref[:, hd_lanes(hd)])

    def consume(kj, slot, masked, heads=all_heads):
        off = pl.multiple_of(kj * tk, tk)
        for hd in heads:
            fcol = fcol_ref[hd, pl.ds(off, tk), :]
            fcol = jnp.concatenate([fcol] * (qt // d), axis=1)
            vt = vt_ref[hd, :, pl.ds(off, tk)]
            for t in range(tq // qt):
                lanes = slice(t * qt, (t + 1) * qt)
                nk = min((t + 1) * qt, tk) if masked else tk
                s = s_ref[slot, hd, 0:nk, lanes] * (scale * LOG2E) - fcol[0:nk]
                if masked:
                    key = lax.broadcasted_iota(jnp.int32, (nk, qt), 0)
                    qry = lax.broadcasted_iota(jnp.int32, (nk, qt), 1) + t * qt
                    s = jnp.where(key <= qry, s, -1e30)
                m_prev = m_ref[hd, :, lanes]
                m_new = jnp.maximum(m_prev, _reduce_rows(s, jnp.maximum))
                p = jnp.exp2(s - m_new)
                alpha = jnp.exp2(m_prev - m_new)
                m_ref[hd, :, lanes] = m_new
                acc_ref[hd, :, lanes] = alpha * acc_ref[hd, :, lanes] + _dot(vt[:, 0:nk], p)

    scores(0, 0)

    def body(i, carry):
        kj = 2 * i
        for hd in all_heads:
            scores(kj + 1, 1, (hd,))
            consume(kj, 0, False, (hd,))
        for hd in all_heads:
            scores(kj + 2, 0, (hd,))
            consume(kj + 1, 1, False, (hd,))
        return carry

    lax.fori_loop(0, qi // 2, body, 0)

    @pl.when(qi % 2 == 0)
    def _():
        consume(qi, 0, True)

    @pl.when(qi % 2 == 1)
    def _():
        scores(qi, 1)
        consume(qi - 1, 0, False)
        consume(qi, 1, True)

    for hd in range(nhd):
        out = (acc_ref[hd, 0:d, :] / acc_ref[hd, d:d + 1, :]).T
        o_ref[:, hd_lanes(hd)] = out.astype(o_ref.dtype)


def _fox(proj, f_rows, batch, seq, base, hf):
    tq = _tile(seq, 512)
    nq = seq // tq
    nhd = next(n for n in (4, 2, 1) if hf % n == 0 and base % n == 0)
    w = nhd * HEAD_DIM
    b0, nblk = base // nhd, hf // nhd
    kern = functools.partial(_fox_kernel, scale=HEAD_DIM ** -0.5)
    return pl.pallas_call(
        kern,
        grid=(batch, nblk, nq),
        in_specs=[pl.BlockSpec((tq, w), lambda b, h, qi: (b * nq + qi, b0 + h)),
                  pl.BlockSpec((seq, w), lambda b, h, qi: (b, b0 + nblk + h)),
                  pl.BlockSpec((seq, w), lambda b, h, qi: (b, b0 + 2 * nblk + h)),
                  pl.BlockSpec((1, nhd, 1, seq), lambda b, h, qi: (b, h, 0, 0))],
        out_specs=pl.BlockSpec((tq, w), lambda b, h, qi: (b * nq + qi, h)),
        out_shape=jax.ShapeDtypeStruct((batch * seq, hf * HEAD_DIM), BF16),
        scratch_shapes=[pltpu.VMEM((nhd, HEAD_DIM + ONES_ROWS, seq), BF16),
                        pltpu.VMEM((nhd, seq, HEAD_DIM), F32),
                        pltpu.VMEM((2, nhd, tq, tq), F32),
                        pltpu.VMEM((nhd, 1, tq), F32),
                        pltpu.VMEM((nhd, HEAD_DIM + ONES_ROWS, tq), F32)],
        compiler_params=_params("parallel", "parallel", "arbitrary"),
        name="fox_attention",
    )(proj, proj, proj, f_rows)


def _merge_kernel(a1_ref, w1_ref, a2_ref, w2_ref, m1_ref, m2_ref, o_ref):
    y1 = _dot(a1_ref[...], w1_ref[...])
    y2 = _dot(a2_ref[...], w2_ref[...])
    g1 = _sigmoid(m1_ref[...].astype(F32))
    g2 = _sigmoid(m2_ref[...].astype(F32))
    o_ref[...] = (g1 * y1 + g2 * y2).astype(o_ref.dtype)


def _merge(o_dn, w_dn, o_fox, w_fox, layer, proj, merge_base_cols):
    m, k1 = o_dn.shape
    k2 = o_fox.shape[1]
    d = w_dn.shape[2]
    tm, tn = _tile(m, 512), _tile(d, 512)
    assert merge_base_cols % tn == 0
    mb = merge_base_cols // tn
    return pl.pallas_call(
        _merge_kernel,
        grid=(m // tm, d // tn),
        in_specs=[pl.BlockSpec((tm, k1), lambda i, j: (i, 0)),
                  pl.BlockSpec((None, k1, tn), lambda i, j: (layer, 0, j)),
                  pl.BlockSpec((tm, k2), lambda i, j: (i, 0)),
                  pl.BlockSpec((None, k2, tn), lambda i, j: (layer, 0, j)),
                  pl.BlockSpec((tm, tn), lambda i, j: (i, mb + j)),
                  pl.BlockSpec((tm, tn), lambda i, j: (i, mb + d // tn + j))],
        out_specs=pl.BlockSpec((tm, tn), lambda i, j: (i, j)),
        out_shape=jax.ShapeDtypeStruct((m, d), BF16),
        compiler_params=_params("parallel", "parallel"),
        name="branch_merge",
    )(o_dn, w_dn, o_fox, w_fox, proj, proj)


def _cast_weight_once(w_ref, wb_ref):
    @pl.when(pl.program_id(1) == 0)
    def _():
        k = w_ref.shape[0]
        for r0 in range(0, k, W_ROWS):
            r1 = min(r0 + W_ROWS, k)
            wb_ref[r0:r1, :] = w_ref[r0:r1, :].astype(BF16)


def _mm_f32_kernel(a_ref, w_ref, o_ref, wb_ref):
    _cast_weight_once(w_ref, wb_ref)
    o_ref[...] = _dot(a_ref[...], wb_ref[...])


def _out_proj(a, w, layer):
    m, k = a.shape
    d = w.shape[2]
    pref = 512 if k > 4096 else 1024
    tm, tn = _tile(m, pref), _tile(d, pref)
    return pl.pallas_call(
        _mm_f32_kernel,
        grid=(d // tn, m // tm),
        in_specs=[pl.BlockSpec((tm, k), lambda j, i: (i, 0)),
                  pl.BlockSpec((None, k, tn), lambda j, i: (layer, 0, j))],
        out_specs=pl.BlockSpec((tm, tn), lambda j, i: (i, j)),
        out_shape=jax.ShapeDtypeStruct((m, d), F32),
        scratch_shapes=[pltpu.VMEM((k, tn), BF16)],
        compiler_params=_params("parallel", "arbitrary"),
        name="out_proj",
    )(a, w)


def _res_norm_kernel(y_ref, x_ref, g_ref, gate_ref, *rest, emit_h):
    x_new = x_ref[...] + gate_ref[0] * (_rms(y_ref[...]) * g_ref[...])
    if emit_h:
        g2_ref, sc_ref, sh_ref, o_ref, h_ref = rest
        o_ref[...] = x_new
        h_ref[...] = ((_rms(x_new) * g2_ref[...]) * (1.0 + sc_ref[0]) + sh_ref[0]).astype(h_ref.dtype)
    else:
        (o_ref,) = rest
        o_ref[...] = x_new


def _residual_norm(y, x2, gain, mod3, gate_idx, seq, nxt=None):
    m, d = x2.shape
    tm = _tile(seq, 256)
    per_b = seq // tm
    row = pl.BlockSpec((tm, d), lambda i: (i, 0))
    vec = pl.BlockSpec((1, d), lambda i: (0, 0))
    modv = lambda idx: pl.BlockSpec((1, 1, d), lambda i: ((i // per_b) * 6 + idx, 0, 0))
    in_specs = [row, row, vec, modv(gate_idx)]
    args = [y, x2, gain.reshape(1, d), mod3]
    out_specs, out_shape = [row], [jax.ShapeDtypeStruct((m, d), F32)]
    if nxt is not None:
        gain2, mod3n, sc_idx, sh_idx = nxt
        in_specs += [vec, modv(sc_idx), modv(sh_idx)]
        args += [gain2.reshape(1, d), mod3n, mod3n]
        out_specs.append(row)
        out_shape.append(jax.ShapeDtypeStruct((m, d), BF16))
    out = pl.pallas_call(
        functools.partial(_res_norm_kernel, emit_h=nxt is not None),
        grid=(m // tm,),
        in_specs=in_specs,
        out_specs=out_specs,
        out_shape=out_shape,
        compiler_params=_params("parallel"),
        name="residual_norm",
    )(*args)
    return (out[0], out[1]) if nxt is not None else (out[0], None)


def _glu_kernel(a_ref, wg_ref, wu_ref, o_ref, wgb_ref, wub_ref):
    _cast_weight_once(wg_ref, wgb_ref)
    _cast_weight_once(wu_ref, wub_ref)
    a = a_ref[...]
    g = _dot(a, wgb_ref[...])
    u = _dot(a, wub_ref[...])
    o_ref[...] = (g * _sigmoid(g) * u).astype(o_ref.dtype)


def _glu(a, wg, wu, layer):
    m, k = a.shape
    n = wg.shape[2]
    tm, tn = _tile(m, 1024), _tile(n, 512)
    return pl.pallas_call(
        _glu_kernel,
        grid=(n // tn, m // tm),
        in_specs=[pl.BlockSpec((tm, k), lambda j, i: (i, 0)),
                  pl.BlockSpec((None, k, tn), lambda j, i: (layer, 0, j)),
                  pl.BlockSpec((None, k, tn), lambda j, i: (layer, 0, j))],
        out_specs=pl.BlockSpec((tm, tn), lambda j, i: (i, j)),
        out_shape=jax.ShapeDtypeStruct((m, n), BF16),
        scratch_shapes=[pltpu.VMEM((k, tn), BF16), pltpu.VMEM((k, tn), BF16)],
        compiler_params=_params("parallel", "arbitrary"),
        name="swiglu_up",
    )(a, wg, wu)


def kernel(x, c, w_ada, b_ada, norm_gains, w_in, dn_conv, dn_a_log, dn_dt_bias, dn_norm_w, fox_f_bias,
           w_branch_dn, w_branch_fox, w_out, w_gate, w_up, w_down):
    batch, seq, d = x.shape
    depth = w_ada.shape[0]
    hv = dn_a_log.shape[1]
    hf = fox_f_bias.shape[1]
    v_dim = hv * HEAD_DIM
    conv_dim = dn_conv.shape[1]
    qk_dim = (conv_dim - v_dim) // 2
    hqk = qk_dim // HEAD_DIM
    fox_dim = hf * HEAD_DIM
    assert 2 * hv + hf <= GATE_LANES

    o_z = conv_dim
    o_b = o_z + v_dim
    o_a = o_b + hv
    o_fq = o_a + hv
    o_ff = o_fq + 3 * fox_dim
    o_mg = o_ff + hf
    fox_base = (conv_dim + v_dim) // HEAD_DIM
    merge_base = conv_dim + v_dim + 3 * fox_dim

    mod = _modulation(c, w_ada, b_ada)
    x2 = x.reshape(batch * seq, d)
    pad = GATE_LANES - (2 * hv + hf)
    zpad = jnp.zeros((pad,), F32)

    main_cols = ((0, o_b), (o_fq, 3 * fox_dim), (o_mg, w_in.shape[2] - o_mg))
    w_in_t = jnp.swapaxes(w_in, 1, 2)
    w_dn_all, w_fox_all = w_branch_dn.astype(BF16), w_branch_fox.astype(BF16)
    mod3s = [mod[l].reshape(batch * 6, 1, d) for l in range(depth)]

    h = _norm_mod(x2, norm_gains[0, 0], mod3s[0], 1, 0, seq)
    for l in range(depth):
        mod3 = mod3s[l]
        bias = jnp.concatenate([jnp.zeros((hv,), F32), dn_dt_bias[l], fox_f_bias[l], zpad])
        mult = jnp.concatenate([jnp.ones((hv,), F32), -jnp.exp(dn_a_log[l]), jnp.ones((hf,), F32), zpad])
        pcol = jnp.stack([bias, mult], axis=0)

        proj = _in_proj(h, w_in_t, l, main_cols)
        rows = _gates(h, w_in_t, l, o_b, o_ff, pcol, batch, seq, hv, hf)
        o_dn = _deltanet(proj, dn_conv[l].T, rows, dn_norm_w[l], batch, seq, hqk, hv)
        f_rows = rows[:, 2 * hv:2 * hv + hf, :].reshape(batch, hf, 1, seq)
        o_fox = _fox(proj, f_rows, batch, seq, fox_base, hf)
        ymix = _merge(o_dn, w_dn_all, o_fox, w_fox_all, l, proj, merge_base)
        y = _out_proj(ymix, w_out, l)
        x2, h = _residual_norm(y, x2, norm_gains[l, 1], mod3, 2, seq,
                               nxt=(norm_gains[l, 2], mod3, 4, 3))

        gu = _glu(h, w_gate, w_up, l)
        y = _out_proj(gu, w_down, l)
        nxt = (norm_gains[l + 1, 0], mod3s[l + 1], 1, 0) if l + 1 < depth else None
        x2, h = _residual_norm(y, x2, norm_gains[l, 3], mod3, 5, seq, nxt=nxt)

    return x2.reshape(batch, seq, d)
```

```python
import functools

import jax
import jax.numpy as jnp
from jax import lax
from jax.experimental import pallas as pl
from jax.experimental.pallas import tpu as pltpu

EPS = 1e-6
HEAD_DIM = 128
CONV_K = 4
DN_CHUNK = 128
GATE_LANES = 128
VMEM_LIMIT_BYTES = 48 * 1024 * 1024

F32 = jnp.float32
BF16 = jnp.bfloat16


def _params(*sem):
    return pltpu.CompilerParams(dimension_semantics=sem, vmem_limit_bytes=VMEM_LIMIT_BYTES)


def _tile(dim, pref):
    if dim <= pref:
        return dim
    t = pref - pref % HEAD_DIM
    while dim % t:
        t -= HEAD_DIM
    assert t > 0, (dim, pref)
    return t


def _dot(a, b):
    return jnp.dot(a.astype(BF16), b.astype(BF16), preferred_element_type=F32)


def _dot_nt(a, b):
    return lax.dot_general(a.astype(BF16), b.astype(BF16), (((1,), (1,)), ((), ())),
                           preferred_element_type=F32)


def _sigmoid(x):
    return 0.5 * jnp.tanh(0.5 * x) + 0.5


def _rms(x):
    return x * lax.rsqrt(jnp.mean(x * x, axis=-1, keepdims=True) + EPS)


def _mod_kernel(c_ref, w_ref, b_ref, o_ref):
    c = c_ref[...]
    cond = c * _sigmoid(c)
    o_ref[0] = _dot(cond, w_ref[0]) + b_ref[0]


def _modulation(c, w_ada, b_ada):
    depth, d, n = w_ada.shape
    b = c.shape[0]
    rows = 16
    c_pad = jnp.zeros((rows, d), F32).at[:b].set(c)
    tn = _tile(n, 1024)
    out = pl.pallas_call(
        _mod_kernel,
        grid=(depth, n // tn),
        in_specs=[pl.BlockSpec((rows, d), lambda l, j: (0, 0)),
                  pl.BlockSpec((1, d, tn), lambda l, j: (l, 0, j)),
                  pl.BlockSpec((1, 1, tn), lambda l, j: (l, 0, j))],
        out_specs=pl.BlockSpec((1, rows, tn), lambda l, j: (l, 0, j)),
        out_shape=jax.ShapeDtypeStruct((depth, rows, n), F32),
        compiler_params=_params("parallel", "parallel"),
        name="adaln_mod",
    )(c_pad, w_ada, b_ada.reshape(depth, 1, n))
    return out[:, :b].reshape(depth, b, 6, d)


def _norm_mod_kernel(x_ref, g_ref, sc_ref, sh_ref, o_ref):
    y = _rms(x_ref[...]) * g_ref[...]
    o_ref[...] = (y * (1.0 + sc_ref[0]) + sh_ref[0]).astype(o_ref.dtype)


def _norm_mod(x2, gain, mod3, sc_idx, sh_idx, seq):
    m, d = x2.shape
    tm = _tile(seq, 512)
    per_b = seq // tm
    return pl.pallas_call(
        _norm_mod_kernel,
        grid=(m // tm,),
        in_specs=[pl.BlockSpec((tm, d), lambda i: (i, 0)),
                  pl.BlockSpec((1, d), lambda i: (0, 0)),
                  pl.BlockSpec((1, 1, d), lambda i: ((i // per_b) * 6 + sc_idx, 0, 0)),
                  pl.BlockSpec((1, 1, d), lambda i: ((i // per_b) * 6 + sh_idx, 0, 0))],
        out_specs=pl.BlockSpec((tm, d), lambda i: (i, 0)),
        out_shape=jax.ShapeDtypeStruct((m, d), BF16),
        compiler_params=_params("parallel"),
        name="norm_mod",
    )(x2, gain.reshape(1, d), mod3, mod3)


W_ROWS = 256
IN_PROJ_ROWS = 1024
IN_PROJ_VMEM_BYTES = 56 * 1024 * 1024


def _in_proj_kernel(a_ref, wa_ref, wb_ref, o_ref, w_ref, *, segments):
    j = pl.program_id(0)
    tn = wa_ref.shape[0]

    @pl.when(pl.program_id(1) == 0)
    def _():
        for lo, hi, shift in segments:
            @pl.when((j >= lo) & (j < hi))
            def _():
                for r0 in range(0, tn - shift, W_ROWS):
                    r1 = min(r0 + W_ROWS, tn - shift)
                    w_ref[r0:r1, :] = wa_ref[r0 + shift:r1 + shift, :].astype(BF16)
                if shift:
                    w_ref[tn - shift:, :] = wb_ref[:shift, :].astype(BF16)

    tm = a_ref.shape[0]
    for r0 in range(0, tm, IN_PROJ_ROWS):
        r1 = min(r0 + IN_PROJ_ROWS, tm)
        o_ref[r0:r1, :] = _dot_nt(a_ref[r0:r1, :], w_ref[...]).astype(o_ref.dtype)


def _in_proj(a, w_in_t, layer, seg_cols):
    m, k = a.shape
    n_out = sum(w for _, w in seg_cols)
    tn = 1024
    while any(w % tn for _, w in seg_cols):
        tn //= 2
    assert tn >= HEAD_DIM
    tm = _tile(m, 2048)
    segments, out0 = [], 0
    for src0, width in seg_cols:
        shift = src0 - out0
        assert 0 <= shift < HEAD_DIM
        segments.append((out0 // tn, (out0 + width) // tn, shift))
        out0 += width
    kern = functools.partial(_in_proj_kernel, segments=tuple(segments))
    nb = tn // HEAD_DIM
    last_blk = (w_in_t.shape[1] - 1) // HEAD_DIM
    return pl.pallas_call(
        kern,
        grid=(n_out // tn, m // tm),
        in_specs=[pl.BlockSpec((tm, k), lambda j, i: (i, 0)),
                  pl.BlockSpec((None, tn, k), lambda j, i: (layer, j, 0)),
                  pl.BlockSpec((None, HEAD_DIM, k),
                               lambda j, i: (layer, jnp.minimum((j + 1) * nb, last_blk), 0))],
        out_specs=pl.BlockSpec((tm, tn), lambda j, i: (i, j)),
        out_shape=jax.ShapeDtypeStruct((m, n_out), BF16),
        scratch_shapes=[pltpu.VMEM((tn, k), BF16)],
        compiler_params=pltpu.CompilerParams(dimension_semantics=("parallel", "arbitrary"),
                                             vmem_limit_bytes=IN_PROJ_VMEM_BYTES),
        name="in_proj",
    )(a, w_in_t, w_in_t)


def _split3(x):
    hi = x.astype(BF16)
    r1 = x - hi.astype(F32)
    mid = r1.astype(BF16)
    lo = (r1 - mid.astype(F32)).astype(BF16)
    return hi, mid, lo


def _gate_act(x, bias, mult, idx, hv, hf):
    xb = x + bias
    e = jnp.exp(-jnp.abs(xb))
    l1p = jnp.log(1.0 + e)
    sig = jnp.where(xb >= 0, 1.0, e) / (1.0 + e)
    softplus = jnp.maximum(xb, 0.0) + l1p
    logsig = jnp.minimum(xb, 0.0) - l1p
    return jnp.where(idx < hv, sig,
                     jnp.where(idx < 2 * hv, mult * softplus,
                               jnp.where(idx < 2 * hv + hf, logsig, 0.0)))


def _gates_kernel(h_ref, wa_ref, wb_ref, pcol_ref, rows_ref, carry_ref, *, hv, hf, chunk):
    t = pl.program_id(1)
    tm = h_ref.shape[0]

    @pl.when(t == 0)
    def _():
        carry_ref[...] = jnp.zeros_like(carry_ref)

    row_w = lax.broadcasted_iota(jnp.int32, wa_ref.shape, 0)
    w = jnp.where(row_w < 2 * hv, wa_ref[...], jnp.where(row_w < 2 * hv + hf, wb_ref[...], 0.0))
    g_cols = _dot_nt(h_ref[...], w)

    lane = lax.broadcasted_iota(jnp.int32, (tm, GATE_LANES), 1)
    val_c = _gate_act(g_cols, pcol_ref[0:1, :], pcol_ref[1:2, :], lane, hv, hf)
    sub = lax.broadcasted_iota(jnp.int32, (GATE_LANES, tm), 0)
    val_r = val_c.T

    r = lax.broadcasted_iota(jnp.int32, (tm, tm), 0)
    c = lax.broadcasted_iota(jnp.int32, (tm, tm), 1)
    sh = chunk.bit_length() - 1
    same = (r >> sh) == (c >> sh)
    triu_blk = jnp.where((r <= c) & same, 1.0, 0.0).astype(BF16)
    triu_all = jnp.where(r <= c, 1.0, 0.0).astype(BF16)

    pieces_r = _split3(val_r)
    cum_r_blk = sum(jnp.dot(p, triu_blk, preferred_element_type=F32) for p in pieces_r)
    cum_r_all = sum(jnp.dot(p, triu_all, preferred_element_type=F32) for p in pieces_r)
    cum_r_all = cum_r_all + carry_ref[:, 0:1]

    is_decay_r = (sub >= hv) & (sub < 2 * hv)
    is_forget_r = (sub >= 2 * hv) & (sub < 2 * hv + hf)
    rows = jnp.where(is_decay_r, cum_r_blk, jnp.where(is_forget_r, cum_r_all, val_r))
    rows_ref[0] = rows
    carry_ref[...] = jnp.broadcast_to(cum_r_all[:, tm - 1:tm], carry_ref.shape)


def _gates(h2, w_in, layer, col_ba, col_f, pcol, batch, seq, hv, hf):
    m, d = h2.shape
    tm = _tile(seq, 512)
    per_b = seq // tm
    assert col_ba % GATE_LANES == 0 and col_f % GATE_LANES == 2 * hv
    blk_ba, blk_f = col_ba // GATE_LANES, col_f // GATE_LANES
    kern = functools.partial(_gates_kernel, hv=hv, hf=hf, chunk=DN_CHUNK)
    return pl.pallas_call(
        kern,
        grid=(batch, per_b),
        in_specs=[pl.BlockSpec((tm, d), lambda b, t: (b * per_b + t, 0)),
                  pl.BlockSpec((None, GATE_LANES, d), lambda b, t: (layer, blk_ba, 0)),
                  pl.BlockSpec((None, GATE_LANES, d), lambda b, t: (layer, blk_f, 0)),
                  pl.BlockSpec((2, GATE_LANES), lambda b, t: (0, 0))],
        out_specs=pl.BlockSpec((1, GATE_LANES, tm), lambda b, t: (b, 0, t)),
        out_shape=jax.ShapeDtypeStruct((batch, GATE_LANES, seq), F32),
        scratch_shapes=[pltpu.VMEM((GATE_LANES, GATE_LANES), F32)],
        compiler_params=_params("parallel", "arbitrary"),
        name="gates",
    )(h2, w_in, w_in, pcol)


def _causal_conv(u, tail, w):
    out = u * w[CONV_K - 1:CONV_K, :]
    row8 = lax.broadcasted_iota(jnp.int32, tail.shape, 0)
    for s in range(1, CONV_K):
        rolled = pltpu.roll(u, s, 0)
        head = jnp.where(row8 < s, pltpu.roll(tail, s, 0), rolled[0:8])
        shifted = jnp.concatenate([head, rolled[8:]], axis=0)
        out = out + shifted * w[CONV_K - 1 - s:CONV_K - s, :]
    return out


INV_LEVELS = (DN_CHUNK // 8).bit_length() - 1
MASK_DIAG8, MASK_EYE, MASK_STRICT = 0, INV_LEVELS + 1, INV_LEVELS + 2
N_MASKS = INV_LEVELS + 3


def _inverse_masks(n):
    r = lax.broadcasted_iota(jnp.int32, (n, n), 0)
    c = lax.broadcasted_iota(jnp.int32, (n, n), 1)
    masks = [(r >> 3) == (c >> 3)]
    for sh in range(3, 3 + INV_LEVELS):
        masks.append(((r >> (sh + 1)) == (c >> (sh + 1))) & ((r >> sh) == (c >> sh) + 1))
    masks += [r == c, r > c]
    return [jnp.where(m, 1.0, 0.0) for m in masks]


def _unit_lower_inverses(lms, mask_ref, maskb_ref):
    n0s = [-(lm * mask_ref[MASK_DIAG8]) for lm in lms]
    n2s = [_dot(n0, n0) for n0 in n0s]
    n4s = [_dot(n2, n2) for n2 in n2s]
    ps = [mask_ref[MASK_EYE] + n0 for n0 in n0s]
    ps = [p + _dot(p, n2) for p, n2 in zip(ps, n2s)]
    ps = [p + _dot(p, n4) for p, n4 in zip(ps, n4s)]
    lms_b = [lm.astype(BF16) for lm in lms]
    for level in range(1, INV_LEVELS + 1):
        xs = [_dot(lm_b * maskb_ref[level], p) for lm_b, p in zip(lms_b, ps)]
        ps = [p - _dot(p, x) for p, x in zip(ps, xs)]
    return ps


def _deltanet_kernel(q_ref, k_ref, v_ref, z_ref, wq_ref, wk_ref, wv_ref, rows_ref, nw_ref,
                     o_ref, s_ref, qt_ref, kt_ref, vt_ref, mask_ref, maskb_ref, u_ref, lhs1_ref, lhs2_ref,
                     egl_ref,
                     *, hv, rep, groups, unroll, scan_unroll):
    i = pl.program_id(1)
    tb = pl.program_id(2)
    rows_blk = q_ref.shape[0]
    n_chunks = rows_blk // DN_CHUNK
    c_ = DN_CHUNK
    d = HEAD_DIM
    nh = groups * rep

    @pl.when(tb == 0)
    def _():
        s_ref[...] = jnp.zeros_like(s_ref)
        qt_ref[...] = jnp.zeros_like(qt_ref)
        kt_ref[...] = jnp.zeros_like(kt_ref)
        vt_ref[...] = jnp.zeros_like(vt_ref)
        for j, m in enumerate(_inverse_masks(c_)):
            mask_ref[j] = m
            maskb_ref[j] = m.astype(BF16)

    r = lax.broadcasted_iota(jnp.int32, (c_, c_), 0)
    c = lax.broadcasted_iota(jnp.int32, (c_, c_), 1)
    nw = nw_ref[...]

    def tail_of(ref, tail_ref, ci, off):
        prev = pl.multiple_of(jnp.maximum(off - 16, 0), 16)
        inside = ref[pl.ds(prev, 16), :].astype(F32)[8:16]
        return jnp.where(ci == 0, tail_ref[...], inside)

    def prepare_body(j, carry):
        chains = []
        for uu in range(unroll):
            ci = j * unroll + uu
            off = pl.multiple_of(ci * c_, c_)
            sl = pl.ds(off, c_)
            q_raw = q_ref[sl, :].astype(F32)
            k_raw = k_ref[sl, :].astype(F32)
            v_raw = v_ref[sl, :].astype(F32)
            qc = _causal_conv(q_raw, tail_of(q_ref, qt_ref, ci, off), wq_ref[...])
            kc = _causal_conv(k_raw, tail_of(k_ref, kt_ref, ci, off), wk_ref[...])
            vc = _causal_conv(v_raw, tail_of(v_ref, vt_ref, ci, off), wv_ref[...])
            qc = qc * _sigmoid(qc)
            kc = kc * _sigmoid(kc)
            vc = vc * _sigmoid(vc)
            row_b = pl.multiple_of(((i * nh) // 8) * 8, 8)
            row_g = pl.multiple_of(((hv + i * nh) // 8) * 8, 8)
            rows8_b = rows_ref[0, pl.ds(row_b, 8), sl]
            rows8_g = rows_ref[0, pl.ds(row_g, 8), sl]
            sub8 = lax.broadcasted_iota(jnp.int32, (8, c_), 0)
            for g in range(groups):
                qg_ = qc[:, g * d:(g + 1) * d]
                kg_ = kc[:, g * d:(g + 1) * d]
                qn = qg_ * (lax.rsqrt(jnp.sum(qg_ * qg_, axis=-1, keepdims=True) + EPS) * (d ** -0.5))
                kn = kg_ * lax.rsqrt(jnp.sum(kg_ * kg_, axis=-1, keepdims=True) + EPS)
                kk = _dot_nt(kn, kn)
                qk = _dot_nt(qn, kn)
                for hh in range(rep):
                    hl = g * rep + hh
                    head = i * nh + hl
                    beta_r = jnp.sum(jnp.where(sub8 == head % 8, rows8_b, 0.0), axis=0, keepdims=True)
                    gc_r = jnp.sum(jnp.where(sub8 == (hv + head) % 8, rows8_g, 0.0), axis=0, keepdims=True)
                    beta_c = jnp.broadcast_to(beta_r, (c_, c_)).T
                    gc_c = jnp.broadcast_to(gc_r, (c_, c_)).T
                    g_last = gc_r[:, c_ - 1:c_]
                    decay = jnp.exp(jnp.where(r >= c, gc_c - gc_r, -1e30))
                    lm = (beta_c * kk) * (decay * mask_ref[MASK_STRICT])
                    eg = jnp.exp(gc_c)
                    v_h = vc[:, hl * d:(hl + 1) * d]
                    rhs = jnp.concatenate([v_h * beta_c, kn * (beta_c * eg)], axis=1).astype(BF16)
                    kd = kn * jnp.exp(g_last - gc_c)
                    lhs2_ref[hl, ci] = jnp.concatenate([qk * decay, kd.T], axis=0).astype(BF16)
                    egl_ref[hl, ci] = jnp.broadcast_to(jnp.exp(g_last), (8, d))
                    chains.append((hl, ci, sl, lm, rhs, (qn * eg).astype(BF16)))
        tinvs = _unit_lower_inverses([ch[3] for ch in chains], mask_ref, maskb_ref)
        uws = [_dot(tinv, ch[4]) for tinv, ch in zip(tinvs, chains)]
        for uw, (hl, ci, sl, _, _, qg) in zip(uws, chains):
            u_ref[hl, sl, :] = uw[:, :d]
            lhs1_ref[hl, ci] = jnp.concatenate([uw[:, d:].astype(BF16), qg], axis=0)
        return carry

    lax.fori_loop(0, n_chunks // unroll, prepare_body, 0)

    heads = range(nh)

    def scan_chunk(ci):
        off = pl.multiple_of(ci * c_, c_)
        sl = pl.ds(off, c_)
        states = [s_ref[hl] for hl in heads]
        ws_qs = [_dot(lhs1_ref[hl, ci], states[hl]) for hl in heads]
        v_new = [u_ref[hl, sl, :] - ws_qs[hl][:c_] for hl in heads]
        av_kv = [_dot(lhs2_ref[hl, ci], v_new[hl]) for hl in heads]
        for hl in heads:
            s_ref[hl] = states[hl] * egl_ref[hl, ci, 0:1, :] + av_kv[hl][c_:]
        for hl in heads:
            o = ws_qs[hl][c_:] + av_kv[hl][:c_]
            z = z_ref[sl, hl * d:(hl + 1) * d].astype(F32)
            out = _rms(o) * nw * (z * _sigmoid(z))
            o_ref[sl, hl * d:(hl + 1) * d] = out.astype(o_ref.dtype)

    def scan_body(j, carry):
        for uu in range(scan_unroll):
            scan_chunk(j * scan_unroll + uu)
        return carry

    lax.fori_loop(0, n_chunks // scan_unroll, scan_body, 0)

    qt_ref[...] = q_ref[rows_blk - 16:, :].astype(F32)[8:16]
    kt_ref[...] = k_ref[rows_blk - 16:, :].astype(F32)[8:16]
    vt_ref[...] = v_ref[rows_blk - 16:, :].astype(F32)[8:16]


def _deltanet(proj, conv_wt, rows, norm_w, batch, seq, hqk, hv):
    rep = hv // hqk
    assert hv == rep * hqk and DN_CHUNK == HEAD_DIM
    groups = next(g for g in (4, 2, 1) if hqk % g == 0 and 8 % (g * rep) == 0)
    nh = groups * rep
    assert 8 % nh == 0 and hv % nh == 0 and (2 * hqk) % nh == 0
    tb = _tile(seq, 1024)
    per_b = seq // tb
    wqk, wv = groups * HEAD_DIM, nh * HEAD_DIM
    nqk_blk, nv_blk = hqk // groups, hv // nh
    k_blk0 = nqk_blk
    v_blk0 = (2 * hqk) // nh
    z_blk0 = v_blk0 + nv_blk
    row = lambda b, i, t: b * per_b + t
    n_chunks = tb // DN_CHUNK
    unroll = next(u for u in (4, 2, 1) if n_chunks % u == 0 and u * nh <= 16)
    scan_unroll = next(u for u in (4, 2, 1) if n_chunks % u == 0)
    kern = functools.partial(_deltanet_kernel, hv=hv, rep=rep, groups=groups, unroll=unroll,
                             scan_unroll=scan_unroll)
    return pl.pallas_call(
        kern,
        grid=(batch, nqk_blk, per_b),
        in_specs=[pl.BlockSpec((tb, wqk), lambda b, i, t: (row(b, i, t), i)),
                  pl.BlockSpec((tb, wqk), lambda b, i, t: (row(b, i, t), k_blk0 + i)),
                  pl.BlockSpec((tb, wv), lambda b, i, t: (row(b, i, t), v_blk0 + i)),
                  pl.BlockSpec((tb, wv), lambda b, i, t: (row(b, i, t), z_blk0 + i)),
                  pl.BlockSpec((CONV_K, wqk), lambda b, i, t: (0, i)),
                  pl.BlockSpec((CONV_K, wqk), lambda b, i, t: (0, k_blk0 + i)),
                  pl.BlockSpec((CONV_K, wv), lambda b, i, t: (0, v_blk0 + i)),
                  pl.BlockSpec((1, GATE_LANES, tb), lambda b, i, t: (b, 0, t)),
                  pl.BlockSpec((1, HEAD_DIM), lambda b, i, t: (0, 0))],
        out_specs=pl.BlockSpec((tb, wv), lambda b, i, t: (row(b, i, t), i)),
        out_shape=jax.ShapeDtypeStruct((batch * seq, hv * HEAD_DIM), BF16),
        scratch_shapes=[pltpu.VMEM((nh, HEAD_DIM, HEAD_DIM), F32),
                        pltpu.VMEM((8, wqk), F32),
                        pltpu.VMEM((8, wqk), F32),
                        pltpu.VMEM((8, wv), F32),
                        pltpu.VMEM((N_MASKS, DN_CHUNK, DN_CHUNK), F32),
                        pltpu.VMEM((N_MASKS, DN_CHUNK, DN_CHUNK), BF16),
                        pltpu.VMEM((nh, tb, HEAD_DIM), F32),
                        pltpu.VMEM((nh, n_chunks, 2 * DN_CHUNK, HEAD_DIM), BF16),
                        pltpu.VMEM((nh, n_chunks, 2 * DN_CHUNK, DN_CHUNK), BF16),
                        pltpu.VMEM((nh, n_chunks, 8, HEAD_DIM), F32)],
        compiler_params=_params("parallel", "parallel", "arbitrary"),
        name="deltanet",
    )(proj, proj, proj, proj, conv_wt, conv_wt, conv_wt, rows, norm_w.reshape(1, HEAD_DIM))


LOG2E = 1.4426950408889634
MXU_COLS = 256
ONES_ROWS = 16


def _reduce_rows(x, op):
    rows, lanes = x.shape
    slabs = 8
    if rows % (8 * slabs) == 0:
        x3 = x.reshape(slabs, rows // slabs, lanes)
        x = x3[0]
        for j in range(1, slabs):
            x = op(x, x3[j])
    final = jnp.max if op is jnp.maximum else jnp.sum
    return final(x, axis=0, keepdims=True)


def _fox_kernel(q_ref, k_ref, v_ref, f_ref, o_ref, vt_ref, fcol_ref, s_ref, m_ref, acc_ref, *, scale):
    qi = pl.program_id(2)
    tq = q_ref.shape[0]
    tk = tq
    seq = k_ref.shape[0]
    qt = min(MXU_COLS, tq)
    d = HEAD_DIM
    nhd = q_ref.shape[1] // d
    hd_lanes = lambda hd: slice(hd * d, (hd + 1) * d)

    @pl.when(qi == 0)
    def _():
        n_blk = seq // d
        group = next(g for g in (4, 2, 1) if n_blk % g == 0)

        def prep(bi, carry):
            for j in range(group):
                off = pl.multiple_of((bi * group + j) * d, d)
                for hd in range(nhd):
                    v_blk = v_ref[pl.ds(off, d), hd_lanes(hd)]
                    vt_ref[hd, 0:d, pl.ds(off, d)] = v_blk.astype(F32).T.astype(BF16)
                    frow = f_ref[0, hd, :, pl.ds(off, d)] * LOG2E
                    fcol_ref[hd, pl.ds(off, d), :] = jnp.broadcast_to(frow, (d, d)).T
            return carry
        lax.fori_loop(0, n_blk // group, prep, 0)
        for hd in range(nhd):
            vt_ref[hd, d:, :] = jnp.ones((ONES_ROWS, seq), BF16)

    m_ref[...] = jnp.full_like(m_ref, -1e30)
    acc_ref[...] = jnp.zeros_like(acc_ref)

    all_heads = tuple(range(nhd))

    def scores(kj, slot, heads=all_heads):
        off = pl.multiple_of(kj * tk, tk)
        for hd in heads:
            s_ref[slot, hd] = _dot_nt(k_ref[pl.ds(off, tk), hd_lanes(hd)], q_ref[:, hd_lanes(hd)])

    def consume(kj, slot, masked, heads=all_heads):
        off = pl.multiple_of(kj * tk, tk)
        for hd in heads:
            fcol = fcol_ref[hd, pl.ds(off, tk), :]
            fcol = jnp.concatenate([fcol] * (qt // d), axis=1)
            vt = vt_ref[hd, :, pl.ds(off, tk)]
            for t in range(tq // qt):
                lanes = slice(t * qt, (t + 1) * qt)
                nk = min((t + 1) * qt, tk) if masked else tk
                s = s_ref[slot, hd, 0:nk, lanes] * (scale * LOG2E) - fcol[0:nk]
                if masked:
                    key = lax.broadcasted_iota(jnp.int32, (nk, qt), 0)
                    qry = lax.broadcasted_iota(jnp.int32, (nk, qt), 1) + t * qt
                    s = jnp.where(key <= qry, s, -1e30)
                m_prev = m_ref[hd, :, lanes]
                m_new = jnp.maximum(m_prev, _reduce_rows(s, jnp.maximum))
                p = jnp.exp2(s - m_new)
                alpha = jnp.exp2(m_prev - m_new)
                m_ref[hd, :, lanes] = m_new
                acc_ref[hd, :, lanes] = alpha * acc_ref[hd, :, lanes] + _dot(vt[:, 0:nk], p)

    scores(0, 0)

    def body(i, carry):
        kj = 2 * i
        for hd in all_heads:
            scores(kj + 1, 1, (hd,))
            consume(kj, 0, False, (hd,))
        for hd in all_heads:
            scores(kj + 2, 0, (hd,))
            consume(kj + 1, 1, False, (hd,))
        return carry

    lax.fori_loop(0, qi // 2, body, 0)

    @pl.when(qi % 2 == 0)
    def _():
        consume(qi, 0, True)

    @pl.when(qi % 2 == 1)
    def _():
        scores(qi, 1)
        consume(qi - 1, 0, False)
        consume(qi, 1, True)

    for hd in range(nhd):
        out = (acc_ref[hd, 0:d, :] / acc_ref[hd, d:d + 1, :]).T
        o_ref[:, hd_lanes(hd)] = out.astype(o_ref.dtype)


def _fox(proj, f_rows, batch, seq, base, hf):
    tq = _tile(seq, 512)
    nq = seq // tq
    nhd = next(n for n in (4, 2, 1) if hf % n == 0 and base % n == 0)
    w = nhd * HEAD_DIM
    b0, nblk = base // nhd, hf // nhd
    kern = functools.partial(_fox_kernel, scale=HEAD_DIM ** -0.5)
    return pl.pallas_call(
        kern,
        grid=(batch, nblk, nq),
        in_specs=[pl.BlockSpec((tq, w), lambda b, h, qi: (b * nq + qi, b0 + h)),
                  pl.BlockSpec((seq, w), lambda b, h, qi: (b, b0 + nblk + h)),
                  pl.BlockSpec((seq, w), lambda b, h, qi: (b, b0 + 2 * nblk + h)),
                  pl.BlockSpec((1, nhd, 1, seq), lambda b, h, qi: (b, h, 0, 0))],
        out_specs=pl.BlockSpec((tq, w), lambda b, h, qi: (b * nq + qi, h)),
        out_shape=jax.ShapeDtypeStruct((batch * seq, hf * HEAD_DIM), BF16),
        scratch_shapes=[pltpu.VMEM((nhd, HEAD_DIM + ONES_ROWS, seq), BF16),
                        pltpu.VMEM((nhd, seq, HEAD_DIM), F32),
                        pltpu.VMEM((2, nhd, tq, tq), F32),
                        pltpu.VMEM((nhd, 1, tq), F32),
                        pltpu.VMEM((nhd, HEAD_DIM + ONES_ROWS, tq), F32)],
        compiler_params=_params("parallel", "parallel", "arbitrary"),
        name="fox_attention",
    )(proj, proj, proj, f_rows)


def _merge_kernel(a1_ref, w1_ref, a2_ref, w2_ref, m1_ref, m2_ref, o_ref):
    y1 = _dot(a1_ref[...], w1_ref[...])
    y2 = _dot(a2_ref[...], w2_ref[...])
    g1 = _sigmoid(m1_ref[...].astype(F32))
    g2 = _sigmoid(m2_ref[...].astype(F32))
    o_ref[...] = (g1 * y1 + g2 * y2).astype(o_ref.dtype)


def _merge(o_dn, w_dn, o_fox, w_fox, layer, proj, merge_base_cols):
    m, k1 = o_dn.shape
    k2 = o_fox.shape[1]
    d = w_dn.shape[2]
    tm, tn = _tile(m, 512), _tile(d, 512)
    assert merge_base_cols % tn == 0
    mb = merge_base_cols // tn
    return pl.pallas_call(
        _merge_kernel,
        grid=(m // tm, d // tn),
        in_specs=[pl.BlockSpec((tm, k1), lambda i, j: (i, 0)),
                  pl.BlockSpec((None, k1, tn), lambda i, j: (layer, 0, j)),
                  pl.BlockSpec((tm, k2), lambda i, j: (i, 0)),
                  pl.BlockSpec((None, k2, tn), lambda i, j: (layer, 0, j)),
                  pl.BlockSpec((tm, tn), lambda i, j: (i, mb + j)),
                  pl.BlockSpec((tm, tn), lambda i, j: (i, mb + d // tn + j))],
        out_specs=pl.BlockSpec((tm, tn), lambda i, j: (i, j)),
        out_shape=jax.ShapeDtypeStruct((m, d), BF16),
        compiler_params=_params("parallel", "parallel"),
        name="branch_merge",
    )(o_dn, w_dn, o_fox, w_fox, proj, proj)


def _cast_weight_once(w_ref, wb_ref):
    @pl.when(pl.program_id(1) == 0)
    def _():
        k = w_ref.shape[0]
        for r0 in range(0, k, W_ROWS):
            r1 = min(r0 + W_ROWS, k)
            wb_ref[r0:r1, :] = w_ref[r0:r1, :].astype(BF16)


def _mm_f32_kernel(a_ref, w_ref, o_ref, wb_ref):
    _cast_weight_once(w_ref, wb_ref)
    o_ref[...] = _dot(a_ref[...], wb_ref[...])


def _out_proj(a, w, layer):
    m, k = a.shape
    d = w.shape[2]
    pref = 512 if k > 4096 else 1024
    tm, tn = _tile(m, pref), _tile(d, pref)
    return pl.pallas_call(
        _mm_f32_kernel,
        grid=(d // tn, m // tm),
        in_specs=[pl.BlockSpec((tm, k), lambda j, i: (i, 0)),
                  pl.BlockSpec((None, k, tn), lambda j, i: (layer, 0, j))],
        out_specs=pl.BlockSpec((tm, tn), lambda j, i: (i, j)),
        out_shape=jax.ShapeDtypeStruct((m, d), F32),
        scratch_shapes=[pltpu.VMEM((k, tn), BF16)],
        compiler_params=_params("parallel", "arbitrary"),
        name="out_proj",
    )(a, w)


def _res_norm_kernel(y_ref, x_ref, g_ref, gate_ref, *rest, emit_h):
    x_new = x_ref[...] + gate_ref[0] * (_rms(y_ref[...]) * g_ref[...])
    if emit_h:
        g2_ref, sc_ref, sh_ref, o_ref, h_ref = rest
        o_ref[...] = x_new
        h_ref[...] = ((_rms(x_new) * g2_ref[...]) * (1.0 + sc_ref[0]) + sh_ref[0]).astype(h_ref.dtype)
    else:
        (o_ref,) = rest
        o_ref[...] = x_new


def _residual_norm(y, x2, gain, mod3, gate_idx, seq, nxt=None):
    m, d = x2.shape
    tm = _tile(seq, 256)
    per_b = seq // tm
    row = pl.BlockSpec((tm, d), lambda i: (i, 0))
    vec = pl.BlockSpec((1, d), lambda i: (0, 0))
    modv = lambda idx: pl.BlockSpec((1, 1, d), lambda i: ((i // per_b) * 6 + idx, 0, 0))
    in_specs = [row, row, vec, modv(gate_idx)]
    args = [y, x2, gain.reshape(1, d), mod3]
    out_specs, out_shape = [row], [jax.ShapeDtypeStruct((m, d), F32)]
    if nxt is not None:
        gain2, mod3n, sc_idx, sh_idx = nxt
        in_specs += [vec, modv(sc_idx), modv(sh_idx)]
        args += [gain2.reshape(1, d), mod3n, mod3n]
        out_specs.append(row)
        out_shape.append(jax.ShapeDtypeStruct((m, d), BF16))
    out = pl.pallas_call(
        functools.partial(_res_norm_kernel, emit_h=nxt is not None),
        grid=(m // tm,),
        in_specs=in_specs,
        out_specs=out_specs,
        out_shape=out_shape,
        compiler_params=_params("parallel"),
        name="residual_norm",
    )(*args)
    return (out[0], out[1]) if nxt is not None else (out[0], None)


def _glu_kernel(a_ref, wg_ref, wu_ref, o_ref, wgb_ref, wub_ref):
    _cast_weight_once(wg_ref, wgb_ref)
    _cast_weight_once(wu_ref, wub_ref)
    a = a_ref[...]
    g = _dot(a, wgb_ref[...])
    u = _dot(a, wub_ref[...])
    o_ref[...] = (g * _sigmoid(g) * u).astype(o_ref.dtype)


def _glu(a, wg, wu, layer):
    m, k = a.shape
    n = wg.shape[2]
    tm, tn = _tile(m, 1024), _tile(n, 512)
    return pl.pallas_call(
        _glu_kernel,
        grid=(n // tn, m // tm),
        in_specs=[pl.BlockSpec((tm, k), lambda j, i: (i, 0)),
                  pl.BlockSpec((None, k, tn), lambda j, i: (layer, 0, j)),
                  pl.BlockSpec((None, k, tn), lambda j, i: (layer, 0, j))],
        out_specs=pl.BlockSpec((tm, tn), lambda j, i: (i, j)),
        out_shape=jax.ShapeDtypeStruct((m, n), BF16),
        scratch_shapes=[pltpu.VMEM((k, tn), BF16), pltpu.VMEM((k, tn), BF16)],
        compiler_params=_params("parallel", "arbitrary"),
        name="swiglu_up",
    )(a, wg, wu)


def kernel(x, c, w_ada, b_ada, norm_gains, w_in, dn_conv, dn_a_log, dn_dt_bias, dn_norm_w, fox_f_bias,
           w_branch_dn, w_branch_fox, w_out, w_gate, w_up, w_down):
    batch, seq, d = x.shape
    depth = w_ada.shape[0]
    hv = dn_a_log.shape[1]
    hf = fox_f_bias.shape[1]
    v_dim = hv * HEAD_DIM
    conv_dim = dn_conv.shape[1]
    qk_dim = (conv_dim - v_dim) // 2
    hqk = qk_dim // HEAD_DIM
    fox_dim = hf * HEAD_DIM
    assert 2 * hv + hf <= GATE_LANES

    o_z = conv_dim
    o_b = o_z + v_dim
    o_a = o_b + hv
    o_fq = o_a + hv
    o_ff = o_fq + 3 * fox_dim
    o_mg = o_ff + hf
    fox_base = (conv_dim + v_dim) // HEAD_DIM
    merge_base = conv_dim + v_dim + 3 * fox_dim

    mod = _modulation(c, w_ada, b_ada)
    x2 = x.reshape(batch * seq, d)
    pad = GATE_LANES - (2 * hv + hf)
    zpad = jnp.zeros((pad,), F32)

    main_cols = ((0, o_b), (o_fq, 3 * fox_dim), (o_mg, w_in.shape[2] - o_mg))
    w_in_t = jnp.swapaxes(w_in, 1, 2)
    w_dn_all, w_fox_all = w_branch_dn.astype(BF16), w_branch_fox.astype(BF16)
    mod3s = [mod[l].reshape(batch * 6, 1, d) for l in range(depth)]

    h = _norm_mod(x2, norm_gains[0, 0], mod3s[0], 1, 0, seq)
    for l in range(depth):
        mod3 = mod3s[l]
        bias = jnp.concatenate([jnp.zeros((hv,), F32), dn_dt_bias[l], fox_f_bias[l], zpad])
        mult = jnp.concatenate([jnp.ones((hv,), F32), -jnp.exp(dn_a_log[l]), jnp.ones((hf,), F32), zpad])
        pcol = jnp.stack([bias, mult], axis=0)

        proj = _in_proj(h, w_in_t, l, main_cols)
        rows = _gates(h, w_in_t, l, o_b, o_ff, pcol, batch, seq, hv, hf)
        o_dn = _deltanet(proj, dn_conv[l].T, rows, dn_norm_w[l], batch, seq, hqk, hv)
        f_rows = rows[:, 2 * hv:2 * hv + hf, :].reshape(batch, hf, 1, seq)
        o_fox = _fox(proj, f_rows, batch, seq, fox_base, hf)
        ymix = _merge(o_dn, w_dn_all, o_fox, w_fox_all, l, proj, merge_base)
        y = _out_proj(ymix, w_out, l)
        x2, h = _residual_norm(y, x2, norm_gains[l, 1], mod3, 2, seq,
                               nxt=(norm_gains[l, 2], mod3, 4, 3))

        gu = _glu(h, w_gate, w_up, l)
        y = _out_proj(gu, w_down, l)
        nxt = (norm_gains[l + 1, 0], mod3s[l + 1], 1, 0) if l + 1 < depth else None
        x2, h = _residual_norm(y, x2, norm_gains[l, 3], mod3, 5, seq, nxt=nxt)

    return x2.reshape(batch, seq, d)
```

```python
import functools

import jax
import jax.numpy as jnp
from jax import lax
from jax.experimental import pallas as pl
from jax.experimental.pallas import tpu as pltpu

EPS = 1e-6
HEAD_DIM = 128
CONV_K = 4
DN_CHUNK = 128
GATE_LANES = 128
VMEM_LIMIT_BYTES = 48 * 1024 * 1024

F32 = jnp.float32
BF16 = jnp.bfloat16


def _params(*sem):
    return pltpu.CompilerParams(dimension_semantics=sem, vmem_limit_bytes=VMEM_LIMIT_BYTES)


def _tile(dim, pref):
    if dim <= pref:
        return dim
    t = pref - pref % HEAD_DIM
    while dim % t:
        t -= HEAD_DIM
    assert t > 0, (dim, pref)
    return t


def _dot(a, b):
    return jnp.dot(a.astype(BF16), b.astype(BF16), preferred_element_type=F32)


def _dot_nt(a, b):
    return lax.dot_general(a.astype(BF16), b.astype(BF16), (((1,), (1,)), ((), ())),
                           preferred_element_type=F32)


def _sigmoid(x):
    return 0.5 * jnp.tanh(0.5 * x) + 0.5


def _rms(x):
    return x * lax.rsqrt(jnp.mean(x * x, axis=-1, keepdims=True) + EPS)


def _mod_kernel(c_ref, w_ref, b_ref, o_ref):
    c = c_ref[...]
    cond = c * _sigmoid(c)
    o_ref[0] = _dot(cond, w_ref[0]) + b_ref[0]


def _modulation(c, w_ada, b_ada):
    depth, d, n = w_ada.shape
    b = c.shape[0]
    rows = 16
    c_pad = jnp.zeros((rows, d), F32).at[:b].set(c)
    tn = _tile(n, 1024)
    out = pl.pallas_call(
        _mod_kernel,
        grid=(depth, n // tn),
        in_specs=[pl.BlockSpec((rows, d), lambda l, j: (0, 0)),
                  pl.BlockSpec((1, d, tn), lambda l, j: (l, 0, j)),
                  pl.BlockSpec((1, 1, tn), lambda l, j: (l, 0, j))],
        out_specs=pl.BlockSpec((1, rows, tn), lambda l, j: (l, 0, j)),
        out_shape=jax.ShapeDtypeStruct((depth, rows, n), F32),
        compiler_params=_params("parallel", "parallel"),
        name="adaln_mod",
    )(c_pad, w_ada, b_ada.reshape(depth, 1, n))
    return out[:, :b].reshape(depth, b, 6, d)


def _norm_mod_kernel(x_ref, g_ref, sc_ref, sh_ref, o_ref):
    y = _rms(x_ref[...]) * g_ref[...]
    o_ref[...] = (y * (1.0 + sc_ref[0]) + sh_ref[0]).astype(o_ref.dtype)


def _norm_mod(x2, gain, mod3, sc_idx, sh_idx, seq):
    m, d = x2.shape
    tm = _tile(seq, 512)
    per_b = seq // tm
    return pl.pallas_call(
        _norm_mod_kernel,
        grid=(m // tm,),
        in_specs=[pl.BlockSpec((tm, d), lambda i: (i, 0)),
                  pl.BlockSpec((1, d), lambda i: (0, 0)),
                  pl.BlockSpec((1, 1, d), lambda i: ((i // per_b) * 6 + sc_idx, 0, 0)),
                  pl.BlockSpec((1, 1, d), lambda i: ((i // per_b) * 6 + sh_idx, 0, 0))],
        out_specs=pl.BlockSpec((tm, d), lambda i: (i, 0)),
        out_shape=jax.ShapeDtypeStruct((m, d), BF16),
        compiler_params=_params("parallel"),
        name="norm_mod",
    )(x2, gain.reshape(1, d), mod3, mod3)


W_ROWS = 256
IN_PROJ_ROWS = 1024
IN_PROJ_VMEM_BYTES = 56 * 1024 * 1024


def _in_proj_kernel(a_ref, wa_ref, wb_ref, o_ref, w_ref, *, segments):
    j = pl.program_id(0)
    tn = wa_ref.shape[0]

    @pl.when(pl.program_id(1) == 0)
    def _():
        for lo, hi, shift in segments:
            @pl.when((j >= lo) & (j < hi))
            def _():
                for r0 in range(0, tn - shift, W_ROWS):
                    r1 = min(r0 + W_ROWS, tn - shift)
                    w_ref[r0:r1, :] = wa_ref[r0 + shift:r1 + shift, :].astype(BF16)
                if shift:
                    w_ref[tn - shift:, :] = wb_ref[:shift, :].astype(BF16)

    tm = a_ref.shape[0]
    for r0 in range(0, tm, IN_PROJ_ROWS):
        r1 = min(r0 + IN_PROJ_ROWS, tm)
        o_ref[r0:r1, :] = _dot_nt(a_ref[r0:r1, :], w_ref[...]).astype(o_ref.dtype)


def _in_proj(a, w_in_t, layer, seg_cols):
    m, k = a.shape
    n_out = sum(w for _, w in seg_cols)
    tn = 1024
    while any(w % tn for _, w in seg_cols):
        tn //= 2
    assert tn >= HEAD_DIM
    tm = _tile(m, 2048)
    segments, out0 = [], 0
    for src0, width in seg_cols:
        shift = src0 - out0
        assert 0 <= shift < HEAD_DIM
        segments.append((out0 // tn, (out0 + width) // tn, shift))
        out0 += width
    kern = functools.partial(_in_proj_kernel, segments=tuple(segments))
    nb = tn // HEAD_DIM
    last_blk = (w_in_t.shape[1] - 1) // HEAD_DIM
    return pl.pallas_call(
        kern,
        grid=(n_out // tn, m // tm),
        in_specs=[pl.BlockSpec((tm, k), lambda j, i: (i, 0)),
                  pl.BlockSpec((None, tn, k), lambda j, i: (layer, j, 0)),
                  pl.BlockSpec((None, HEAD_DIM, k),
                               lambda j, i: (layer, jnp.minimum((j + 1) * nb, last_blk), 0))],
        out_specs=pl.BlockSpec((tm, tn), lambda j, i: (i, j)),
        out_shape=jax.ShapeDtypeStruct((m, n_out), BF16),
        scratch_shapes=[pltpu.VMEM((tn, k), BF16)],
        compiler_params=pltpu.CompilerParams(dimension_semantics=("parallel", "arbitrary"),
                                             vmem_limit_bytes=IN_PROJ_VMEM_BYTES),
        name="in_proj",
    )(a, w_in_t, w_in_t)


def _split3(x):
    hi = x.astype(BF16)
    r1 = x - hi.astype(F32)
    mid = r1.astype(BF16)
    lo = (r1 - mid.astype(F32)).astype(BF16)
    return hi, mid, lo


def _gate_act(x, bias, mult, idx, hv, hf):
    xb = x + bias
    e = jnp.exp(-jnp.abs(xb))
    l1p = jnp.log(1.0 + e)
    sig = jnp.where(xb >= 0, 1.0, e) / (1.0 + e)
    softplus = jnp.maximum(xb, 0.0) + l1p
    logsig = jnp.minimum(xb, 0.0) - l1p
    return jnp.where(idx < hv, sig,
                     jnp.where(idx < 2 * hv, mult * softplus,
                               jnp.where(idx < 2 * hv + hf, logsig, 0.0)))


def _gates_kernel(h_ref, wa_ref, wb_ref, pcol_ref, rows_ref, carry_ref, *, hv, hf, chunk):
    t = pl.program_id(1)
    tm = h_ref.shape[0]

    @pl.when(t == 0)
    def _():
        carry_ref[...] = jnp.zeros_like(carry_ref)

    row_w = lax.broadcasted_iota(jnp.int32, wa_ref.shape, 0)
    w = jnp.where(row_w < 2 * hv, wa_ref[...], jnp.where(row_w < 2 * hv + hf, wb_ref[...], 0.0))
    g_cols = _dot_nt(h_ref[...], w)

    lane = lax.broadcasted_iota(jnp.int32, (tm, GATE_LANES), 1)
    val_c = _gate_act(g_cols, pcol_ref[0:1, :], pcol_ref[1:2, :], lane, hv, hf)
    sub = lax.broadcasted_iota(jnp.int32, (GATE_LANES, tm), 0)
    val_r = val_c.T

    r = lax.broadcasted_iota(jnp.int32, (tm, tm), 0)
    c = lax.broadcasted_iota(jnp.int32, (tm, tm), 1)
    sh = chunk.bit_length() - 1
    same = (r >> sh) == (c >> sh)
    triu_blk = jnp.where((r <= c) & same, 1.0, 0.0).astype(BF16)
    triu_all = jnp.where(r <= c, 1.0, 0.0).astype(BF16)

    pieces_r = _split3(val_r)
    cum_r_blk = sum(jnp.dot(p, triu_blk, preferred_element_type=F32) for p in pieces_r)
    cum_r_all = sum(jnp.dot(p, triu_all, preferred_element_type=F32) for p in pieces_r)
    cum_r_all = cum_r_all + carry_ref[:, 0:1]

    is_decay_r = (sub >= hv) & (sub < 2 * hv)
    is_forget_r = (sub >= 2 * hv) & (sub < 2 * hv + hf)
    rows = jnp.where(is_decay_r, cum_r_blk, jnp.where(is_forget_r, cum_r_all, val_r))
    rows_ref[0] = rows
    carry_ref[...] = jnp.broadcast_to(cum_r_all[:, tm - 1:tm], carry_ref.shape)


def _gates(h2, w_in, layer, col_ba, col_f, pcol, batch, seq, hv, hf):
    m, d = h2.shape
    tm = _tile(seq, 512)
    per_b = seq // tm
    assert col_ba % GATE_LANES == 0 and col_f % GATE_LANES == 2 * hv
    blk_ba, blk_f = col_ba // GATE_LANES, col_f // GATE_LANES
    kern = functools.partial(_gates_kernel, hv=hv, hf=hf, chunk=DN_CHUNK)
    return pl.pallas_call(
        kern,
        grid=(batch, per_b),
        in_specs=[pl.BlockSpec((tm, d), lambda b, t: (b * per_b + t, 0)),
                  pl.BlockSpec((None, GATE_LANES, d), lambda b, t: (layer, blk_ba, 0)),
                  pl.BlockSpec((None, GATE_LANES, d), lambda b, t: (layer, blk_f, 0)),
                  pl.BlockSpec((2, GATE_LANES), lambda b, t: (0, 0))],
        out_specs=pl.BlockSpec((1, GATE_LANES, tm), lambda b, t: (b, 0, t)),
        out_shape=jax.ShapeDtypeStruct((batch, GATE_LANES, seq), F32),
        scratch_shapes=[pltpu.VMEM((GATE_LANES, GATE_LANES), F32)],
        compiler_params=_params("parallel", "arbitrary"),
        name="gates",
    )(h2, w_in, w_in, pcol)


def _causal_conv(u, tail, w):
    out = u * w[CONV_K - 1:CONV_K, :]
    row8 = lax.broadcasted_iota(jnp.int32, tail.shape, 0)
    for s in range(1, CONV_K):
        rolled = pltpu.roll(u, s, 0)
        head = jnp.where(row8 < s, pltpu.roll(tail, s, 0), rolled[0:8])
        shifted = jnp.concatenate([head, rolled[8:]], axis=0)
        out = out + shifted * w[CONV_K - 1 - s:CONV_K - s, :]
    return out


INV_LEVELS = (DN_CHUNK // 8).bit_length() - 1
MASK_DIAG8, MASK_EYE, MASK_STRICT = 0, INV_LEVELS + 1, INV_LEVELS + 2
N_MASKS = INV_LEVELS + 3


def _inverse_masks(n):
    r = lax.broadcasted_iota(jnp.int32, (n, n), 0)
    c = lax.broadcasted_iota(jnp.int32, (n, n), 1)
    masks = [(r >> 3) == (c >> 3)]
    for sh in range(3, 3 + INV_LEVELS):
        masks.append(((r >> (sh + 1)) == (c >> (sh + 1))) & ((r >> sh) == (c >> sh) + 1))
    masks += [r == c, r > c]
    return [jnp.where(m, 1.0, 0.0) for m in masks]


def _unit_lower_inverses(lms, mask_ref, maskb_ref):
    n0s = [-(lm * mask_ref[MASK_DIAG8]) for lm in lms]
    n2s = [_dot(n0, n0) for n0 in n0s]
    n4s = [_dot(n2, n2) for n2 in n2s]
    ps = [mask_ref[MASK_EYE] + n0 for n0 in n0s]
    ps = [p + _dot(p, n2) for p, n2 in zip(ps, n2s)]
    ps = [p + _dot(p, n4) for p, n4 in zip(ps, n4s)]
    lms_b = [lm.astype(BF16) for lm in lms]
    for level in range(1, INV_LEVELS + 1):
        xs = [_dot(lm_b * maskb_ref[level], p) for lm_b, p in zip(lms_b, ps)]
        ps = [p - _dot(p, x) for p, x in zip(ps, xs)]
    return ps


def _deltanet_kernel(q_ref, k_ref, v_ref, z_ref, wq_ref, wk_ref, wv_ref, rows_ref, nw_ref,
                     o_ref, s_ref, qt_ref, kt_ref, vt_ref, mask_ref, maskb_ref, u_ref, lhs1_ref, lhs2_ref,
                     egl_ref,
                     *, hv, rep, groups, unroll, scan_unroll):
    i = pl.program_id(1)
    tb = pl.program_id(2)
    rows_blk = q_ref.shape[0]
    n_chunks = rows_blk // DN_CHUNK
    c_ = DN_CHUNK
    d = HEAD_DIM
    nh = groups * rep

    @pl.when(tb == 0)
    def _():
        s_ref[...] = jnp.zeros_like(s_ref)
        qt_ref[...] = jnp.zeros_like(qt_ref)
        kt_ref[...] = jnp.zeros_like(kt_ref)
        vt_ref[...] = jnp.zeros_like(vt_ref)
        for j, m in enumerate(_inverse_masks(c_)):
            mask_ref[j] = m
            maskb_ref[j] = m.astype(BF16)

    r = lax.broadcasted_iota(jnp.int32, (c_, c_), 0)
    c = lax.broadcasted_iota(jnp.int32, (c_, c_), 1)
    nw = nw_ref[...]

    def tail_of(ref, tail_ref, ci, off):
        prev = pl.multiple_of(jnp.maximum(off - 16, 0), 16)
        inside = ref[pl.ds(prev, 16), :].astype(F32)[8:16]
        return jnp.where(ci == 0, tail_ref[...], inside)

    def prepare_body(j, carry):
        chains = []
        for uu in range(unroll):
            ci = j * unroll + uu
            off = pl.multiple_of(ci * c_, c_)
            sl = pl.ds(off, c_)
            q_raw = q_ref[sl, :].astype(F32)
            k_raw = k_ref[sl, :].astype(F32)
            v_raw = v_ref[sl, :].astype(F32)
            qc = _causal_conv(q_raw, tail_of(q_ref, qt_ref, ci, off), wq_ref[...])
            kc = _causal_conv(k_raw, tail_of(k_ref, kt_ref, ci, off), wk_ref[...])
            vc = _causal_conv(v_raw, tail_of(v_ref, vt_ref, ci, off), wv_ref[...])
            qc = qc * _sigmoid(qc)
            kc = kc * _sigmoid(kc)
            vc = vc * _sigmoid(vc)
            row_b = pl.multiple_of(((i * nh) // 8) * 8, 8)
            row_g = pl.multiple_of(((hv + i * nh) // 8) * 8, 8)
            rows8_b = rows_ref[0, pl.ds(row_b, 8), sl]
            rows8_g = rows_ref[0, pl.ds(row_g, 8), sl]
            sub8 = lax.broadcasted_iota(jnp.int32, (8, c_), 0)
            for g in range(groups):
                qg_ = qc[:, g * d:(g + 1) * d]
                kg_ = kc[:, g * d:(g + 1) * d]
                qn = qg_ * (lax.rsqrt(jnp.sum(qg_ * qg_, axis=-1, keepdims=True) + EPS) * (d ** -0.5))
                kn = kg_ * lax.rsqrt(jnp.sum(kg_ * kg_, axis=-1, keepdims=True) + EPS)
                kk = _dot_nt(kn, kn)
                qk = _dot_nt(qn, kn)
                for hh in range(rep):
                    hl = g * rep + hh
                    head = i * nh + hl
                    beta_r = jnp.sum(jnp.where(sub8 == head % 8, rows8_b, 0.0), axis=0, keepdims=True)
                    gc_r = jnp.sum(jnp.where(sub8 == (hv + head) % 8, rows8_g, 0.0), axis=0, keepdims=True)
                    beta_c = jnp.broadcast_to(beta_r, (c_, c_)).T
                    gc_c = jnp.broadcast_to(gc_r, (c_, c_)).T
                    g_last = gc_r[:, c_ - 1:c_]
                    decay = jnp.exp(jnp.where(r >= c, gc_c - gc_r, -1e30))
                    lm = (beta_c * kk) * (decay * mask_ref[MASK_STRICT])
                    eg = jnp.exp(gc_c)
                    v_h = vc[:, hl * d:(hl + 1) * d]
                    rhs = jnp.concatenate([v_h * beta_c, kn * (beta_c * eg)], axis=1).astype(BF16)
                    kd = kn * jnp.exp(g_last - gc_c)
                    lhs2_ref[hl, ci] = jnp.concatenate([qk * decay, kd.T], axis=0).astype(BF16)
                    egl_ref[hl, ci] = jnp.broadcast_to(jnp.exp(g_last), (8, d))
                    chains.append((hl, ci, sl, lm, rhs, (qn * eg).astype(BF16)))
        tinvs = _unit_lower_inverses([ch[3] for ch in chains], mask_ref, maskb_ref)
        uws = [_dot(tinv, ch[4]) for tinv, ch in zip(tinvs, chains)]
        for uw, (hl, ci, sl, _, _, qg) in zip(uws, chains):
            u_ref[hl, sl, :] = uw[:, :d]
            lhs1_ref[hl, ci] = jnp.concatenate([uw[:, d:].astype(BF16), qg], axis=0)
        return carry

    lax.fori_loop(0, n_chunks // unroll, prepare_body, 0)

    heads = range(nh)

    def scan_chunk(ci):
        off = pl.multiple_of(ci * c_, c_)
        sl = pl.ds(off, c_)
        states = [s_ref[hl] for hl in heads]
        ws_qs = [_dot(lhs1_ref[hl, ci], states[hl]) for hl in heads]
        v_new = [u_ref[hl, sl, :] - ws_qs[hl][:c_] for hl in heads]
        av_kv = [_dot(lhs2_ref[hl, ci], v_new[hl]) for hl in heads]
        for hl in heads:
            s_ref[hl] = states[hl] * egl_ref[hl, ci, 0:1, :] + av_kv[hl][c_:]
        for hl in heads:
            o = ws_qs[hl][c_:] + av_kv[hl][:c_]
            z = z_ref[sl, hl * d:(hl + 1) * d].astype(F32)
            out = _rms(o) * nw * (z * _sigmoid(z))
            o_ref[sl, hl * d:(hl + 1) * d] = out.astype(o_ref.dtype)

    def scan_body(j, carry):
        for uu in range(scan_unroll):
            scan_chunk(j * scan_unroll + uu)
        return carry

    lax.fori_loop(0, n_chunks // scan_unroll, scan_body, 0)

    qt_ref[...] = q_ref[rows_blk - 16:, :].astype(F32)[8:16]
    kt_ref[...] = k_ref[rows_blk - 16:, :].astype(F32)[8:16]
    vt_ref[...] = v_ref[rows_blk - 16:, :].astype(F32)[8:16]


def _deltanet(proj, conv_wt, rows, norm_w, batch, seq, hqk, hv):
    rep = hv // hqk
    assert hv == rep * hqk and DN_CHUNK == HEAD_DIM
    groups = next(g for g in (4, 2, 1) if hqk % g == 0 and 8 % (g * rep) == 0)
    nh = groups * rep
    assert 8 % nh == 0 and hv % nh == 0 and (2 * hqk) % nh == 0
    tb = _tile(seq, 1024)
    per_b = seq // tb
    wqk, wv = groups * HEAD_DIM, nh * HEAD_DIM
    nqk_blk, nv_blk = hqk // groups, hv // nh
    k_blk0 = nqk_blk
    v_blk0 = (2 * hqk) // nh
    z_blk0 = v_blk0 + nv_blk
    row = lambda b, i, t: b * per_b + t
    n_chunks = tb // DN_CHUNK
    unroll = next(u for u in (4, 2, 1) if n_chunks % u == 0 and u * nh <= 16)
    scan_unroll = next(u for u in (4, 2, 1) if n_chunks % u == 0)
    kern = functools.partial(_deltanet_kernel, hv=hv, rep=rep, groups=groups, unroll=unroll,
                             scan_unroll=scan_unroll)
    return pl.pallas_call(
        kern,
        grid=(batch, nqk_blk, per_b),
        in_specs=[pl.BlockSpec((tb, wqk), lambda b, i, t: (row(b, i, t), i)),
                  pl.BlockSpec((tb, wqk), lambda b, i, t: (row(b, i, t), k_blk0 + i)),
                  pl.BlockSpec((tb, wv), lambda b, i, t: (row(b, i, t), v_blk0 + i)),
                  pl.BlockSpec((tb, wv), lambda b, i, t: (row(b, i, t), z_blk0 + i)),
                  pl.BlockSpec((CONV_K, wqk), lambda b, i, t: (0, i)),
                  pl.BlockSpec((CONV_K, wqk), lambda b, i, t: (0, k_blk0 + i)),
                  pl.BlockSpec((CONV_K, wv), lambda b, i, t: (0, v_blk0 + i)),
                  pl.BlockSpec((1, GATE_LANES, tb), lambda b, i, t: (b, 0, t)),
                  pl.BlockSpec((1, HEAD_DIM), lambda b, i, t: (0, 0))],
        out_specs=pl.BlockSpec((tb, wv), lambda b, i, t: (row(b, i, t), i)),
        out_shape=jax.ShapeDtypeStruct((batch * seq, hv * HEAD_DIM), BF16),
        scratch_shapes=[pltpu.VMEM((nh, HEAD_DIM, HEAD_DIM), F32),
                        pltpu.VMEM((8, wqk), F32),
                        pltpu.VMEM((8, wqk), F32),
                        pltpu.VMEM((8, wv), F32),
                        pltpu.VMEM((N_MASKS, DN_CHUNK, DN_CHUNK), F32),
                        pltpu.VMEM((N_MASKS, DN_CHUNK, DN_CHUNK), BF16),
                        pltpu.VMEM((nh, tb, HEAD_DIM), F32),
                        pltpu.VMEM((nh, n_chunks, 2 * DN_CHUNK, HEAD_DIM), BF16),
                        pltpu.VMEM((nh, n_chunks, 2 * DN_CHUNK, DN_CHUNK), BF16),
                        pltpu.VMEM((nh, n_chunks, 8, HEAD_DIM), F32)],
        compiler_params=_params("parallel", "parallel", "arbitrary"),
        name="deltanet",
    )(proj, proj, proj, proj, conv_wt, conv_wt, conv_wt, rows, norm_w.reshape(1, HEAD_DIM))


LOG2E = 1.4426950408889634
MXU_COLS = 256
ONES_ROWS = 16


def _reduce_rows(x, op):
    rows, lanes = x.shape
    slabs = 8
    if rows % (8 * slabs) == 0:
        x3 = x.reshape(slabs, rows // slabs, lanes)
        x = x3[0]
        for j in range(1, slabs):
            x = op(x, x3[j])
    final = jnp.max if op is jnp.maximum else jnp.sum
    return final(x, axis=0, keepdims=True)


def _fox_kernel(q_ref, k_ref, v_ref, f_ref, o_ref, vt_ref, fcol_ref, s_ref, m_ref, acc_ref, *, scale):
    qi = pl.program_id(2)
    tq = q_ref.shape[0]
    tk = tq
    seq = k_ref.shape[0]
    qt = min(MXU_COLS, tq)
    d = HEAD_DIM
    nhd = q_ref.shape[1] // d
    hd_lanes = lambda hd: slice(hd * d, (hd + 1) * d)

    @pl.when(qi == 0)
    def _():
        n_blk = seq // d
        group = next(g for g in (4, 2, 1) if n_blk % g == 0)

        def prep(bi, carry):
            for j in range(group):
                off = pl.multiple_of((bi * group + j) * d, d)
                for hd in range(nhd):
                    v_blk = v_ref[pl.ds(off, d), hd_lanes(hd)]
                    vt_ref[hd, 0:d, pl.ds(off, d)] = v_blk.astype(F32).T.astype(BF16)
                    frow = f_ref[0, hd, :, pl.ds(off, d)] * LOG2E
                    fcol_ref[hd, pl.ds(off, d), :] = jnp.broadcast_to(frow, (d, d)).T
            return carry
        lax.fori_loop(0, n_blk // group, prep, 0)
        for hd in range(nhd):
            vt_ref[hd, d:, :] = jnp.ones((ONES_ROWS, seq), BF16)

    m_ref[...] = jnp.full_like(m_ref, -1e30)
    acc_ref[...] = jnp.zeros_like(acc_ref)

    all_heads = tuple(range(nhd))

    def scores(kj, slot, heads=all_heads):
        off = pl.multiple_of(kj * tk, tk)
        for hd in heads:
            s_ref[slot, hd] = _dot_nt(k_ref[pl.ds(off, tk), hd_lanes(hd)], q_ref[:, hd_lanes(hd)])

    def consume(kj, slot, masked, heads=all_heads):
        off = pl.multiple_of(kj * tk, tk)
        for hd in heads:
            fcol = fcol_ref[hd, pl.ds(off, tk), :]
            fcol = jnp.concatenate([fcol] * (qt // d), axis=1)
            vt = vt_ref[hd, :, pl.ds(off, tk)]
            for t in range(tq // qt):
                lanes = slice(t * qt, (t + 1) * qt)
                nk = min((t + 1) * qt, tk) if masked else tk
                s = s_ref[slot, hd, 0:nk, lanes] * (scale * LOG2E) - fcol[0:nk]
                if masked:
                    key = lax.broadcasted_iota(jnp.int32, (nk, qt), 0)
                    qry = lax.broadcasted_iota(jnp.int32, (nk, qt), 1) + t * qt
                    s = jnp.where(key <= qry, s, -1e30)
                m_prev = m_ref[hd, :, lanes]
                m_new = jnp.maximum(m_prev, _reduce_rows(s, jnp.maximum))
                p = jnp.exp2(s - m_new)
                alpha = jnp.exp2(m_prev - m_new)
                m_ref[hd, :, lanes] = m_new
                acc_ref[hd, :, lanes] = alpha * acc_ref[hd, :, lanes] + _dot(vt[:, 0:nk], p)

    scores(0, 0)

    def body(i, carry):
        kj = 2 * i
        for hd in all_heads:
            scores(kj + 1, 1, (hd,))
            consume(kj, 0, False, (hd,))
        for hd in all_heads:
            scores(kj + 2, 0, (hd,))
            consume(kj + 1, 1, False, (hd,))
        return carry

    lax.fori_loop(0, qi // 2, body, 0)

    @pl.when(qi % 2 == 0)
    def _():
        consume(qi, 0, True)

    @pl.when(qi % 2 == 1)
    def _():
        scores(qi, 1)
        consume(qi - 1, 0, False)
        consume(qi, 1, True)

    for hd in range(nhd):
        out = (acc_ref[hd, 0:d, :] / acc_ref[hd, d:d + 1, :]).T
        o_ref[:, hd_lanes(hd)] = out.astype(o_ref.dtype)


def _fox(proj, f_rows, batch, seq, base, hf):
    tq = _tile(seq, 512)
    nq = seq // tq
    nhd = next(n for n in (4, 2, 1) if hf % n == 0 and base % n == 0)
    w = nhd * HEAD_DIM
    b0, nblk = base // nhd, hf // nhd
    kern = functools.partial(_fox_kernel, scale=HEAD_DIM ** -0.5)
    return pl.pallas_call(
        kern,
        grid=(batch, nblk, nq),
        in_specs=[pl.BlockSpec((tq, w), lambda b, h, qi: (b * nq + qi, b0 + h)),
                  pl.BlockSpec((seq, w), lambda b, h, qi: (b, b0 + nblk + h)),
                  pl.BlockSpec((seq, w), lambda b, h, qi: (b, b0 + 2 * nblk + h)),
                  pl.BlockSpec((1, nhd, 1, seq), lambda b, h, qi: (b, h, 0, 0))],
        out_specs=pl.BlockSpec((tq, w), lambda b, h, qi: (b * nq + qi, h)),
        out_shape=jax.ShapeDtypeStruct((batch * seq, hf * HEAD_DIM), BF16),
        scratch_shapes=[pltpu.VMEM((nhd, HEAD_DIM + ONES_ROWS, seq), BF16),
                        pltpu.VMEM((nhd, seq, HEAD_DIM), F32),
                        pltpu.VMEM((2, nhd, tq, tq), F32),
                        pltpu.VMEM((nhd, 1, tq), F32),
                        pltpu.VMEM((nhd, HEAD_DIM + ONES_ROWS, tq), F32)],
        compiler_params=_params("parallel", "parallel", "arbitrary"),
        name="fox_attention",
    )(proj, proj, proj, f_rows)


MERGE_ROWS = 512


def _merge_kernel(a1_ref, w1_ref, a2_ref, w2_ref, m1_ref, m2_ref, o_ref):
    tm = o_ref.shape[0]
    for r0 in range(0, tm, MERGE_ROWS):
        rows = slice(r0, min(r0 + MERGE_ROWS, tm))
        y1 = _dot(a1_ref[rows, :], w1_ref[...])
        y2 = _dot(a2_ref[rows, :], w2_ref[...])
        g1 = _sigmoid(m1_ref[rows, :].astype(F32))
        g2 = _sigmoid(m2_ref[rows, :].astype(F32))
        o_ref[rows, :] = (g1 * y1 + g2 * y2).astype(o_ref.dtype)


def _merge(o_dn, w_dn, o_fox, w_fox, layer, proj, merge_base_cols):
    m, k1 = o_dn.shape
    k2 = o_fox.shape[1]
    d = w_dn.shape[2]
    tm, tn = _tile(m, 1024), _tile(d, 512)
    assert merge_base_cols % tn == 0
    mb = merge_base_cols // tn
    return pl.pallas_call(
        _merge_kernel,
        grid=(m // tm, d // tn),
        in_specs=[pl.BlockSpec((tm, k1), lambda i, j: (i, 0)),
                  pl.BlockSpec((None, k1, tn), lambda i, j: (layer, 0, j)),
                  pl.BlockSpec((tm, k2), lambda i, j: (i, 0)),
                  pl.BlockSpec((None, k2, tn), lambda i, j: (layer, 0, j)),
                  pl.BlockSpec((tm, tn), lambda i, j: (i, mb + j)),
                  pl.BlockSpec((tm, tn), lambda i, j: (i, mb + d // tn + j))],
        out_specs=pl.BlockSpec((tm, tn), lambda i, j: (i, j)),
        out_shape=jax.ShapeDtypeStruct((m, d), BF16),
        compiler_params=pltpu.CompilerParams(dimension_semantics=("parallel", "parallel"),
                                             vmem_limit_bytes=IN_PROJ_VMEM_BYTES),
        name="branch_merge",
    )(o_dn, w_dn, o_fox, w_fox, proj, proj)


def _cast_weight_once(w_ref, wb_ref):
    @pl.when(pl.program_id(1) == 0)
    def _():
        k = w_ref.shape[0]
        for r0 in range(0, k, W_ROWS):
            r1 = min(r0 + W_ROWS, k)
            wb_ref[r0:r1, :] = w_ref[r0:r1, :].astype(BF16)


def _mm_f32_kernel(a_ref, w_ref, o_ref, wb_ref):
    _cast_weight_once(w_ref, wb_ref)
    o_ref[...] = _dot(a_ref[...], wb_ref[...])


def _out_proj(a, w, layer):
    m, k = a.shape
    d = w.shape[2]
    pref = 512 if k > 4096 else 1024
    tm, tn = _tile(m, pref), _tile(d, pref)
    return pl.pallas_call(
        _mm_f32_kernel,
        grid=(d // tn, m // tm),
        in_specs=[pl.BlockSpec((tm, k), lambda j, i: (i, 0)),
                  pl.BlockSpec((None, k, tn), lambda j, i: (layer, 0, j))],
        out_specs=pl.BlockSpec((tm, tn), lambda j, i: (i, j)),
        out_shape=jax.ShapeDtypeStruct((m, d), F32),
        scratch_shapes=[pltpu.VMEM((k, tn), BF16)],
        compiler_params=_params("parallel", "arbitrary"),
        name="out_proj",
    )(a, w)


def _res_norm_kernel(y_ref, x_ref, g_ref, gate_ref, *rest, emit_h):
    x_new = x_ref[...] + gate_ref[0] * (_rms(y_ref[...]) * g_ref[...])
    if emit_h:
        g2_ref, sc_ref, sh_ref, o_ref, h_ref = rest
        o_ref[...] = x_new
        h_ref[...] = ((_rms(x_new) * g2_ref[...]) * (1.0 + sc_ref[0]) + sh_ref[0]).astype(h_ref.dtype)
    else:
        (o_ref,) = rest
        o_ref[...] = x_new


def _residual_norm(y, x2, gain, mod3, gate_idx, seq, nxt=None):
    m, d = x2.shape
    tm = _tile(seq, 256)
    per_b = seq // tm
    row = pl.BlockSpec((tm, d), lambda i: (i, 0))
    vec = pl.BlockSpec((1, d), lambda i: (0, 0))
    modv = lambda idx: pl.BlockSpec((1, 1, d), lambda i: ((i // per_b) * 6 + idx, 0, 0))
    in_specs = [row, row, vec, modv(gate_idx)]
    args = [y, x2, gain.reshape(1, d), mod3]
    out_specs, out_shape = [row], [jax.ShapeDtypeStruct((m, d), F32)]
    if nxt is not None:
        gain2, mod3n, sc_idx, sh_idx = nxt
        in_specs += [vec, modv(sc_idx), modv(sh_idx)]
        args += [gain2.reshape(1, d), mod3n, mod3n]
        out_specs.append(row)
        out_shape.append(jax.ShapeDtypeStruct((m, d), BF16))
    out = pl.pallas_call(
        functools.partial(_res_norm_kernel, emit_h=nxt is not None),
        grid=(m // tm,),
        in_specs=in_specs,
        out_specs=out_specs,
        out_shape=out_shape,
        compiler_params=_params("parallel"),
        name="residual_norm",
    )(*args)
    return (out[0], out[1]) if nxt is not None else (out[0], None)


def _glu_kernel(a_ref, wg_ref, wu_ref, o_ref, wgb_ref, wub_ref):
    _cast_weight_once(wg_ref, wgb_ref)
    _cast_weight_once(wu_ref, wub_ref)
    tm = a_ref.shape[0]
    for r0 in range(0, tm, IN_PROJ_ROWS):
        rows = slice(r0, min(r0 + IN_PROJ_ROWS, tm))
        g = _dot(a_ref[rows, :], wgb_ref[...])
        u = _dot(a_ref[rows, :], wub_ref[...])
        o_ref[rows, :] = (g * _sigmoid(g) * u).astype(o_ref.dtype)


def _glu(a, wg, wu, layer):
    m, k = a.shape
    n = wg.shape[2]
    tm, tn = _tile(m, 2048), _tile(n, 512)
    return pl.pallas_call(
        _glu_kernel,
        grid=(n // tn, m // tm),
        in_specs=[pl.BlockSpec((tm, k), lambda j, i: (i, 0)),
                  pl.BlockSpec((None, k, tn), lambda j, i: (layer, 0, j)),
                  pl.BlockSpec((None, k, tn), lambda j, i: (layer, 0, j))],
        out_specs=pl.BlockSpec((tm, tn), lambda j, i: (i, j)),
        out_shape=jax.ShapeDtypeStruct((m, n), BF16),
        scratch_shapes=[pltpu.VMEM((k, tn), BF16), pltpu.VMEM((k, tn), BF16)],
        compiler_params=pltpu.CompilerParams(dimension_semantics=("parallel", "arbitrary"),
                                             vmem_limit_bytes=IN_PROJ_VMEM_BYTES),
        name="swiglu_up",
    )(a, wg, wu)


def kernel(x, c, w_ada, b_ada, norm_gains, w_in, dn_conv, dn_a_log, dn_dt_bias, dn_norm_w, fox_f_bias,
           w_branch_dn, w_branch_fox, w_out, w_gate, w_up, w_down):
    batch, seq, d = x.shape
    depth = w_ada.shape[0]
    hv = dn_a_log.shape[1]
    hf = fox_f_bias.shape[1]
    v_dim = hv * HEAD_DIM
    conv_dim = dn_conv.shape[1]
    qk_dim = (conv_dim - v_dim) // 2
    hqk = qk_dim // HEAD_DIM
    fox_dim = hf * HEAD_DIM
    assert 2 * hv + hf <= GATE_LANES

    o_z = conv_dim
    o_b = o_z + v_dim
    o_a = o_b + hv
    o_fq = o_a + hv
    o_ff = o_fq + 3 * fox_dim
    o_mg = o_ff + hf
    fox_base = (conv_dim + v_dim) // HEAD_DIM
    merge_base = conv_dim + v_dim + 3 * fox_dim

    mod = _modulation(c, w_ada, b_ada)
    x2 = x.reshape(batch * seq, d)
    pad = GATE_LANES - (2 * hv + hf)
    zpad = jnp.zeros((pad,), F32)

    main_cols = ((0, o_b), (o_fq, 3 * fox_dim), (o_mg, w_in.shape[2] - o_mg))
    w_in_t = jnp.swapaxes(w_in, 1, 2)
    w_dn_all, w_fox_all = w_branch_dn.astype(BF16), w_branch_fox.astype(BF16)
    mod3s = [mod[l].reshape(batch * 6, 1, d) for l in range(depth)]

    h = _norm_mod(x2, norm_gains[0, 0], mod3s[0], 1, 0, seq)
    for l in range(depth):
        mod3 = mod3s[l]
        bias = jnp.concatenate([jnp.zeros((hv,), F32), dn_dt_bias[l], fox_f_bias[l], zpad])
        mult = jnp.concatenate([jnp.ones((hv,), F32), -jnp.exp(dn_a_log[l]), jnp.ones((hf,), F32), zpad])
        pcol = jnp.stack([bias, mult], axis=0)

        proj = _in_proj(h, w_in_t, l, main_cols)
        rows = _gates(h, w_in_t, l, o_b, o_ff, pcol, batch, seq, hv, hf)
        o_dn = _deltanet(proj, dn_conv[l].T, rows, dn_norm_w[l], batch, seq, hqk, hv)
        f_rows = rows[:, 2 * hv:2 * hv + hf, :].reshape(batch, hf, 1, seq)
        o_fox = _fox(proj, f_rows, batch, seq, fox_base, hf)
        ymix = _merge(o_dn, w_dn_all, o_fox, w_fox_all, l, proj, merge_base)
        y = _out_proj(ymix, w_out, l)
        x2, h = _residual_norm(y, x2, norm_gains[l, 1], mod3, 2, seq,
                               nxt=(norm_gains[l, 2], mod3, 4, 3))

        gu = _glu(h, w_gate, w_up, l)
        y = _out_proj(gu, w_down, l)
        nxt = (norm_gains[l + 1, 0], mod3s[l + 1], 1, 0) if l + 1 < depth else None
        x2, h = _residual_norm(y, x2, norm_gains[l, 3], mod3, 5, seq, nxt=nxt)

    return x2.reshape(batch, seq, d)
```

```python
import functools

import jax
import jax.numpy as jnp
from jax import lax
from jax.experimental import pallas as pl
from jax.experimental.pallas import tpu as pltpu

EPS = 1e-6
HEAD_DIM = 128
CONV_K = 4
DN_CHUNK = 128
GATE_LANES = 128
VMEM_LIMIT_BYTES = 48 * 1024 * 1024

F32 = jnp.float32
BF16 = jnp.bfloat16


def _params(*sem):
    return pltpu.CompilerParams(dimension_semantics=sem, vmem_limit_bytes=VMEM_LIMIT_BYTES)


def _tile(dim, pref):
    if dim <= pref:
        return dim
    t = pref - pref % HEAD_DIM
    while dim % t:
        t -= HEAD_DIM
    assert t > 0, (dim, pref)
    return t


def _dot(a, b):
    return jnp.dot(a.astype(BF16), b.astype(BF16), preferred_element_type=F32)


def _dot_nt(a, b):
    return lax.dot_general(a.astype(BF16), b.astype(BF16), (((1,), (1,)), ((), ())),
                           preferred_element_type=F32)


def _sigmoid(x):
    return 0.5 * jnp.tanh(0.5 * x) + 0.5


def _rms(x):
    return x * lax.rsqrt(jnp.mean(x * x, axis=-1, keepdims=True) + EPS)


def _mod_kernel(c_ref, w_ref, b_ref, o_ref):
    c = c_ref[...]
    cond = c * _sigmoid(c)
    o_ref[0] = _dot(cond, w_ref[0]) + b_ref[0]


def _modulation(c, w_ada, b_ada):
    depth, d, n = w_ada.shape
    b = c.shape[0]
    rows = 16
    c_pad = jnp.zeros((rows, d), F32).at[:b].set(c)
    tn = _tile(n, 1024)
    out = pl.pallas_call(
        _mod_kernel,
        grid=(depth, n // tn),
        in_specs=[pl.BlockSpec((rows, d), lambda l, j: (0, 0)),
                  pl.BlockSpec((1, d, tn), lambda l, j: (l, 0, j)),
                  pl.BlockSpec((1, 1, tn), lambda l, j: (l, 0, j))],
        out_specs=pl.BlockSpec((1, rows, tn), lambda l, j: (l, 0, j)),
        out_shape=jax.ShapeDtypeStruct((depth, rows, n), F32),
        compiler_params=_params("parallel", "parallel"),
        name="adaln_mod",
    )(c_pad, w_ada, b_ada.reshape(depth, 1, n))
    return out[:, :b].reshape(depth, b, 6, d)


def _norm_mod_kernel(x_ref, g_ref, sc_ref, sh_ref, o_ref):
    y = _rms(x_ref[...]) * g_ref[...]
    o_ref[...] = (y * (1.0 + sc_ref[0]) + sh_ref[0]).astype(o_ref.dtype)


def _norm_mod(x2, gain, mod3, sc_idx, sh_idx, seq):
    m, d = x2.shape
    tm = _tile(seq, 512)
    per_b = seq // tm
    return pl.pallas_call(
        _norm_mod_kernel,
        grid=(m // tm,),
        in_specs=[pl.BlockSpec((tm, d), lambda i: (i, 0)),
                  pl.BlockSpec((1, d), lambda i: (0, 0)),
                  pl.BlockSpec((1, 1, d), lambda i: ((i // per_b) * 6 + sc_idx, 0, 0)),
                  pl.BlockSpec((1, 1, d), lambda i: ((i // per_b) * 6 + sh_idx, 0, 0))],
        out_specs=pl.BlockSpec((tm, d), lambda i: (i, 0)),
        out_shape=jax.ShapeDtypeStruct((m, d), BF16),
        compiler_params=_params("parallel"),
        name="norm_mod",
    )(x2, gain.reshape(1, d), mod3, mod3)


W_ROWS = 256
IN_PROJ_ROWS = 1024
IN_PROJ_VMEM_BYTES = 56 * 1024 * 1024


def _in_proj_kernel(a_ref, wa_ref, wb_ref, o_ref, w_ref, *, segments):
    j = pl.program_id(0)
    tn = wa_ref.shape[0]

    @pl.when(pl.program_id(1) == 0)
    def _():
        for lo, hi, shift in segments:
            @pl.when((j >= lo) & (j < hi))
            def _():
                for r0 in range(0, tn - shift, W_ROWS):
                    r1 = min(r0 + W_ROWS, tn - shift)
                    w_ref[r0:r1, :] = wa_ref[r0 + shift:r1 + shift, :].astype(BF16)
                if shift:
                    w_ref[tn - shift:, :] = wb_ref[:shift, :].astype(BF16)

    tm = a_ref.shape[0]
    for r0 in range(0, tm, IN_PROJ_ROWS):
        r1 = min(r0 + IN_PROJ_ROWS, tm)
        o_ref[r0:r1, :] = _dot_nt(a_ref[r0:r1, :], w_ref[...]).astype(o_ref.dtype)


def _in_proj(a, w_in_t, layer, seg_cols):
    m, k = a.shape
    n_out = sum(w for _, w in seg_cols)
    tn = 1024
    while any(w % tn for _, w in seg_cols):
        tn //= 2
    assert tn >= HEAD_DIM
    tm = _tile(m, 2048)
    segments, out0 = [], 0
    for src0, width in seg_cols:
        shift = src0 - out0
        assert 0 <= shift < HEAD_DIM
        segments.append((out0 // tn, (out0 + width) // tn, shift))
        out0 += width
    kern = functools.partial(_in_proj_kernel, segments=tuple(segments))
    nb = tn // HEAD_DIM
    last_blk = (w_in_t.shape[1] - 1) // HEAD_DIM
    return pl.pallas_call(
        kern,
        grid=(n_out // tn, m // tm),
        in_specs=[pl.BlockSpec((tm, k), lambda j, i: (i, 0)),
                  pl.BlockSpec((None, tn, k), lambda j, i: (layer, j, 0)),
                  pl.BlockSpec((None, HEAD_DIM, k),
                               lambda j, i: (layer, jnp.minimum((j + 1) * nb, last_blk), 0))],
        out_specs=pl.BlockSpec((tm, tn), lambda j, i: (i, j)),
        out_shape=jax.ShapeDtypeStruct((m, n_out), BF16),
        scratch_shapes=[pltpu.VMEM((tn, k), BF16)],
        compiler_params=pltpu.CompilerParams(dimension_semantics=("parallel", "arbitrary"),
                                             vmem_limit_bytes=IN_PROJ_VMEM_BYTES),
        name="in_proj",
    )(a, w_in_t, w_in_t)


def _split3(x):
    hi = x.astype(BF16)
    r1 = x - hi.astype(F32)
    mid = r1.astype(BF16)
    lo = (r1 - mid.astype(F32)).astype(BF16)
    return hi, mid, lo


def _gate_act(x, bias, mult, idx, hv, hf):
    xb = x + bias
    e = jnp.exp(-jnp.abs(xb))
    l1p = jnp.log(1.0 + e)
    sig = jnp.where(xb >= 0, 1.0, e) / (1.0 + e)
    softplus = jnp.maximum(xb, 0.0) + l1p
    logsig = jnp.minimum(xb, 0.0) - l1p
    return jnp.where(idx < hv, sig,
                     jnp.where(idx < 2 * hv, mult * softplus,
                               jnp.where(idx < 2 * hv + hf, logsig, 0.0)))


def _gates_kernel(h_ref, wa_ref, wb_ref, pcol_ref, rows_ref, carry_ref, *, hv, hf, chunk):
    t = pl.program_id(1)
    tm = h_ref.shape[0]

    @pl.when(t == 0)
    def _():
        carry_ref[...] = jnp.zeros_like(carry_ref)

    row_w = lax.broadcasted_iota(jnp.int32, wa_ref.shape, 0)
    w = jnp.where(row_w < 2 * hv, wa_ref[...], jnp.where(row_w < 2 * hv + hf, wb_ref[...], 0.0))
    g_cols = _dot_nt(h_ref[...], w)

    lane = lax.broadcasted_iota(jnp.int32, (tm, GATE_LANES), 1)
    val_c = _gate_act(g_cols, pcol_ref[0:1, :], pcol_ref[1:2, :], lane, hv, hf)
    sub = lax.broadcasted_iota(jnp.int32, (GATE_LANES, tm), 0)
    val_r = val_c.T

    r = lax.broadcasted_iota(jnp.int32, (tm, tm), 0)
    c = lax.broadcasted_iota(jnp.int32, (tm, tm), 1)
    sh = chunk.bit_length() - 1
    same = (r >> sh) == (c >> sh)
    triu_blk = jnp.where((r <= c) & same, 1.0, 0.0).astype(BF16)
    triu_all = jnp.where(r <= c, 1.0, 0.0).astype(BF16)

    pieces_r = _split3(val_r)
    cum_r_blk = sum(jnp.dot(p, triu_blk, preferred_element_type=F32) for p in pieces_r)
    cum_r_all = sum(jnp.dot(p, triu_all, preferred_element_type=F32) for p in pieces_r)
    cum_r_all = cum_r_all + carry_ref[:, 0:1]

    is_decay_r = (sub >= hv) & (sub < 2 * hv)
    is_forget_r = (sub >= 2 * hv) & (sub < 2 * hv + hf)
    rows = jnp.where(is_decay_r, cum_r_blk, jnp.where(is_forget_r, cum_r_all, val_r))
    rows_ref[0] = rows
    carry_ref[...] = jnp.broadcast_to(cum_r_all[:, tm - 1:tm], carry_ref.shape)


def _gates(h2, w_in, layer, col_ba, col_f, pcol, batch, seq, hv, hf):
    m, d = h2.shape
    tm = _tile(seq, 512)
    per_b = seq // tm
    assert col_ba % GATE_LANES == 0 and col_f % GATE_LANES == 2 * hv
    blk_ba, blk_f = col_ba // GATE_LANES, col_f // GATE_LANES
    kern = functools.partial(_gates_kernel, hv=hv, hf=hf, chunk=DN_CHUNK)
    return pl.pallas_call(
        kern,
        grid=(batch, per_b),
        in_specs=[pl.BlockSpec((tm, d), lambda b, t: (b * per_b + t, 0)),
                  pl.BlockSpec((None, GATE_LANES, d), lambda b, t: (layer, blk_ba, 0)),
                  pl.BlockSpec((None, GATE_LANES, d), lambda b, t: (layer, blk_f, 0)),
                  pl.BlockSpec((2, GATE_LANES), lambda b, t: (0, 0))],
        out_specs=pl.BlockSpec((1, GATE_LANES, tm), lambda b, t: (b, 0, t)),
        out_shape=jax.ShapeDtypeStruct((batch, GATE_LANES, seq), F32),
        scratch_shapes=[pltpu.VMEM((GATE_LANES, GATE_LANES), F32)],
        compiler_params=_params("parallel", "arbitrary"),
        name="gates",
    )(h2, w_in, w_in, pcol)


def _causal_conv(u, tail, w):
    out = u * w[CONV_K - 1:CONV_K, :]
    row8 = lax.broadcasted_iota(jnp.int32, tail.shape, 0)
    for s in range(1, CONV_K):
        rolled = pltpu.roll(u, s, 0)
        head = jnp.where(row8 < s, pltpu.roll(tail, s, 0), rolled[0:8])
        shifted = jnp.concatenate([head, rolled[8:]], axis=0)
        out = out + shifted * w[CONV_K - 1 - s:CONV_K - s, :]
    return out


INV_LEVELS = (DN_CHUNK // 8).bit_length() - 1
MASK_DIAG8, MASK_EYE, MASK_STRICT = 0, INV_LEVELS + 1, INV_LEVELS + 2
N_MASKS = INV_LEVELS + 3


def _inverse_masks(n):
    r = lax.broadcasted_iota(jnp.int32, (n, n), 0)
    c = lax.broadcasted_iota(jnp.int32, (n, n), 1)
    masks = [(r >> 3) == (c >> 3)]
    for sh in range(3, 3 + INV_LEVELS):
        masks.append(((r >> (sh + 1)) == (c >> (sh + 1))) & ((r >> sh) == (c >> sh) + 1))
    masks += [r == c, r > c]
    return [jnp.where(m, 1.0, 0.0) for m in masks]


def _unit_lower_inverses(lms, mask_ref, maskb_ref):
    n0s = [-(lm * mask_ref[MASK_DIAG8]) for lm in lms]
    n2s = [_dot(n0, n0) for n0 in n0s]
    n4s = [_dot(n2, n2) for n2 in n2s]
    ps = [mask_ref[MASK_EYE] + n0 for n0 in n0s]
    ps = [p + _dot(p, n2) for p, n2 in zip(ps, n2s)]
    ps = [p + _dot(p, n4) for p, n4 in zip(ps, n4s)]
    lms_b = [lm.astype(BF16) for lm in lms]
    for level in range(1, INV_LEVELS + 1):
        xs = [_dot(lm_b * maskb_ref[level], p) for lm_b, p in zip(lms_b, ps)]
        ps = [p - _dot(p, x) for p, x in zip(ps, xs)]
    return ps


def _deltanet_kernel(q_ref, k_ref, v_ref, z_ref, wq_ref, wk_ref, wv_ref, rows_ref, nw_ref,
                     o_ref, s_ref, qt_ref, kt_ref, vt_ref, mask_ref, maskb_ref, u_ref, lhs1_ref, lhs2_ref,
                     egl_ref,
                     *, hv, rep, groups, unroll, scan_unroll):
    i = pl.program_id(1)
    tb = pl.program_id(2)
    rows_blk = q_ref.shape[0]
    n_chunks = rows_blk // DN_CHUNK
    c_ = DN_CHUNK
    d = HEAD_DIM
    nh = groups * rep

    @pl.when(tb == 0)
    def _():
        s_ref[...] = jnp.zeros_like(s_ref)
        qt_ref[...] = jnp.zeros_like(qt_ref)
        kt_ref[...] = jnp.zeros_like(kt_ref)
        vt_ref[...] = jnp.zeros_like(vt_ref)
        for j, m in enumerate(_inverse_masks(c_)):
            mask_ref[j] = m
            maskb_ref[j] = m.astype(BF16)

    r = lax.broadcasted_iota(jnp.int32, (c_, c_), 0)
    c = lax.broadcasted_iota(jnp.int32, (c_, c_), 1)
    nw = nw_ref[...]

    def tail_of(ref, tail_ref, ci, off):
        prev = pl.multiple_of(jnp.maximum(off - 16, 0), 16)
        inside = ref[pl.ds(prev, 16), :].astype(F32)[8:16]
        return jnp.where(ci == 0, tail_ref[...], inside)

    def prepare_body(j, carry):
        chains = []
        for uu in range(unroll):
            ci = j * unroll + uu
            off = pl.multiple_of(ci * c_, c_)
            sl = pl.ds(off, c_)
            q_raw = q_ref[sl, :].astype(F32)
            k_raw = k_ref[sl, :].astype(F32)
            v_raw = v_ref[sl, :].astype(F32)
            qc = _causal_conv(q_raw, tail_of(q_ref, qt_ref, ci, off), wq_ref[...])
            kc = _causal_conv(k_raw, tail_of(k_ref, kt_ref, ci, off), wk_ref[...])
            vc = _causal_conv(v_raw, tail_of(v_ref, vt_ref, ci, off), wv_ref[...])
            qc = qc * _sigmoid(qc)
            kc = kc * _sigmoid(kc)
            vc = vc * _sigmoid(vc)
            row_b = pl.multiple_of(((i * nh) // 8) * 8, 8)
            row_g = pl.multiple_of(((hv + i * nh) // 8) * 8, 8)
            rows8_b = rows_ref[0, pl.ds(row_b, 8), sl]
            rows8_g = rows_ref[0, pl.ds(row_g, 8), sl]
            sub8 = lax.broadcasted_iota(jnp.int32, (8, c_), 0)
            for g in range(groups):
                qg_ = qc[:, g * d:(g + 1) * d]
                kg_ = kc[:, g * d:(g + 1) * d]
                qn = qg_ * (lax.rsqrt(jnp.sum(qg_ * qg_, axis=-1, keepdims=True) + EPS) * (d ** -0.5))
                kn = kg_ * lax.rsqrt(jnp.sum(kg_ * kg_, axis=-1, keepdims=True) + EPS)
                kk = _dot_nt(kn, kn)
                qk = _dot_nt(qn, kn)
                for hh in range(rep):
                    hl = g * rep + hh
                    head = i * nh + hl
                    beta_r = jnp.sum(jnp.where(sub8 == head % 8, rows8_b, 0.0), axis=0, keepdims=True)
                    gc_r = jnp.sum(jnp.where(sub8 == (hv + head) % 8, rows8_g, 0.0), axis=0, keepdims=True)
                    beta_c = jnp.broadcast_to(beta_r, (c_, c_)).T
                    gc_c = jnp.broadcast_to(gc_r, (c_, c_)).T
                    g_last = gc_r[:, c_ - 1:c_]
                    decay = jnp.exp(jnp.where(r >= c, gc_c - gc_r, -1e30))
                    lm = (beta_c * kk) * (decay * mask_ref[MASK_STRICT])
                    eg = jnp.exp(gc_c)
                    v_h = vc[:, hl * d:(hl + 1) * d]
                    rhs = jnp.concatenate([v_h * beta_c, kn * (beta_c * eg)], axis=1).astype(BF16)
                    kd = kn * jnp.exp(g_last - gc_c)
                    lhs2_ref[hl, ci] = jnp.concatenate([qk * decay, kd.T], axis=0).astype(BF16)
                    egl_ref[hl, ci] = jnp.broadcast_to(jnp.exp(g_last), (8, d))
                    chains.append((hl, ci, sl, lm, rhs, (qn * eg).astype(BF16)))
        tinvs = _unit_lower_inverses([ch[3] for ch in chains], mask_ref, maskb_ref)
        uws = [_dot(tinv, ch[4]) for tinv, ch in zip(tinvs, chains)]
        for uw, (hl, ci, sl, _, _, qg) in zip(uws, chains):
            u_ref[hl, sl, :] = uw[:, :d]
            lhs1_ref[hl, ci] = jnp.concatenate([uw[:, d:].astype(BF16), qg], axis=0)
        return carry

    lax.fori_loop(0, n_chunks // unroll, prepare_body, 0)

    heads = range(nh)

    def scan_chunk(ci):
        off = pl.multiple_of(ci * c_, c_)
        sl = pl.ds(off, c_)
        states = [s_ref[hl] for hl in heads]
        ws_qs = [_dot(lhs1_ref[hl, ci], states[hl]) for hl in heads]
        v_new = [u_ref[hl, sl, :] - ws_qs[hl][:c_] for hl in heads]
        av_kv = [_dot(lhs2_ref[hl, ci], v_new[hl]) for hl in heads]
        for hl in heads:
            s_ref[hl] = states[hl] * egl_ref[hl, ci, 0:1, :] + av_kv[hl][c_:]
        for hl in heads:
            o = ws_qs[hl][c_:] + av_kv[hl][:c_]
            z = z_ref[sl, hl * d:(hl + 1) * d].astype(F32)
            out = _rms(o) * nw * (z * _sigmoid(z))
            o_ref[sl, hl * d:(hl + 1) * d] = out.astype(o_ref.dtype)

    def scan_body(j, carry):
        for uu in range(scan_unroll):
            scan_chunk(j * scan_unroll + uu)
        return carry

    lax.fori_loop(0, n_chunks // scan_unroll, scan_body, 0)

    qt_ref[...] = q_ref[rows_blk - 16:, :].astype(F32)[8:16]
    kt_ref[...] = k_ref[rows_blk - 16:, :].astype(F32)[8:16]
    vt_ref[...] = v_ref[rows_blk - 16:, :].astype(F32)[8:16]


def _deltanet(proj, conv_wt, rows, norm_w, batch, seq, hqk, hv):
    rep = hv // hqk
    assert hv == rep * hqk and DN_CHUNK == HEAD_DIM
    groups = next(g for g in (4, 2, 1) if hqk % g == 0 and 8 % (g * rep) == 0)
    nh = groups * rep
    assert 8 % nh == 0 and hv % nh == 0 and (2 * hqk) % nh == 0
    tb = _tile(seq, 1024)
    per_b = seq // tb
    wqk, wv = groups * HEAD_DIM, nh * HEAD_DIM
    nqk_blk, nv_blk = hqk // groups, hv // nh
    k_blk0 = nqk_blk
    v_blk0 = (2 * hqk) // nh
    z_blk0 = v_blk0 + nv_blk
    row = lambda b, i, t: b * per_b + t
    n_chunks = tb // DN_CHUNK
    unroll = next(u for u in (4, 2, 1) if n_chunks % u == 0 and u * nh <= 16)
    scan_unroll = next(u for u in (4, 2, 1) if n_chunks % u == 0)
    kern = functools.partial(_deltanet_kernel, hv=hv, rep=rep, groups=groups, unroll=unroll,
                             scan_unroll=scan_unroll)
    return pl.pallas_call(
        kern,
        grid=(batch, nqk_blk, per_b),
        in_specs=[pl.BlockSpec((tb, wqk), lambda b, i, t: (row(b, i, t), i)),
                  pl.BlockSpec((tb, wqk), lambda b, i, t: (row(b, i, t), k_blk0 + i)),
                  pl.BlockSpec((tb, wv), lambda b, i, t: (row(b, i, t), v_blk0 + i)),
                  pl.BlockSpec((tb, wv), lambda b, i, t: (row(b, i, t), z_blk0 + i)),
                  pl.BlockSpec((CONV_K, wqk), lambda b, i, t: (0, i)),
                  pl.BlockSpec((CONV_K, wqk), lambda b, i, t: (0, k_blk0 + i)),
                  pl.BlockSpec((CONV_K, wv), lambda b, i, t: (0, v_blk0 + i)),
                  pl.BlockSpec((1, GATE_LANES, tb), lambda b, i, t: (b, 0, t)),
                  pl.BlockSpec((1, HEAD_DIM), lambda b, i, t: (0, 0))],
        out_specs=pl.BlockSpec((tb, wv), lambda b, i, t: (row(b, i, t), i)),
        out_shape=jax.ShapeDtypeStruct((batch * seq, hv * HEAD_DIM), BF16),
        scratch_shapes=[pltpu.VMEM((nh, HEAD_DIM, HEAD_DIM), F32),
                        pltpu.VMEM((8, wqk), F32),
                        pltpu.VMEM((8, wqk), F32),
                        pltpu.VMEM((8, wv), F32),
                        pltpu.VMEM((N_MASKS, DN_CHUNK, DN_CHUNK), F32),
                        pltpu.VMEM((N_MASKS, DN_CHUNK, DN_CHUNK), BF16),
                        pltpu.VMEM((nh, tb, HEAD_DIM), F32),
                        pltpu.VMEM((nh, n_chunks, 2 * DN_CHUNK, HEAD_DIM), BF16),
                        pltpu.VMEM((nh, n_chunks, 2 * DN_CHUNK, DN_CHUNK), BF16),
                        pltpu.VMEM((nh, n_chunks, 8, HEAD_DIM), F32)],
        compiler_params=_params("parallel", "parallel", "arbitrary"),
        name="deltanet",
    )(proj, proj, proj, proj, conv_wt, conv_wt, conv_wt, rows, norm_w.reshape(1, HEAD_DIM))


LOG2E = 1.4426950408889634
MXU_COLS = 256
ONES_ROWS = 16


def _reduce_rows(x, op):
    rows, lanes = x.shape
    slabs = 8
    if rows % (8 * slabs) == 0:
        x3 = x.reshape(slabs, rows // slabs, lanes)
        x = x3[0]
        for j in range(1, slabs):
            x = op(x, x3[j])
    final = jnp.max if op is jnp.maximum else jnp.sum
    return final(x, axis=0, keepdims=True)


def _fox_kernel(q_ref, k_ref, v_ref, f_ref, o_ref, vt_ref, fcol_ref, s_ref, m_ref, acc_ref, *, scale):
    qi = pl.program_id(2)
    tq = q_ref.shape[0]
    tk = tq // 2
    seq = k_ref.shape[0]
    qt = min(MXU_COLS, tq)
    d = HEAD_DIM
    nhd = q_ref.shape[1] // d
    hd_lanes = lambda hd: slice(hd * d, (hd + 1) * d)

    @pl.when(qi == 0)
    def _():
        n_blk = seq // d
        group = next(g for g in (4, 2, 1) if n_blk % g == 0)

        def prep(bi, carry):
            for j in range(group):
                off = pl.multiple_of((bi * group + j) * d, d)
                for hd in range(nhd):
                    v_blk = v_ref[pl.ds(off, d), hd_lanes(hd)]
                    vt_ref[hd, 0:d, pl.ds(off, d)] = v_blk.astype(F32).T.astype(BF16)
                    frow = f_ref[0, hd, :, pl.ds(off, d)] * LOG2E
                    fcol_ref[hd, pl.ds(off, d), :] = jnp.broadcast_to(frow, (d, d)).T
            return carry
        lax.fori_loop(0, n_blk // group, prep, 0)
        for hd in range(nhd):
            vt_ref[hd, d:, :] = jnp.ones((ONES_ROWS, seq), BF16)

    m_ref[...] = jnp.full_like(m_ref, -1e30)
    acc_ref[...] = jnp.zeros_like(acc_ref)

    all_heads = tuple(range(nhd))

    def scores(kj, slot, heads=all_heads):
        off = pl.multiple_of(kj * tk, tk)
        for hd in heads:
            s_ref[slot, hd] = _dot_nt(k_ref[pl.ds(off, tk), hd_lanes(hd)], q_ref[:, hd_lanes(hd)])

    def consume(kj, slot, diag, heads=all_heads):
        off = pl.multiple_of(kj * tk, tk)
        for hd in heads:
            fcol = fcol_ref[hd, pl.ds(off, tk), :]
            fcol = jnp.concatenate([fcol] * (qt // d), axis=1)
            vt = vt_ref[hd, :, pl.ds(off, tk)]
            for t in range(tq // qt):
                lanes = slice(t * qt, (t + 1) * qt)
                masked = False
                if diag is not None:
                    k0, q0 = diag * tk, t * qt
                    if k0 > q0 + qt - 1:
                        continue
                    masked = k0 + tk - 1 > q0
                s = s_ref[slot, hd, :, lanes] * (scale * LOG2E) - fcol
                if masked:
                    key = lax.broadcasted_iota(jnp.int32, (tk, qt), 0) + k0
                    qry = lax.broadcasted_iota(jnp.int32, (tk, qt), 1) + q0
                    s = jnp.where(key <= qry, s, -1e30)
                m_prev = m_ref[hd, :, lanes]
                m_new = jnp.maximum(m_prev, _reduce_rows(s, jnp.maximum))
                p = jnp.exp2(s - m_new)
                alpha = jnp.exp2(m_prev - m_new)
                m_ref[hd, :, lanes] = m_new
                acc_ref[hd, :, lanes] = alpha * acc_ref[hd, :, lanes] + _dot(vt, p)

    scores(0, 0)

    def body(i, carry):
        kj = 2 * i
        for hd in all_heads:
            scores(kj + 1, 1, (hd,))
            consume(kj, 0, None, (hd,))
        for hd in all_heads:
            scores(kj + 2, 0, (hd,))
            consume(kj + 1, 1, None, (hd,))
        return carry

    lax.fori_loop(0, qi, body, 0)
    for hd in all_heads:
        scores(2 * qi + 1, 1, (hd,))
        consume(2 * qi, 0, 0, (hd,))
    consume(2 * qi + 1, 1, 1)

    for hd in range(nhd):
        out = (acc_ref[hd, 0:d, :] / acc_ref[hd, d:d + 1, :]).T
        o_ref[:, hd_lanes(hd)] = out.astype(o_ref.dtype)


def _fox(proj, f_rows, batch, seq, base, hf):
    tq = _tile(seq, 512)
    nq = seq // tq
    nhd = next(n for n in (4, 2, 1) if hf % n == 0 and base % n == 0)
    w = nhd * HEAD_DIM
    b0, nblk = base // nhd, hf // nhd
    kern = functools.partial(_fox_kernel, scale=HEAD_DIM ** -0.5)
    return pl.pallas_call(
        kern,
        grid=(batch, nblk, nq),
        in_specs=[pl.BlockSpec((tq, w), lambda b, h, qi: (b * nq + qi, b0 + h)),
                  pl.BlockSpec((seq, w), lambda b, h, qi: (b, b0 + nblk + h)),
                  pl.BlockSpec((seq, w), lambda b, h, qi: (b, b0 + 2 * nblk + h)),
                  pl.BlockSpec((1, nhd, 1, seq), lambda b, h, qi: (b, h, 0, 0))],
        out_specs=pl.BlockSpec((tq, w), lambda b, h, qi: (b * nq + qi, h)),
        out_shape=jax.ShapeDtypeStruct((batch * seq, hf * HEAD_DIM), BF16),
        scratch_shapes=[pltpu.VMEM((nhd, HEAD_DIM + ONES_ROWS, seq), BF16),
                        pltpu.VMEM((nhd, seq, HEAD_DIM), F32),
                        pltpu.VMEM((2, nhd, tq // 2, tq), F32),
                        pltpu.VMEM((nhd, 1, tq), F32),
                        pltpu.VMEM((nhd, HEAD_DIM + ONES_ROWS, tq), F32)],
        compiler_params=_params("parallel", "parallel", "arbitrary"),
        name="fox_attention",
    )(proj, proj, proj, f_rows)


MERGE_ROWS = 512


def _merge_kernel(a1_ref, w1_ref, a2_ref, w2_ref, m1_ref, m2_ref, o_ref):
    tm = o_ref.shape[0]
    for r0 in range(0, tm, MERGE_ROWS):
        rows = slice(r0, min(r0 + MERGE_ROWS, tm))
        y1 = _dot(a1_ref[rows, :], w1_ref[...])
        y2 = _dot(a2_ref[rows, :], w2_ref[...])
        g1 = _sigmoid(m1_ref[rows, :].astype(F32))
        g2 = _sigmoid(m2_ref[rows, :].astype(F32))
        o_ref[rows, :] = (g1 * y1 + g2 * y2).astype(o_ref.dtype)


def _merge(o_dn, w_dn, o_fox, w_fox, layer, proj, merge_base_cols):
    m, k1 = o_dn.shape
    k2 = o_fox.shape[1]
    d = w_dn.shape[2]
    tm, tn = _tile(m, 1024), _tile(d, 512)
    assert merge_base_cols % tn == 0
    mb = merge_base_cols // tn
    return pl.pallas_call(
        _merge_kernel,
        grid=(m // tm, d // tn),
        in_specs=[pl.BlockSpec((tm, k1), lambda i, j: (i, 0)),
                  pl.BlockSpec((None, k1, tn), lambda i, j: (layer, 0, j)),
                  pl.BlockSpec((tm, k2), lambda i, j: (i, 0)),
                  pl.BlockSpec((None, k2, tn), lambda i, j: (layer, 0, j)),
                  pl.BlockSpec((tm, tn), lambda i, j: (i, mb + j)),
                  pl.BlockSpec((tm, tn), lambda i, j: (i, mb + d // tn + j))],
        out_specs=pl.BlockSpec((tm, tn), lambda i, j: (i, j)),
        out_shape=jax.ShapeDtypeStruct((m, d), BF16),
        compiler_params=pltpu.CompilerParams(dimension_semantics=("parallel", "parallel"),
                                             vmem_limit_bytes=IN_PROJ_VMEM_BYTES),
        name="branch_merge",
    )(o_dn, w_dn, o_fox, w_fox, proj, proj)


def _cast_weight_once(w_ref, wb_ref):
    @pl.when(pl.program_id(1) == 0)
    def _():
        k = w_ref.shape[0]
        for r0 in range(0, k, W_ROWS):
            r1 = min(r0 + W_ROWS, k)
            wb_ref[r0:r1, :] = w_ref[r0:r1, :].astype(BF16)


def _mm_f32_kernel(a_ref, w_ref, o_ref, wb_ref):
    _cast_weight_once(w_ref, wb_ref)
    o_ref[...] = _dot(a_ref[...], wb_ref[...])


def _out_proj(a, w, layer):
    m, k = a.shape
    d = w.shape[2]
    pref = 512 if k > 4096 else 1024
    tm, tn = _tile(m, pref), _tile(d, pref)
    return pl.pallas_call(
        _mm_f32_kernel,
        grid=(d // tn, m // tm),
        in_specs=[pl.BlockSpec((tm, k), lambda j, i: (i, 0)),
                  pl.BlockSpec((None, k, tn), lambda j, i: (layer, 0, j))],
        out_specs=pl.BlockSpec((tm, tn), lambda j, i: (i, j)),
        out_shape=jax.ShapeDtypeStruct((m, d), F32),
        scratch_shapes=[pltpu.VMEM((k, tn), BF16)],
        compiler_params=_params("parallel", "arbitrary"),
        name="out_proj",
    )(a, w)


def _res_norm_kernel(y_ref, x_ref, g_ref, gate_ref, *rest, emit_h):
    x_new = x_ref[...] + gate_ref[0] * (_rms(y_ref[...]) * g_ref[...])
    if emit_h:
        g2_ref, sc_ref, sh_ref, o_ref, h_ref = rest
        o_ref[...] = x_new
        h_ref[...] = ((_rms(x_new) * g2_ref[...]) * (1.0 + sc_ref[0]) + sh_ref[0]).astype(h_ref.dtype)
    else:
        (o_ref,) = rest
        o_ref[...] = x_new


def _residual_norm(y, x2, gain, mod3, gate_idx, seq, nxt=None):
    m, d = x2.shape
    tm = _tile(seq, 256)
    per_b = seq // tm
    row = pl.BlockSpec((tm, d), lambda i: (i, 0))
    vec = pl.BlockSpec((1, d), lambda i: (0, 0))
    modv = lambda idx: pl.BlockSpec((1, 1, d), lambda i: ((i // per_b) * 6 + idx, 0, 0))
    in_specs = [row, row, vec, modv(gate_idx)]
    args = [y, x2, gain.reshape(1, d), mod3]
    out_specs, out_shape = [row], [jax.ShapeDtypeStruct((m, d), F32)]
    if nxt is not None:
        gain2, mod3n, sc_idx, sh_idx = nxt
        in_specs += [vec, modv(sc_idx), modv(sh_idx)]
        args += [gain2.reshape(1, d), mod3n, mod3n]
        out_specs.append(row)
        out_shape.append(jax.ShapeDtypeStruct((m, d), BF16))
    out = pl.pallas_call(
        functools.partial(_res_norm_kernel, emit_h=nxt is not None),
        grid=(m // tm,),
        in_specs=in_specs,
        out_specs=out_specs,
        out_shape=out_shape,
        compiler_params=_params("parallel"),
        name="residual_norm",
    )(*args)
    return (out[0], out[1]) if nxt is not None else (out[0], None)


def _glu_kernel(a_ref, wg_ref, wu_ref, o_ref, wgb_ref, wub_ref):
    _cast_weight_once(wg_ref, wgb_ref)
    _cast_weight_once(wu_ref, wub_ref)
    tm = a_ref.shape[0]
    for r0 in range(0, tm, IN_PROJ_ROWS):
        rows = slice(r0, min(r0 + IN_PROJ_ROWS, tm))
        g = _dot(a_ref[rows, :], wgb_ref[...])
        u = _dot(a_ref[rows, :], wub_ref[...])
        o_ref[rows, :] = (g * _sigmoid(g) * u).astype(o_ref.dtype)


def _glu(a, wg, wu, layer):
    m, k = a.shape
    n = wg.shape[2]
    tm, tn = _tile(m, 2048), _tile(n, 512)
    return pl.pallas_call(
        _glu_kernel,
        grid=(n // tn, m // tm),
        in_specs=[pl.BlockSpec((tm, k), lambda j, i: (i, 0)),
                  pl.BlockSpec((None, k, tn), lambda j, i: (layer, 0, j)),
                  pl.BlockSpec((None, k, tn), lambda j, i: (layer, 0, j))],
        out_specs=pl.BlockSpec((tm, tn), lambda j, i: (i, j)),
        out_shape=jax.ShapeDtypeStruct((m, n), BF16),
        scratch_shapes=[pltpu.VMEM((k, tn), BF16), pltpu.VMEM((k, tn), BF16)],
        compiler_params=pltpu.CompilerParams(dimension_semantics=("parallel", "arbitrary"),
                                             vmem_limit_bytes=IN_PROJ_VMEM_BYTES),
        name="swiglu_up",
    )(a, wg, wu)


def kernel(x, c, w_ada, b_ada, norm_gains, w_in, dn_conv, dn_a_log, dn_dt_bias, dn_norm_w, fox_f_bias,
           w_branch_dn, w_branch_fox, w_out, w_gate, w_up, w_down):
    batch, seq, d = x.shape
    depth = w_ada.shape[0]
    hv = dn_a_log.shape[1]
    hf = fox_f_bias.shape[1]
    v_dim = hv * HEAD_DIM
    conv_dim = dn_conv.shape[1]
    qk_dim = (conv_dim - v_dim) // 2
    hqk = qk_dim // HEAD_DIM
    fox_dim = hf * HEAD_DIM
    assert 2 * hv + hf <= GATE_LANES

    o_z = conv_dim
    o_b = o_z + v_dim
    o_a = o_b + hv
    o_fq = o_a + hv
    o_ff = o_fq + 3 * fox_dim
    o_mg = o_ff + hf
    fox_base = (conv_dim + v_dim) // HEAD_DIM
    merge_base = conv_dim + v_dim + 3 * fox_dim

    mod = _modulation(c, w_ada, b_ada)
    x2 = x.reshape(batch * seq, d)
    pad = GATE_LANES - (2 * hv + hf)
    zpad = jnp.zeros((pad,), F32)

    main_cols = ((0, o_b), (o_fq, 3 * fox_dim), (o_mg, w_in.shape[2] - o_mg))
    w_in_t = jnp.swapaxes(w_in, 1, 2)
    w_dn_all, w_fox_all = w_branch_dn.astype(BF16), w_branch_fox.astype(BF16)
    mod3s = [mod[l].reshape(batch * 6, 1, d) for l in range(depth)]

    h = _norm_mod(x2, norm_gains[0, 0], mod3s[0], 1, 0, seq)
    for l in range(depth):
        mod3 = mod3s[l]
        bias = jnp.concatenate([jnp.zeros((hv,), F32), dn_dt_bias[l], fox_f_bias[l], zpad])
        mult = jnp.concatenate([jnp.ones((hv,), F32), -jnp.exp(dn_a_log[l]), jnp.ones((hf,), F32), zpad])
        pcol = jnp.stack([bias, mult], axis=0)

        proj = _in_proj(h, w_in_t, l, main_cols)
        rows = _gates(h, w_in_t, l, o_b, o_ff, pcol, batch, seq, hv, hf)
        o_dn = _deltanet(proj, dn_conv[l].T, rows, dn_norm_w[l], batch, seq, hqk, hv)
        f_rows = rows[:, 2 * hv:2 * hv + hf, :].reshape(batch, hf, 1, seq)
        o_fox = _fox(proj, f_rows, batch, seq, fox_base, hf)
        ymix = _merge(o_dn, w_dn_all, o_fox, w_fox_all, l, proj, merge_base)
        y = _out_proj(ymix, w_out, l)
        x2, h = _residual_norm(y, x2, norm_gains[l, 1], mod3, 2, seq,
                               nxt=(norm_gains[l, 2], mod3, 4, 3))

        gu = _glu(h, w_gate, w_up, l)
        y = _out_proj(gu, w_down, l)
        nxt = (norm_gains[l + 1, 0], mod3s[l + 1], 1, 0) if l + 1 < depth else None
        x2, h = _residual_norm(y, x2, norm_gains[l, 3], mod3, 5, seq, nxt=nxt)

    return x2.reshape(batch, seq, d)
```
